```python
import math
import jax, jax.numpy as jnp
from jax import lax
import numpy as np

D_MODEL = 1024
BATCH = 8
SEQ = 8192
DEPTH = 4

HEAD_DIM = 64
N_HEADS_TOTAL = D_MODEL // HEAD_DIM
N_HEADS_A = N_HEADS_TOTAL // 2
N_HEADS_B = N_HEADS_TOTAL // 4
N_HEADS_C = N_HEADS_TOTAL - N_HEADS_A - N_HEADS_B
DIFF_HALF = HEAD_DIM // 2
WIDTH_A = N_HEADS_A * HEAD_DIM
WIDTH_B = N_HEADS_B * HEAD_DIM
WIDTH_C = N_HEADS_C * HEAD_DIM
MIX_WIDTH = WIDTH_A + WIDTH_B + WIDTH_C
SPLITS = [int(c) for c in np.cumsum([WIDTH_A] * 3 + [WIDTH_B] * 3 + [WIDTH_C] * 2)]
D_FF = D_MODEL
DILATED_BRANCHES = ((128, 1), (512, 4), (2048, 16))
Q_BLOCK = 128
N_BUCKETS = 32
MAX_DISTANCE = 2048
N_BIAS_HEADS = N_HEADS_A + N_HEADS_B
EPS = 1e-6
NEG_INF = -1e30
SB_MASK = 1e4

kernel_name = 'hybrid_dilated_diff_stickbreak_macaron'


def rms_norm(x, gain):
    xf = x.astype(jnp.float32)
    y = xf * lax.rsqrt(jnp.mean(xf * xf, axis=-1, keepdims=True) + EPS)
    return (y * gain.astype(jnp.float32)).astype(x.dtype)


def swiglu(x, w_gate, w_up, w_down):
    return (jax.nn.silu(x @ w_gate) * (x @ w_up)) @ w_down


def t5_bucket(dist):
    dist = jnp.maximum(dist, 0)
    max_exact = N_BUCKETS // 2
    d_f = jnp.maximum(dist, 1).astype(jnp.float32)
    large = max_exact + (jnp.log(d_f / max_exact) / math.log(MAX_DISTANCE / max_exact)
                         * (N_BUCKETS - max_exact)).astype(jnp.int32)
    large = jnp.minimum(large, N_BUCKETS - 1)
    return jnp.where(dist < max_exact, dist, large)


def split_heads(t, n_heads, dh):
    b, s, _ = t.shape
    return t.reshape(b, s, n_heads, dh).transpose(0, 2, 1, 3)


def merge_heads(t):
    b, h, s, dh = t.shape
    return t.transpose(0, 2, 1, 3).reshape(b, s, h * dh)


def dilated_bias_masks(bias_a, seq):
    out = []
    i = jnp.arange(Q_BLOCK, dtype=jnp.int32)[:, None]
    j = jnp.arange(2 * Q_BLOCK, dtype=jnp.int32)[None, :]
    for window, dil in DILATED_BRANCHES:
        n = window // dil
        nb = -(-(seq // dil) // Q_BLOCK)
        off = i + n - j
        blk = jnp.arange(nb, dtype=jnp.int32)[:, None, None]
        valid = (off >= 0) & (off <= n) & (blk * Q_BLOCK - n + j >= 0)
        bias = jnp.moveaxis(jnp.take(bias_a, t5_bucket(off * dil), axis=0), -1, 0)
        out.append(jnp.where(valid[None], bias[:, None], NEG_INF))
    return out


def dilated_attention(q, k, v, bias_masks):
    b, h, s, dh = q.shape
    outs, lses = [], []
    for (window, dil), bm in zip(DILATED_BRANCHES, bias_masks):
        n = window // dil
        length = s // dil
        nb = bm.shape[1]
        padl = nb * Q_BLOCK

        def sub(t):
            return t.reshape(b, h, length, dil, dh).transpose(0, 1, 3, 2, 4)

        def band(t):
            tp = jnp.pad(sub(t), ((0, 0), (0, 0), (0, 0), (n, padl - length), (0, 0)))
            tp = tp.reshape(b, h, dil, nb + 1, Q_BLOCK, dh)
            return jnp.concatenate([tp[:, :, :, :-1], tp[:, :, :, 1:]], axis=4)

        qs = jnp.pad(sub(q), ((0, 0), (0, 0), (0, 0), (0, padl - length), (0, 0)))
        qs = qs.reshape(b, h, dil, nb, Q_BLOCK, dh)
        logits = jnp.einsum('bhrnqd,bhrnkd->bhrnqk', qs, band(k)) + bm[None, :, None]
        lse = jax.nn.logsumexp(logits, axis=-1)
        o = jnp.einsum('bhrnqk,bhrnkd->bhrnqd', jnp.exp(logits - lse[..., None]), band(v))
        o = o.reshape(b, h, dil, padl, dh)[:, :, :, :length].transpose(0, 1, 3, 2, 4)
        lse = lse.reshape(b, h, dil, padl)[..., :length].transpose(0, 1, 3, 2)
        outs.append(o.reshape(b, h, s, dh))
        lses.append(lse.reshape(b, h, s))
    wts = jax.nn.softmax(jnp.stack(lses), axis=0)
    return jnp.sum(wts[..., None] * jnp.stack(outs), axis=0)


def diff_attention(q, k, v, lam, dist_bias):
    s = q.shape[3]
    outs = []
    for i in range(s // Q_BLOCK):
        t0, kl = i * Q_BLOCK, (i + 1) * Q_BLOCK
        tq = t0 + jnp.arange(Q_BLOCK, dtype=jnp.int32)
        dist = tq[:, None] - jnp.arange(kl, dtype=jnp.int32)[None, :]
        bias = jnp.where(dist >= 0, jnp.take(dist_bias, jnp.maximum(dist, 0), axis=1), NEG_INF)
        logits = jnp.einsum('bhmqd,bhmkd->bhmqk', q[:, :, :, t0:kl], k[:, :, :, :kl]) + bias[:, None]
        e = jnp.exp(logits - jnp.max(logits, axis=-1, keepdims=True))
        pv = jnp.einsum('bhmqk,bhkd->bhmqd', e, v[:, :, :kl]) / jnp.sum(e, axis=-1)[..., None]
        outs.append(pv[:, :, 0] - lam * pv[:, :, 1])
    return jnp.concatenate(outs, axis=2)


def stick_breaking_attention(q, k, v):
    b, h, s, dh = q.shape
    ar = jnp.arange(Q_BLOCK, dtype=jnp.int32)
    incl = (ar[:, None] >= ar[None, :]).astype(q.dtype)
    outs = []
    for i in range(s // Q_BLOCK):
        t0, nk = i * Q_BLOCK, i + 1
        kl = nk * Q_BLOCK
        tq = t0 + ar
        strict = jnp.arange(kl, dtype=jnp.int32)[None, :] < tq[:, None]
        z = jnp.where(strict, jnp.einsum('bhqd,bhkd->bhqk', q[:, :, t0:kl], k[:, :, :kl]), -SB_MASK)
        lnot = (-jax.nn.softplus(z)).reshape(b, h, Q_BLOCK, nk, Q_BLOCK)
        within = jnp.einsum('bhqnj,js->bhqns', lnot, incl)
        later = (jnp.arange(nk)[:, None] > jnp.arange(nk)[None, :]).astype(q.dtype)
        carry = jnp.einsum('bhqn,nm->bhqm', jnp.sum(lnot, axis=-1), later)
        log_a = z + (within + carry[..., None]).reshape(b, h, Q_BLOCK, kl)
        outs.append(jnp.einsum('bhqk,bhkd->bhqd', jnp.exp(log_a), v[:, :, :kl]))
    return jnp.concatenate(outs, axis=2)


def token_mix(h, w_in, w_out, q_norm_a, k_norm_a, q_norm_b, k_norm_b,
              lambda_q1, lambda_k1, lambda_q2, lambda_k2, diff_subln, a_masks, b_dist_bias, layer):
    b, s, _ = h.shape
    proj = (h @ w_in).astype(jnp.float32)
    qa, ka, va, qb, kb, vb, qc, kc, vc = jnp.split(proj, SPLITS, axis=-1)

    qa = rms_norm(split_heads(qa, N_HEADS_A, HEAD_DIM), q_norm_a) * (HEAD_DIM ** -0.5)
    ka = rms_norm(split_heads(ka, N_HEADS_A, HEAD_DIM), k_norm_a)
    out_a = dilated_attention(qa, ka, split_heads(va, N_HEADS_A, HEAD_DIM), a_masks)

    def two_maps(t, g):
        t = split_heads(t, N_HEADS_B, HEAD_DIM).reshape(b, N_HEADS_B, s, 2, DIFF_HALF)
        return rms_norm(t, g).transpose(0, 1, 3, 2, 4)
    lam_init = 0.8 - 0.6 * math.exp(-0.3 * layer)
    lam = (jnp.exp(jnp.sum(lambda_q1.astype(jnp.float32) * lambda_k1.astype(jnp.float32)))
           - jnp.exp(jnp.sum(lambda_q2.astype(jnp.float32) * lambda_k2.astype(jnp.float32)))
           + lam_init)
    out_b = diff_attention(two_maps(qb, q_norm_b) * (DIFF_HALF ** -0.5), two_maps(kb, k_norm_b),
                           split_heads(vb, N_HEADS_B, HEAD_DIM), lam, b_dist_bias)
    out_b = rms_norm(out_b, diff_subln) * (1.0 - lam_init)

    out_c = stick_breaking_attention(split_heads(qc, N_HEADS_C, HEAD_DIM) * (HEAD_DIM ** -0.5),
                                     split_heads(kc, N_HEADS_C, HEAD_DIM),
                                     split_heads(vc, N_HEADS_C, HEAD_DIM))

    mixed = jnp.concatenate([merge_heads(out_a), merge_heads(out_b), merge_heads(out_c)], axis=-1)
    return mixed.astype(h.dtype) @ w_out


def setup_inputs(seed: int = 0) -> dict:
    key = jax.random.key(seed)
    ks = jax.random.split(key, 22)
    f32 = jnp.float32

    def nrm(k, shape, scale):
        return scale * jax.random.normal(k, shape, f32)

    def gain(k, shape):
        return 1.0 + 0.01 * jax.random.normal(k, shape, f32)

    return {
        'x': nrm(ks[0], (BATCH, SEQ, D_MODEL), 1.0),
        'rel_bias': nrm(ks[1], (N_BUCKETS, N_BIAS_HEADS), 0.5),
        'ffn1_norm': gain(ks[2], (DEPTH, D_MODEL)),
        'ffn1_w_gate': nrm(ks[3], (DEPTH, D_MODEL, D_FF), D_MODEL ** -0.5),
        'ffn1_w_up': nrm(ks[4], (DEPTH, D_MODEL, D_FF), D_MODEL ** -0.5),
        'ffn1_w_down': nrm(ks[5], (DEPTH, D_FF, D_MODEL), D_FF ** -0.5),
        'mix_norm': gain(ks[6], (DEPTH, D_MODEL)),
        'w_in': nrm(ks[7], (DEPTH, D_MODEL, 3 * MIX_WIDTH), D_MODEL ** -0.5),
        'q_norm_a': gain(ks[8], (DEPTH, HEAD_DIM)),
        'k_norm_a': gain(ks[9], (DEPTH, HEAD_DIM)),
        'q_norm_b': gain(ks[10], (DEPTH, DIFF_HALF)),
        'k_norm_b': gain(ks[11], (DEPTH, DIFF_HALF)),
        'lambda_q1': nrm(ks[12], (DEPTH, DIFF_HALF), 0.1),
        'lambda_k1': nrm(ks[13], (DEPTH, DIFF_HALF), 0.1),
        'lambda_q2': nrm(ks[14], (DEPTH, DIFF_HALF), 0.1),
        'lambda_k2': nrm(ks[15], (DEPTH, DIFF_HALF), 0.1),
        'diff_subln': gain(ks[16], (DEPTH, HEAD_DIM)),
        'w_out': nrm(ks[17], (DEPTH, MIX_WIDTH, D_MODEL), MIX_WIDTH ** -0.5),
        'ffn2_norm': gain(ks[18], (DEPTH, D_MODEL)),
        'ffn2_w_gate': nrm(ks[19], (DEPTH, D_MODEL, D_FF), D_MODEL ** -0.5),
        'ffn2_w_up': nrm(ks[20], (DEPTH, D_MODEL, D_FF), D_MODEL ** -0.5),
        'ffn2_w_down': nrm(ks[21], (DEPTH, D_FF, D_MODEL), D_FF ** -0.5),
    }


def reference(x, rel_bias, ffn1_norm, ffn1_w_gate, ffn1_w_up, ffn1_w_down, mix_norm, w_in,
              q_norm_a, k_norm_a, q_norm_b, k_norm_b, lambda_q1, lambda_k1, lambda_q2, lambda_k2,
              diff_subln, w_out, ffn2_norm, ffn2_w_gate, ffn2_w_up, ffn2_w_down):
    seq = x.shape[1]
    rb = rel_bias.astype(jnp.float32)
    a_masks = dilated_bias_masks(rb[:, :N_HEADS_A], seq)
    b_dist_bias = jnp.take(rb[:, N_HEADS_A:], t5_bucket(jnp.arange(seq, dtype=jnp.int32)), axis=0).T
    for layer in range(DEPTH):
        h = rms_norm(x, ffn1_norm[layer])
        x = x + 0.5 * swiglu(h, ffn1_w_gate[layer], ffn1_w_up[layer], ffn1_w_down[layer])
        h = rms_norm(x, mix_norm[layer])
        x = x + token_mix(h, w_in[layer], w_out[layer], q_norm_a[layer], k_norm_a[layer],
                          q_norm_b[layer], k_norm_b[layer], lambda_q1[layer], lambda_k1[layer],
                          lambda_q2[layer], lambda_k2[layer], diff_subln[layer],
                          a_masks, b_dist_bias, layer)
        h = rms_norm(x, ffn2_norm[layer])
        x = x + 0.5 * swiglu(h, ffn2_w_gate[layer], ffn2_w_up[layer], ffn2_w_down[layer])
    return x
```

```python
import functools
import math

import jax
import jax.numpy as jnp
import numpy as np
from jax import lax
from jax.experimental import pallas as pl
from jax.experimental.pallas import tpu as pltpu

F32 = jnp.float32
BF16 = jnp.bfloat16

HEAD_DIM = 64
DIFF_HALF = HEAD_DIM // 2
N_BUCKETS = 32
MAX_DISTANCE = 2048
DILATED_BRANCHES = ((128, 1), (512, 4), (2048, 16))
Q_BLOCK = 128
EPS = 1e-6
NEG_INF = -1e30
SB_MASK = 1e4

LANES = 128
A_CHUNK = 2048
ATT_TILE = 256
ROW_TILE = 512
VMEM_LIMIT = 56 * 1024 * 1024


def _cparams(*sem):
    return pltpu.CompilerParams(dimension_semantics=sem, vmem_limit_bytes=VMEM_LIMIT)


def _rms(x, gain_row):
    ms = jnp.mean(x * x, axis=-1, keepdims=True)
    return x * lax.rsqrt(ms + EPS) * gain_row


def _dot(a, b):
    return jnp.dot(a, b, preferred_element_type=F32)


def _dot_nt(a, b):
    return lax.dot_general(a, b, (((1,), (1,)), ((), ())), preferred_element_type=F32)


def _ffn_kernel(x_ref, g_ref, wg_ref, wu_ref, wd_ref, o_ref):
    x = x_ref[...]
    h = _rms(x, g_ref[...]).astype(BF16)
    gate = _dot(h, wg_ref[...])
    up = _dot(h, wu_ref[...])
    act = (gate * jax.nn.sigmoid(gate) * up).astype(BF16)
    o_ref[...] = x + 0.5 * _dot(act, wd_ref[...])


def _ffn(x2, gain, wg, wu, wd):
    t, d = x2.shape
    dff = wg.shape[1]
    row = pl.BlockSpec((ROW_TILE, d), lambda i: (i, 0))
    full = lambda shape: pl.BlockSpec(shape, lambda i: (0, 0))
    return pl.pallas_call(
        _ffn_kernel,
        out_shape=jax.ShapeDtypeStruct((t, d), F32),
        grid=(t // ROW_TILE,),
        in_specs=[row, full((1, d)), full((d, dff)), full((d, dff)), full((dff, d))],
        out_specs=row,
        compiler_params=_cparams("parallel"),
        name="ffn_half_step",
    )(x2, gain.reshape(1, d), wg, wu, wd)


def _proj_kernel(widths, x_ref, g_ref, w_ref, gqa_ref, gka_ref, gqb_ref, gkb_ref,
                 g64_ref, g32_ref, qa_ref, ka_ref, va_ref, qb_ref, kb_ref, vb_ref,
                 qc_ref, kc_ref, vc_ref):
    wa, wb, wc = widths
    h = _rms(x_ref[...], g_ref[...]).astype(BF16)

    def group_norm(t, ones_ref, gain_ref, group, scale):
        ss = _dot((t * t).astype(BF16), ones_ref[...])
        return t * lax.rsqrt(ss * (1.0 / group) + EPS) * (gain_ref[...] * scale)

    col = 0

    def section(width):
        nonlocal col
        out = [_dot(h, w_ref[:, col + c:col + c + 256]) for c in range(0, width, 256)]
        col += width
        return out

    for c, t in enumerate(section(wa)):
        qa_ref[:, c * 256:(c + 1) * 256] = group_norm(t, g64_ref, gqa_ref, HEAD_DIM, HEAD_DIM ** -0.5)
    for c, t in enumerate(section(wa)):
        ka_ref[:, c * 256:(c + 1) * 256] = group_norm(t, g64_ref, gka_ref, HEAD_DIM, 1.0)
    for c, t in enumerate(section(wa)):
        va_ref[:, c * 256:(c + 1) * 256] = t
    for c, t in enumerate(section(wb)):
        qb_ref[:, c * 256:(c + 1) * 256] = group_norm(
            t, g32_ref, gqb_ref, DIFF_HALF, DIFF_HALF ** -0.5).astype(BF16)
    for c, t in enumerate(section(wb)):
        kb_ref[:, c * 256:(c + 1) * 256] = group_norm(t, g32_ref, gkb_ref, DIFF_HALF, 1.0).astype(BF16)
    for c, t in enumerate(section(wb)):
        vb_ref[:, c * 256:(c + 1) * 256] = t.astype(BF16)
    for c, t in enumerate(section(wc)):
        qc_ref[:, c * 256:(c + 1) * 256] = (t * HEAD_DIM ** -0.5).astype(BF16)
    for c, t in enumerate(section(wc)):
        kc_ref[:, c * 256:(c + 1) * 256] = t.astype(BF16)
    for c, t in enumerate(section(wc)):
        vc_ref[:, c * 256:(c + 1) * 256] = t.astype(BF16)


def _block_diag_ones(group):
    idx = np.arange(256) // group
    return jnp.asarray(idx[:, None] == idx[None, :], dtype=BF16)


def _proj(x2, gain, w_in, gqa, gka, gqb, gkb, widths):
    t, d = x2.shape
    wa, wb, wc = widths
    row = lambda w: pl.BlockSpec((ROW_TILE, w), lambda i: (i, 0))
    full = lambda shape: pl.BlockSpec(shape, lambda i: (0, 0))
    tile256 = lambda g: jnp.tile(g, 256 // g.shape[0]).reshape(1, 256)
    out_shape = ([jax.ShapeDtypeStruct((t, wa), F32)] * 3
                 + [jax.ShapeDtypeStruct((t, wb), BF16)] * 3
                 + [jax.ShapeDtypeStruct((t, wc), BF16)] * 3)
    return pl.pallas_call(
        functools.partial(_proj_kernel, widths),
        out_shape=out_shape,
        grid=(t // ROW_TILE,),
        in_specs=[row(d), full((1, d)), full(w_in.shape)] + [full((1, 256))] * 4
                 + [full((256, 256))] * 2,
        out_specs=[row(wa)] * 3 + [row(wb)] * 3 + [row(wc)] * 3,
        compiler_params=_cparams("parallel"),
        name="norm_in_proj",
    )(x2, gain.reshape(1, d), w_in, tile256(gqa), tile256(gka), tile256(gqb), tile256(gkb),
      _block_diag_ones(HEAD_DIM), _block_diag_ones(DIFF_HALF))


def _dilated_kernel(q_ref, kp_ref, kc_ref, vp_ref, vc_ref, bias_ref, o_ref,
                    kk_ref, vv_ref, m_ref, l_ref, acc_ref):
    chunk = pl.program_id(1)
    kk_ref[0:A_CHUNK] = kp_ref[0]
    kk_ref[A_CHUNK:] = kc_ref[0]
    vv_ref[0:A_CHUNK] = vp_ref[0]
    vv_ref[A_CHUNK:] = vc_ref[0]
    m_ref[...] = jnp.full(m_ref.shape, NEG_INF, F32)
    l_ref[...] = jnp.zeros(l_ref.shape, F32)
    acc_ref[...] = jnp.zeros(acc_ref.shape, F32)

    lane = lax.broadcasted_iota(jnp.int32, (Q_BLOCK, LANES), 1)
    head0 = lane < HEAD_DIM

    for bi, (window, dil) in enumerate(DILATED_BRANCHES):
        nblk = A_CHUNK // (Q_BLOCK * dil)
        shift = int(math.log2(nblk))

        def block(idx, carry, bi=bi, dil=dil, nblk=nblk, shift=shift):
            rho = idx >> shift
            t = idx & (nblk - 1)
            q_start = rho + t * (Q_BLOCK * dil)
            k_start = A_CHUNK + q_start - Q_BLOCK * dil
            rows = pl.ds(q_start, Q_BLOCK, stride=dil) if dil > 1 else pl.ds(q_start, Q_BLOCK)
            krows = (pl.ds(k_start, 2 * Q_BLOCK, stride=dil) if dil > 1
                     else pl.ds(k_start, 2 * Q_BLOCK))
            q = q_ref[0, rows, :]
            k = kk_ref[krows, :].astype(BF16)
            v = vv_ref[krows, :].astype(BF16)
            q2 = jnp.concatenate([jnp.where(head0, q, 0.0), jnp.where(head0, 0.0, q)],
                                 axis=0).astype(BF16)
            first = jnp.logical_and(chunk == 0, t == 0).astype(jnp.int32)
            s = _dot_nt(q2, k) + bias_ref[bi, first, 0]
            m_old = jnp.concatenate([m_ref[0, rows, :], m_ref[1, rows, :]], axis=0)
            l_old = jnp.concatenate([l_ref[0, rows, :], l_ref[1, rows, :]], axis=0)
            m_new = jnp.maximum(m_old, jnp.max(s, axis=-1, keepdims=True))
            alpha = jnp.exp(m_old - m_new)
            p = jnp.exp(s - jnp.concatenate([m_new, m_new], axis=1))
            l_new = alpha * l_old + jnp.sum(p, axis=-1, keepdims=True)
            pv = _dot(p.astype(BF16), v)
            acc_old = acc_ref[rows, :]
            acc_new = jnp.where(head0, alpha[:Q_BLOCK] * acc_old + pv[:Q_BLOCK],
                                alpha[Q_BLOCK:] * acc_old + pv[Q_BLOCK:])
            acc_ref[rows, :] = acc_new
            m_ref[0, rows, :] = m_new[:Q_BLOCK]
            m_ref[1, rows, :] = m_new[Q_BLOCK:]
            l_ref[0, rows, :] = l_new[:Q_BLOCK]
            l_ref[1, rows, :] = l_new[Q_BLOCK:]
            return carry

        lax.fori_loop(0, dil * nblk, block, 0)

    lane_c = lax.broadcasted_iota(jnp.int32, (A_CHUNK, LANES), 1)
    denom = jnp.where(lane_c < HEAD_DIM, l_ref[0], l_ref[1])
    o_ref[0] = (acc_ref[...] / denom).astype(BF16)


def _dilated_attention(qa, ka, va, bias_a):
    b, s, wa = qa.shape
    n_pairs = wa // LANES
    cur = pl.BlockSpec((1, A_CHUNK, LANES), lambda i, c, p: (i, c, p))
    prev = pl.BlockSpec((1, A_CHUNK, LANES), lambda i, c, p: (i, jnp.maximum(c - 1, 0), p))
    nb = len(DILATED_BRANCHES)
    bias = pl.BlockSpec((nb, 2, 1, 2 * Q_BLOCK, 2 * Q_BLOCK), lambda i, c, p: (0, 0, p, 0, 0))
    return pl.pallas_call(
        _dilated_kernel,
        out_shape=jax.ShapeDtypeStruct((b, s, wa), BF16),
        grid=(b, s // A_CHUNK, n_pairs),
        in_specs=[cur, prev, cur, prev, cur, bias],
        out_specs=cur,
        scratch_shapes=[pltpu.VMEM((2 * A_CHUNK, LANES), F32),
                        pltpu.VMEM((2 * A_CHUNK, LANES), F32),
                        pltpu.VMEM((2, A_CHUNK, LANES), F32),
                        pltpu.VMEM((2, A_CHUNK, LANES), F32),
                        pltpu.VMEM((A_CHUNK, LANES), F32)],
        compiler_params=_cparams("parallel", "parallel", "parallel"),
        name="dilated_attention",
    )(qa, ka, ka, va, va, bias_a)


def _diff_kernel(n_bias, q_ref, k_ref, v_ref, bias_ref, lam_ref, gain_ref, o_ref,
                 q4_ref, m_ref, l_ref, acc_ref):
    qi = pl.program_id(2)
    tq = ATT_TILE
    lane = lax.broadcasted_iota(jnp.int32, (tq, LANES), 1)
    q = q_ref[0]
    zero = jnp.zeros_like(q)
    for g in range(4):
        sel = jnp.logical_and(lane >= g * DIFF_HALF, lane < (g + 1) * DIFF_HALF)
        q4_ref[g * tq:(g + 1) * tq] = jnp.where(sel, q, zero)
    m_ref[...] = jnp.full(m_ref.shape, NEG_INF, F32)
    l_ref[...] = jnp.zeros(l_ref.shape, F32)
    acc_ref[...] = jnp.zeros(acc_ref.shape, F32)

    def kv_step(kj, carry):
        start = pl.multiple_of(kj * tq, tq)
        k = k_ref[0, pl.ds(start, tq), :]
        v = v_ref[0, pl.ds(start, tq), :]
        d = jnp.minimum(qi - kj, n_bias - 1)
        b0 = bias_ref[d, 0]
        b1 = bias_ref[d, 1]
        s = _dot_nt(q4_ref[...], k) + jnp.concatenate([b0, b0, b1, b1], axis=0)
        m_old = m_ref[...]
        m_new = jnp.maximum(m_old, jnp.max(s, axis=-1, keepdims=True))
        alpha = jnp.exp(m_old - m_new)
        p = jnp.exp(s - jnp.concatenate([m_new, m_new], axis=1))
        l_ref[...] = alpha * l_ref[...] + jnp.sum(p, axis=-1, keepdims=True)
        acc_ref[...] = alpha * acc_ref[...] + _dot(p.astype(BF16), v)
        m_ref[...] = m_new
        return carry

    lax.fori_loop(0, qi + 1, kv_step, 0)

    pv = acc_ref[...] / l_ref[...]
    lam = lam_ref[...]
    gain = gain_ref[...]
    head0 = lane < HEAD_DIM
    outs = []
    for h in range(2):
        diff = pv[(2 * h) * tq:(2 * h + 1) * tq] - lam * pv[(2 * h + 1) * tq:(2 * h + 2) * tq]
        mine = head0 if h == 0 else jnp.logical_not(head0)
        ms = jnp.sum(jnp.where(mine, diff * diff, 0.0), axis=-1, keepdims=True) * (1.0 / HEAD_DIM)
        outs.append(diff * lax.rsqrt(ms + EPS) * gain)
    o_ref[0] = jnp.where(head0, outs[0], outs[1]).astype(BF16)


def _diff_attention(qb, kb, vb, bias_b, lam, post_gain):
    b, s, wb = qb.shape
    n_pairs = wb // LANES
    n_bias = bias_b.shape[0]
    tq = ATT_TILE
    qspec = pl.BlockSpec((1, tq, LANES), lambda i, p, j: (i, j, p))
    kvspec = pl.BlockSpec((1, s, LANES), lambda i, p, j: (i, 0, p))
    bspec = pl.BlockSpec((n_bias, 2, tq, tq), lambda i, p, j: (0, p, 0, 0))
    vec = pl.BlockSpec((1, LANES), lambda i, p, j: (0, 0))
    return pl.pallas_call(
        functools.partial(_diff_kernel, n_bias),
        out_shape=jax.ShapeDtypeStruct((b, s, wb), BF16),
        grid=(b, n_pairs, s // tq),
        in_specs=[qspec, kvspec, kvspec, bspec, vec, vec],
        out_specs=qspec,
        scratch_shapes=[pltpu.VMEM((4 * tq, LANES), BF16),
                        pltpu.VMEM((4 * tq, LANES), F32),
                        pltpu.VMEM((4 * tq, LANES), F32),
                        pltpu.VMEM((4 * tq, LANES), F32)],
        compiler_params=_cparams("parallel", "parallel", "parallel"),
        name="diff_attention",
    )(qb, kb, vb, bias_b, lam, post_gain)


def _stick_kernel(q_ref, k_ref, v_ref, tri_ref, o_ref, q2_ref, carry_ref, acc_ref):
    qi = pl.program_id(2)
    tq = ATT_TILE
    lane = lax.broadcasted_iota(jnp.int32, (tq, LANES), 1)
    head0 = lane < HEAD_DIM
    q = q_ref[0]
    zero = jnp.zeros_like(q)
    q2_ref[0:tq] = jnp.where(head0, q, zero)
    q2_ref[tq:] = jnp.where(head0, zero, q)
    carry_ref[...] = jnp.zeros(carry_ref.shape, F32)
    acc_ref[...] = jnp.zeros(acc_ref.shape, F32)

    def kv_step(kj, diagonal):
        start = pl.multiple_of(kj * tq, tq)
        k = k_ref[0, pl.ds(start, tq), :]
        v = v_ref[0, pl.ds(start, tq), :]
        z = _dot_nt(q2_ref[...], k)
        if diagonal:
            r = lax.broadcasted_iota(jnp.int32, (tq, tq), 0)
            c = lax.broadcasted_iota(jnp.int32, (tq, tq), 1)
            strict = jnp.concatenate([c < r, c < r], axis=0)
            z = jnp.where(strict, z, -SB_MASK)
        lnot = -(jnp.maximum(z, 0.0) + jnp.log1p(jnp.exp(-jnp.abs(z))))
        within = _dot(lnot.astype(BF16), tri_ref[...])
        carry = carry_ref[...]
        log_a = z + within + jnp.concatenate([carry, carry], axis=1)
        acc_ref[...] += _dot(jnp.exp(log_a).astype(BF16), v)
        carry_ref[...] = carry + jnp.sum(lnot, axis=-1, keepdims=True)

    kv_step(qi, True)

    def body(i, c):
        kv_step(qi - 1 - i, False)
        return c

    lax.fori_loop(0, qi, body, 0)
    o_ref[0] = jnp.where(head0, acc_ref[0:tq], acc_ref[tq:]).astype(BF16)


def _stick_attention(qc, kc, vc):
    b, s, wc = qc.shape
    n_pairs = wc // LANES
    tq = ATT_TILE
    idx = np.arange(tq)
    tri = jnp.asarray(idx[:, None] >= idx[None, :], dtype=BF16)
    qspec = pl.BlockSpec((1, tq, LANES), lambda i, p, j: (i, j, p))
    kvspec = pl.BlockSpec((1, s, LANES), lambda i, p, j: (i, 0, p))
    return pl.pallas_call(
        _stick_kernel,
        out_shape=jax.ShapeDtypeStruct((b, s, wc), BF16),
        grid=(b, n_pairs, s // tq),
        in_specs=[qspec, kvspec, kvspec, pl.BlockSpec((tq, tq), lambda i, p, j: (0, 0))],
        out_specs=qspec,
        scratch_shapes=[pltpu.VMEM((2 * tq, LANES), BF16),
                        pltpu.VMEM((2 * tq, LANES), F32),
                        pltpu.VMEM((2 * tq, LANES), F32)],
        compiler_params=_cparams("parallel", "parallel", "parallel"),
        name="stick_breaking_attention",
    )(qc, kc, vc, tri)


def _out_proj_kernel(widths, x_ref, a_ref, b_ref, c_ref, w_ref, o_ref):
    wa, wb, wc = widths
    y = _dot(a_ref[...], w_ref[0:wa])
    y += _dot(b_ref[...], w_ref[wa:wa + wb])
    y += _dot(c_ref[...], w_ref[wa + wb:wa + wb + wc])
    o_ref[...] = x_ref[...] + y


def _out_proj(x2, oa, ob, oc, w_out, widths):
    t, d = x2.shape
    row = lambda w: pl.BlockSpec((ROW_TILE, w), lambda i: (i, 0))
    return pl.pallas_call(
        functools.partial(_out_proj_kernel, widths),
        out_shape=jax.ShapeDtypeStruct((t, d), F32),
        grid=(t // ROW_TILE,),
        in_specs=[row(d), row(widths[0]), row(widths[1]), row(widths[2]),
                  pl.BlockSpec(w_out.shape, lambda i: (0, 0))],
        out_specs=row(d),
        compiler_params=_cparams("parallel"),
        name="out_proj_residual",
    )(x2, oa, ob, oc, w_out)


def _t5_bucket(dist):
    dist = jnp.maximum(dist, 0)
    max_exact = N_BUCKETS // 2
    d_f = jnp.maximum(dist, 1).astype(F32)
    large = max_exact + (jnp.log(d_f / max_exact) / math.log(MAX_DISTANCE / max_exact)
                         * (N_BUCKETS - max_exact)).astype(jnp.int32)
    large = jnp.minimum(large, N_BUCKETS - 1)
    return jnp.where(dist < max_exact, dist, large)


def _dilated_bias_table(bias_a):
    n_heads = bias_a.shape[1]
    i = jnp.arange(Q_BLOCK, dtype=jnp.int32)[:, None]
    j = jnp.arange(2 * Q_BLOCK, dtype=jnp.int32)[None, :]
    tables = []
    for window, dil in DILATED_BRANCHES:
        n = window // dil
        off = i + n - j
        band = (off >= 0) & (off <= n)
        bias = jnp.moveaxis(jnp.take(bias_a, _t5_bucket(off * dil), axis=0), -1, 0)
        variants = [jnp.where(valid[None], bias, NEG_INF) for valid in (band, band & (j >= n))]
        tables.append(jnp.stack(variants))
    table = jnp.stack(tables)
    return table.reshape(len(DILATED_BRANCHES), 2, n_heads // 2, 2 * Q_BLOCK, 2 * Q_BLOCK)


def _diff_bias_table(bias_b, seq):
    t = ATT_TILE
    n_bias = min(seq // t, MAX_DISTANCE // t + 2)
    by_dist = jnp.take(bias_b, _t5_bucket(jnp.arange(seq, dtype=jnp.int32)), axis=0).T
    r = jnp.arange(t, dtype=jnp.int32)[:, None]
    c = jnp.arange(t, dtype=jnp.int32)[None, :]
    tiles = []
    for d in range(n_bias):
        dist = d * t + r - c
        tile = jnp.take(by_dist, jnp.clip(dist, 0, seq - 1), axis=1)
        tiles.append(jnp.where(dist >= 0, tile, NEG_INF))
    return jnp.stack(tiles)


def kernel(x, rel_bias, ffn1_norm, ffn1_w_gate, ffn1_w_up, ffn1_w_down, mix_norm, w_in,
           q_norm_a, k_norm_a, q_norm_b, k_norm_b, lambda_q1, lambda_k1, lambda_q2, lambda_k2,
           diff_subln, w_out, ffn2_norm, ffn2_w_gate, ffn2_w_up, ffn2_w_down):
    b, s, d = x.shape
    depth = w_in.shape[0]
    n_heads = d // HEAD_DIM
    wa = (n_heads // 2) * HEAD_DIM
    wb = (n_heads // 4) * HEAD_DIM
    wc = d - wa - wb
    widths = (wa, wb, wc)
    assert w_in.shape[2] == 3 * d and s % A_CHUNK == 0 and (b * s) % ROW_TILE == 0

    rb = rel_bias.astype(F32)
    bias_a = _dilated_bias_table(rb[:, :wa // HEAD_DIM])
    bias_b = _diff_bias_table(rb[:, wa // HEAD_DIM:], s)

    x2 = x.reshape(b * s, d)
    for layer in range(depth):
        x2 = _ffn(x2, ffn1_norm[layer], ffn1_w_gate[layer].astype(BF16),
                  ffn1_w_up[layer].astype(BF16), ffn1_w_down[layer].astype(BF16))

        qa, ka, va, qb, kb, vb, qc, kc, vc = _proj(
            x2, mix_norm[layer], w_in[layer].astype(BF16), q_norm_a[layer], k_norm_a[layer],
            q_norm_b[layer], k_norm_b[layer], widths)
        seq3 = lambda t: t.reshape(b, s, t.shape[-1])

        out_a = _dilated_attention(seq3(qa), seq3(ka), seq3(va), bias_a)

        lam_init = 0.8 - 0.6 * math.exp(-0.3 * layer)
        lam = (jnp.exp(jnp.sum(lambda_q1[layer].astype(F32) * lambda_k1[layer].astype(F32)))
               - jnp.exp(jnp.sum(lambda_q2[layer].astype(F32) * lambda_k2[layer].astype(F32)))
               + lam_init)
        lam_row = jnp.full((1, LANES), lam, F32)
        post_gain = (jnp.tile(diff_subln[layer].astype(F32), LANES // HEAD_DIM)
                     * (1.0 - lam_init)).reshape(1, LANES)
        out_b = _diff_attention(seq3(qb), seq3(kb), seq3(vb), bias_b, lam_row, post_gain)

        out_c = _stick_attention(seq3(qc), seq3(kc), seq3(vc))

        x2 = _out_proj(x2, out_a.reshape(b * s, wa), out_b.reshape(b * s, wb),
                       out_c.reshape(b * s, wc), w_out[layer].astype(BF16), widths)

        x2 = _ffn(x2, ffn2_norm[layer], ffn2_w_gate[layer].astype(BF16),
                  ffn2_w_up[layer].astype(BF16), ffn2_w_down[layer].astype(BF16))
    return x2.reshape(b, s, d)
```

```python
import functools
import math

import jax
import jax.numpy as jnp
import numpy as np
from jax import lax
from jax.experimental import pallas as pl
from jax.experimental.pallas import tpu as pltpu

F32 = jnp.float32
BF16 = jnp.bfloat16

HEAD_DIM = 64
DIFF_HALF = HEAD_DIM // 2
N_BUCKETS = 32
MAX_DISTANCE = 2048
DILATED_BRANCHES = ((128, 1), (512, 4), (2048, 16))
Q_BLOCK = 128
EPS = 1e-6
NEG_INF = -1e30
SB_MASK = 1e4
LOG2E = math.log2(math.e)

LANES = 128
A_CHUNK = 2048
ATT_TILE = 256
ROW_TILE = 512
ONES_ROWS = 16
VMEM_LIMIT = 56 * 1024 * 1024


def _cparams(*sem):
    return pltpu.CompilerParams(dimension_semantics=sem, vmem_limit_bytes=VMEM_LIMIT)


def _rms(x, gain_row):
    ms = jnp.mean(x * x, axis=-1, keepdims=True)
    return x * lax.rsqrt(ms + EPS) * gain_row


def _dot(a, b):
    return jnp.dot(a, b, preferred_element_type=F32)


def _dot_nt(a, b):
    return lax.dot_general(a, b, (((1,), (1,)), ((), ())), preferred_element_type=F32)


def _ffn_kernel(x_ref, g_ref, wg_ref, wu_ref, wd_ref, o_ref):
    x = x_ref[...]
    h = _rms(x, g_ref[...]).astype(BF16)
    gate = _dot(h, wg_ref[...])
    up = _dot(h, wu_ref[...])
    act = (gate * jax.nn.sigmoid(gate) * up).astype(BF16)
    o_ref[...] = x + 0.5 * _dot(act, wd_ref[...])


def _ffn(x2, gain, wg, wu, wd):
    t, d = x2.shape
    dff = wg.shape[1]
    row = pl.BlockSpec((ROW_TILE, d), lambda i: (i, 0))
    full = lambda shape: pl.BlockSpec(shape, lambda i: (0, 0))
    return pl.pallas_call(
        _ffn_kernel,
        out_shape=jax.ShapeDtypeStruct((t, d), F32),
        grid=(t // ROW_TILE,),
        in_specs=[row, full((1, d)), full((d, dff)), full((d, dff)), full((dff, d))],
        out_specs=row,
        compiler_params=_cparams("parallel"),
        name="ffn_half_step",
    )(x2, gain.reshape(1, d), wg, wu, wd)


def _proj_kernel(widths, x_ref, g_ref, wn_ref, wt_ref, gqa_ref, gka_ref, gkb_ref, gqb_ref,
                 g64_ref, g32_ref, qa_ref, ka_ref, va_ref, kb_ref, kc_ref,
                 qbt_ref, vbt_ref, qct_ref, vct_ref):
    wa, wb, wc = widths
    h = _rms(x_ref[...], g_ref[...]).astype(BF16)
    sub = 256

    def group_norm(t, ones_ref, gain_ref, group, scale):
        ss = _dot((t * t).astype(BF16), ones_ref[...])
        return t * lax.rsqrt(ss * (1.0 / group) + EPS) * (gain_ref[...] * scale)

    def group_norm_t(t, ones_ref, gain_ref, group, scale):
        ss = _dot(ones_ref[...], (t * t).astype(BF16))
        return t * lax.rsqrt(ss * (1.0 / group) + EPS) * (gain_ref[...] * scale)

    col = 0
    for ref, width, post in (
            (qa_ref, wa, lambda t: group_norm(t, g64_ref, gqa_ref, HEAD_DIM, HEAD_DIM ** -0.5)),
            (ka_ref, wa, lambda t: group_norm(t, g64_ref, gka_ref, HEAD_DIM, 1.0)),
            (va_ref, wa, lambda t: t),
            (kb_ref, wb, lambda t: group_norm(t, g32_ref, gkb_ref, DIFF_HALF, 1.0)),
            (kc_ref, wc, lambda t: t)):
        for c in range(0, width, sub):
            t = _dot(h, wn_ref[:, col + c:col + c + sub])
            ref[:, c:c + sub] = post(t).astype(ref.dtype)
        col += width

    row = 0
    for ref, width, post in (
            (qbt_ref, wb, lambda t: group_norm_t(t, g32_ref, gqb_ref, DIFF_HALF,
                                                 DIFF_HALF ** -0.5 * LOG2E)),
            (vbt_ref, wb, lambda t: t),
            (qct_ref, wc, lambda t: t * (HEAD_DIM ** -0.5 * LOG2E)),
            (vct_ref, wc, lambda t: t)):
        for r in range(0, width, sub):
            t = post(_dot_nt(wt_ref[row + r:row + r + sub, :], h)).astype(BF16)
            for j in range(ROW_TILE // ATT_TILE):
                ref[j, r:r + sub, :] = t[:, j * ATT_TILE:(j + 1) * ATT_TILE]
        row += width


def _block_diag_ones(group):
    idx = np.arange(256) // group
    return jnp.asarray(idx[:, None] == idx[None, :], dtype=BF16)


def _proj(x2, gain, w_in, gqa, gka, gqb, gkb, widths):
    t, d = x2.shape
    wa, wb, wc = widths
    assert wa % 256 == 0 and wb % 256 == 0 and wc % 256 == 0
    w = w_in.astype(BF16)
    o = np.cumsum([0, wa, wa, wa, wb, wb, wb, wc, wc, wc])
    sec = lambda i: w[:, o[i]:o[i + 1]]
    w_nat = jnp.concatenate([sec(0), sec(1), sec(2), sec(4), sec(7)], axis=1)
    w_tr = jnp.concatenate([sec(3), sec(5), sec(6), sec(8)], axis=1).T
    row = lambda wd: pl.BlockSpec((ROW_TILE, wd), lambda i: (i, 0))
    full = lambda shape: pl.BlockSpec(shape, lambda i: (0,) * len(shape))
    slab = lambda wd: pl.BlockSpec((ROW_TILE // ATT_TILE, wd, ATT_TILE), lambda i: (i, 0, 0))
    tile256 = lambda g: jnp.tile(g.astype(F32), 256 // g.shape[0])
    out_shape = ([jax.ShapeDtypeStruct((t, wa), F32)] * 3
                 + [jax.ShapeDtypeStruct((t, wb), BF16), jax.ShapeDtypeStruct((t, wc), BF16)]
                 + [jax.ShapeDtypeStruct((t // ATT_TILE, wd, ATT_TILE), BF16)
                    for wd in (wb, wb, wc, wc)])
    return pl.pallas_call(
        functools.partial(_proj_kernel, widths),
        out_shape=out_shape,
        grid=(t // ROW_TILE,),
        in_specs=[row(d), full((1, d)), full(w_nat.shape), full(w_tr.shape)]
                 + [full((1, 256))] * 3 + [full((256, 1))] + [full((256, 256))] * 2,
        out_specs=[row(wa)] * 3 + [row(wb), row(wc)] + [slab(wb), slab(wb), slab(wc), slab(wc)],
        compiler_params=_cparams("parallel"),
        name="norm_in_proj",
    )(x2, gain.reshape(1, d), w_nat, w_tr, tile256(gqa).reshape(1, 256), tile256(gka).reshape(1, 256),
      tile256(gkb).reshape(1, 256), tile256(gqb).reshape(256, 1),
      _block_diag_ones(HEAD_DIM), _block_diag_ones(DIFF_HALF))


def _dilated_kernel(q_ref, kp_ref, kc_ref, vp_ref, vc_ref, bias_ref, o_ref,
                    kk_ref, vv_ref, m_ref, l_ref, acc_ref):
    chunk = pl.program_id(1)
    kk_ref[0:A_CHUNK] = kp_ref[0]
    kk_ref[A_CHUNK:] = kc_ref[0]
    vv_ref[0:A_CHUNK] = vp_ref[0]
    vv_ref[A_CHUNK:] = vc_ref[0]
    m_ref[...] = jnp.full(m_ref.shape, NEG_INF, F32)
    l_ref[...] = jnp.zeros(l_ref.shape, F32)
    acc_ref[...] = jnp.zeros(acc_ref.shape, F32)

    lane = lax.broadcasted_iota(jnp.int32, (Q_BLOCK, LANES), 1)
    head0 = lane < HEAD_DIM

    for bi, (window, dil) in enumerate(DILATED_BRANCHES):
        nblk = A_CHUNK // (Q_BLOCK * dil)
        shift = int(math.log2(nblk))
        n_iter = dil * nblk
        ds = (lambda start, size, dil=dil:
              pl.ds(start, size, stride=dil) if dil > 1 else pl.ds(start, size))

        def load(idx, bi=bi, dil=dil, nblk=nblk, shift=shift, ds=ds):
            rho = idx >> shift
            t = idx & (nblk - 1)
            q_start = rho + t * (Q_BLOCK * dil)
            rows = ds(q_start, Q_BLOCK)
            krows = ds(A_CHUNK + q_start - Q_BLOCK * dil, 2 * Q_BLOCK)
            first = jnp.logical_and(chunk == 0, t == 0).astype(jnp.int32)
            return dict(
                rows=rows, q=q_ref[0, rows, :], k=kk_ref[krows, :], v=vv_ref[krows, :],
                bias=bias_ref[bi, first, 0],
                m=jnp.concatenate([m_ref[0, rows, :], m_ref[1, rows, :]], axis=0),
                l=jnp.concatenate([l_ref[0, rows, :], l_ref[1, rows, :]], axis=0),
                acc=acc_ref[rows, :])

        def compute(b):
            q = b["q"]
            q2 = jnp.concatenate([jnp.where(head0, q, 0.0), jnp.where(head0, 0.0, q)],
                                 axis=0).astype(BF16)
            s = _dot_nt(q2, b["k"].astype(BF16)) + b["bias"]
            m_new = jnp.maximum(b["m"], jnp.max(s, axis=-1, keepdims=True))
            alpha = jnp.exp(b["m"] - m_new)
            p = jnp.exp(s - jnp.concatenate([m_new, m_new], axis=1))
            l_new = alpha * b["l"] + jnp.sum(p, axis=-1, keepdims=True)
            pv = _dot(p.astype(BF16), b["v"].astype(BF16))
            acc_new = jnp.where(head0, alpha[:Q_BLOCK] * b["acc"] + pv[:Q_BLOCK],
                                alpha[Q_BLOCK:] * b["acc"] + pv[Q_BLOCK:])
            return m_new, l_new, acc_new

        def store(b, res):
            m_new, l_new, acc_new = res
            rows = b["rows"]
            acc_ref[rows, :] = acc_new
            m_ref[0, rows, :] = m_new[:Q_BLOCK]
            m_ref[1, rows, :] = m_new[Q_BLOCK:]
            l_ref[0, rows, :] = l_new[:Q_BLOCK]
            l_ref[1, rows, :] = l_new[Q_BLOCK:]

        def two_blocks(i, carry, load=load, compute=compute, store=store, half=n_iter // 2):
            blocks = [load(i), load(i + half)]
            results = [compute(b) for b in blocks]
            for b, res in zip(blocks, results):
                store(b, res)
            return carry

        lax.fori_loop(0, n_iter // 2, two_blocks, 0)

    lane_c = lax.broadcasted_iota(jnp.int32, (A_CHUNK, LANES), 1)
    denom = jnp.where(lane_c < HEAD_DIM, l_ref[0], l_ref[1])
    o_ref[0] = (acc_ref[...] / denom).astype(BF16)


def _dilated_attention(qa, ka, va, bias_a):
    b, s, wa = qa.shape
    n_pairs = wa // LANES
    cur = pl.BlockSpec((1, A_CHUNK, LANES), lambda i, c, p: (i, c, p))
    prev = pl.BlockSpec((1, A_CHUNK, LANES), lambda i, c, p: (i, jnp.maximum(c - 1, 0), p))
    nb = len(DILATED_BRANCHES)
    bias = pl.BlockSpec((nb, 2, 1, 2 * Q_BLOCK, 2 * Q_BLOCK), lambda i, c, p: (0, 0, p, 0, 0))
    return pl.pallas_call(
        _dilated_kernel,
        out_shape=jax.ShapeDtypeStruct((b, s, wa), BF16),
        grid=(b, s // A_CHUNK, n_pairs),
        in_specs=[cur, prev, cur, prev, cur, bias],
        out_specs=cur,
        scratch_shapes=[pltpu.VMEM((2 * A_CHUNK, LANES), F32),
                        pltpu.VMEM((2 * A_CHUNK, LANES), F32),
                        pltpu.VMEM((2, A_CHUNK, LANES), F32),
                        pltpu.VMEM((2, A_CHUNK, LANES), F32),
                        pltpu.VMEM((A_CHUNK, LANES), F32)],
        compiler_params=_cparams("parallel", "parallel", "parallel"),
        name="dilated_attention",
    )(qa, ka, ka, va, va, bias_a)


def _diff_kernel(n_bias, qt_ref, k_ref, vt_ref, bias_ref, lam_ref, gain_ref, o_ref,
                 q4_ref, m_ref, mt_ref, acc_ref, s_ref, p_ref):
    qi = pl.program_id(2)
    t = ATT_TILE
    qt = qt_ref[0, 0]
    row = lax.broadcasted_iota(jnp.int32, (LANES, t), 0)
    for g in range(4):
        sel = jnp.logical_and(row >= g * DIFF_HALF, row < (g + 1) * DIFF_HALF)
        q4_ref[:, g * t:(g + 1) * t] = jnp.where(sel, qt, jnp.zeros_like(qt))
    m_ref[...] = jnp.full(m_ref.shape, NEG_INF, F32)
    acc_ref[...] = jnp.zeros(acc_ref.shape, F32)
    p_ref[...] = jnp.zeros(p_ref.shape, BF16)
    ones = jnp.ones((ONES_ROWS, 2 * t), BF16)

    def tile_of(step):
        return jnp.clip(qi - step, 0, qi)

    def values(i):
        vt = jnp.concatenate([vt_ref[0, tile_of(2 * i)], vt_ref[0, tile_of(2 * i + 1)]], axis=1)
        return _dot(jnp.concatenate([vt, ones], axis=0), p_ref[...])

    def logits(i):
        tile_max = None
        for half in range(2):
            step = 2 * i + half
            k = k_ref[0, pl.ds(pl.multiple_of(tile_of(step) * t, t), t), :]
            d = jnp.where(step > qi, n_bias, jnp.minimum(step, n_bias - 1))
            b0 = bias_ref[d, 0]
            b1 = bias_ref[d, 1]
            s = _dot(k, q4_ref[...]) + jnp.concatenate([b0, b0, b1, b1], axis=1)
            s_ref[half * t:(half + 1) * t] = s
            mx = jnp.max(s, axis=0, keepdims=True)
            tile_max = mx if tile_max is None else jnp.maximum(tile_max, mx)
        mt_ref[...] = tile_max

    def sweep(i, carry):
        pv = values(i - 1)
        m_old = m_ref[...]
        m_new = jnp.maximum(m_old, mt_ref[...])
        alpha = jnp.exp2(m_old - m_new)
        m_ref[...] = m_new
        p_ref[...] = jnp.exp2(s_ref[...] - m_new).astype(BF16)
        acc_ref[...] = alpha * (acc_ref[...] + pv)
        logits(i + 1)
        return carry

    logits(0)
    n_iter = (qi + 2) // 2
    lax.fori_loop(0, n_iter, sweep, 0)
    acc = acc_ref[...] + values(n_iter - 1)

    pv = acc[0:LANES] / acc[LANES:LANES + 1]
    lam = lam_ref[...]
    outs = []
    for h in range(2):
        rows = slice(h * HEAD_DIM, (h + 1) * HEAD_DIM)
        diff = pv[rows, (2 * h) * t:(2 * h + 1) * t] - lam * pv[rows, (2 * h + 1) * t:(2 * h + 2) * t]
        ms = jnp.mean(diff * diff, axis=0, keepdims=True)
        outs.append(diff * lax.rsqrt(ms + EPS))
    out_t = jnp.concatenate(outs, axis=0) * gain_ref[...]
    o_ref[0] = out_t.T.astype(BF16)


def _diff_attention(qbt, kb, vbt, bias_bt, lam, post_gain):
    b, s, wb = kb.shape
    n_pairs = wb // LANES
    n_bias = bias_bt.shape[0] - 1
    t = ATT_TILE
    nt = s // t
    qspec = pl.BlockSpec((1, 1, LANES, t), lambda i, p, j: (i, j, p, 0))
    kspec = pl.BlockSpec((1, s, LANES), lambda i, p, j: (i, 0, p))
    vspec = pl.BlockSpec((1, nt, LANES, t), lambda i, p, j: (i, 0, p, 0))
    bspec = pl.BlockSpec((n_bias + 1, 2, t, t), lambda i, p, j: (0, p, 0, 0))
    return pl.pallas_call(
        functools.partial(_diff_kernel, n_bias),
        out_shape=jax.ShapeDtypeStruct((b, s, wb), BF16),
        grid=(b, n_pairs, nt),
        in_specs=[qspec, kspec, vspec, bspec,
                  pl.BlockSpec((1, t), lambda i, p, j: (0, 0)),
                  pl.BlockSpec((LANES, t), lambda i, p, j: (0, 0))],
        out_specs=pl.BlockSpec((1, t, LANES), lambda i, p, j: (i, j, p)),
        scratch_shapes=[pltpu.VMEM((LANES, 4 * t), BF16),
                        pltpu.VMEM((1, 4 * t), F32),
                        pltpu.VMEM((1, 4 * t), F32),
                        pltpu.VMEM((LANES + ONES_ROWS, 4 * t), F32),
                        pltpu.VMEM((2 * t, 4 * t), F32),
                        pltpu.VMEM((2 * t, 4 * t), BF16)],
        compiler_params=_cparams("parallel", "parallel", "parallel"),
        name="diff_attention",
    )(qbt.reshape(b, nt, wb, t), kb, vbt.reshape(b, nt, wb, t), bias_bt, lam, post_gain)


def _stick_kernel(qt_ref, k_ref, vt_ref, tri_ref, o_ref, q2_ref, carry_ref, scale_ref, acc_ref,
                  z_ref, lw_ref, p_ref):
    qi = pl.program_id(2)
    t = ATT_TILE
    qt = qt_ref[0, 0]
    row = lax.broadcasted_iota(jnp.int32, (LANES, t), 0)
    zero = jnp.zeros_like(qt)
    q2_ref[:, 0:t] = jnp.where(row < HEAD_DIM, qt, zero)
    q2_ref[:, t:] = jnp.where(row < HEAD_DIM, zero, qt)
    carry_ref[...] = jnp.zeros(carry_ref.shape, F32)
    scale_ref[...] = jnp.ones(scale_ref.shape, F32)
    acc_ref[...] = jnp.zeros(acc_ref.shape, F32)
    p_ref[...] = jnp.zeros(p_ref.shape, BF16)

    def tile_of(step):
        return jnp.clip(qi - step, 0, qi)

    def logits(i, diagonal=False):
        for half in range(2):
            k = k_ref[0, pl.ds(pl.multiple_of(tile_of(2 * i + half) * t, t), t), :]
            z = _dot(k, q2_ref[...])
            if diagonal and half == 0:
                key = lax.broadcasted_iota(jnp.int32, (t, t), 0)
                qry = lax.broadcasted_iota(jnp.int32, (t, t), 1)
                strict = jnp.concatenate([key < qry, key < qry], axis=1)
                z = jnp.where(strict, z, -SB_MASK * LOG2E)
            z_ref[half] = z

    def log_weights():
        for half in range(2):
            z = z_ref[half]
            neg_abs = pltpu.bitcast(pltpu.bitcast(z, jnp.uint32) | jnp.uint32(0x80000000), F32)
            sp = jnp.maximum(z, 0.0) + jnp.log(1.0 + jnp.exp2(neg_abs)) * LOG2E
            w = _dot(tri_ref[...], sp.astype(BF16))
            lw_ref[half, 0:t] = z + w[0:t]
            lw_ref[half, t:] = w[t:]

    def weights():
        carry = carry_ref[...]
        near_sum = lw_ref[0, t:t + 1]
        p_ref[0:t] = jnp.exp2(lw_ref[0, 0:t]).astype(BF16)
        p_ref[t:] = jnp.exp2(lw_ref[1, 0:t] + near_sum).astype(BF16)
        scale_ref[...] = jnp.exp2(carry)
        carry_ref[...] = carry + near_sum + lw_ref[1, t:t + 1]

    def values(i):
        vts = []
        for half in range(2):
            step = 2 * i + half
            valid = jnp.logical_and(step >= 0, step <= qi)
            vt = vt_ref[0, tile_of(step)]
            vts.append(jnp.where(valid, vt, jnp.zeros_like(vt)))
        acc_ref[...] += _dot(jnp.concatenate(vts, axis=1), p_ref[...]) * scale_ref[...]

    def sweep(i, c):
        values(i - 1)
        weights()
        log_weights()
        logits(i + 2)
        return c

    logits(0, diagonal=True)
    log_weights()
    logits(1)
    n_iter = (qi + 2) // 2
    lax.fori_loop(0, n_iter, sweep, 0)
    values(n_iter - 1)
    out_t = jnp.concatenate([acc_ref[0:HEAD_DIM, 0:t], acc_ref[HEAD_DIM:, t:]], axis=0)
    o_ref[0] = out_t.T.astype(BF16)


def _stick_attention(qct, kc, vct):
    b, s, wc = kc.shape
    n_pairs = wc // LANES
    t = ATT_TILE
    nt = s // t
    idx = np.arange(t)
    tri = np.concatenate([idx[None, :] >= idx[:, None], np.ones((ONES_ROWS, t), bool)], axis=0)
    tri = -jnp.asarray(tri, dtype=BF16)
    return pl.pallas_call(
        _stick_kernel,
        out_shape=jax.ShapeDtypeStruct((b, s, wc), BF16),
        grid=(b, n_pairs, nt),
        in_specs=[pl.BlockSpec((1, 1, LANES, t), lambda i, p, j: (i, j, p, 0)),
                  pl.BlockSpec((1, s, LANES), lambda i, p, j: (i, 0, p)),
                  pl.BlockSpec((1, nt, LANES, t), lambda i, p, j: (i, 0, p, 0)),
                  pl.BlockSpec((t + ONES_ROWS, t), lambda i, p, j: (0, 0))],
        out_specs=pl.BlockSpec((1, t, LANES), lambda i, p, j: (i, j, p)),
        scratch_shapes=[pltpu.VMEM((LANES, 2 * t), BF16),
                        pltpu.VMEM((1, 2 * t), F32),
                        pltpu.VMEM((1, 2 * t), F32),
                        pltpu.VMEM((LANES, 2 * t), F32),
                        pltpu.VMEM((2, t, 2 * t), F32),
                        pltpu.VMEM((2, t + ONES_ROWS, 2 * t), F32),
                        pltpu.VMEM((2 * t, 2 * t), BF16)],
        compiler_params=_cparams("parallel", "parallel", "parallel"),
        name="stick_breaking_attention",
    )(qct.reshape(b, nt, wc, t), kc, vct.reshape(b, nt, wc, t), tri)


def _out_proj_kernel(widths, x_ref, a_ref, b_ref, c_ref, w_ref, o_ref):
    wa, wb, wc = widths
    y = _dot(a_ref[...], w_ref[0:wa])
    y += _dot(b_ref[...], w_ref[wa:wa + wb])
    y += _dot(c_ref[...], w_ref[wa + wb:wa + wb + wc])
    o_ref[...] = x_ref[...] + y


def _out_proj(x2, oa, ob, oc, w_out, widths):
    t, d = x2.shape
    row = lambda w: pl.BlockSpec((ROW_TILE, w), lambda i: (i, 0))
    return pl.pallas_call(
        functools.partial(_out_proj_kernel, widths),
        out_shape=jax.ShapeDtypeStruct((t, d), F32),
        grid=(t // ROW_TILE,),
        in_specs=[row(d), row(widths[0]), row(widths[1]), row(widths[2]),
                  pl.BlockSpec(w_out.shape, lambda i: (0, 0))],
        out_specs=row(d),
        compiler_params=_cparams("parallel"),
        name="out_proj_residual",
    )(x2, oa, ob, oc, w_out)


def _t5_bucket(dist):
    dist = jnp.maximum(dist, 0)
    max_exact = N_BUCKETS // 2
    d_f = jnp.maximum(dist, 1).astype(F32)
    large = max_exact + (jnp.log(d_f / max_exact) / math.log(MAX_DISTANCE / max_exact)
                         * (N_BUCKETS - max_exact)).astype(jnp.int32)
    large = jnp.minimum(large, N_BUCKETS - 1)
    return jnp.where(dist < max_exact, dist, large)


def _dilated_bias_table(bias_a):
    n_heads = bias_a.shape[1]
    i = jnp.arange(Q_BLOCK, dtype=jnp.int32)[:, None]
    j = jnp.arange(2 * Q_BLOCK, dtype=jnp.int32)[None, :]
    tables = []
    for window, dil in DILATED_BRANCHES:
        n = window // dil
        off = i + n - j
        band = (off >= 0) & (off <= n)
        bias = jnp.moveaxis(jnp.take(bias_a, _t5_bucket(off * dil), axis=0), -1, 0)
        variants = [jnp.where(valid[None], bias, NEG_INF) for valid in (band, band & (j >= n))]
        tables.append(jnp.stack(variants))
    table = jnp.stack(tables)
    return table.reshape(len(DILATED_BRANCHES), 2, n_heads // 2, 2 * Q_BLOCK, 2 * Q_BLOCK)


def _diff_bias_table(bias_b, seq):
    t = ATT_TILE
    n_bias = min(seq // t, MAX_DISTANCE // t + 2)
    by_dist = jnp.take(bias_b, _t5_bucket(jnp.arange(seq, dtype=jnp.int32)), axis=0).T
    key = jnp.arange(t, dtype=jnp.int32)[:, None]
    qry = jnp.arange(t, dtype=jnp.int32)[None, :]
    tiles = []
    for d in range(n_bias):
        dist = d * t + qry - key
        tile = jnp.take(by_dist, jnp.clip(dist, 0, seq - 1), axis=1)
        tiles.append(jnp.where(dist >= 0, tile * LOG2E, NEG_INF))
    tiles.append(jnp.full_like(tiles[0], NEG_INF))
    return jnp.stack(tiles)


def kernel(x, rel_bias, ffn1_norm, ffn1_w_gate, ffn1_w_up, ffn1_w_down, mix_norm, w_in,
           q_norm_a, k_norm_a, q_norm_b, k_norm_b, lambda_q1, lambda_k1, lambda_q2, lambda_k2,
           diff_subln, w_out, ffn2_norm, ffn2_w_gate, ffn2_w_up, ffn2_w_down):
    b, s, d = x.shape
    depth = w_in.shape[0]
    n_heads = d // HEAD_DIM
    wa = (n_heads // 2) * HEAD_DIM
    wb = (n_heads // 4) * HEAD_DIM
    wc = d - wa - wb
    widths = (wa, wb, wc)
    assert w_in.shape[2] == 3 * d and s % A_CHUNK == 0 and (b * s) % ROW_TILE == 0

    rb = rel_bias.astype(F32)
    bias_a = _dilated_bias_table(rb[:, :wa // HEAD_DIM])
    bias_bt = _diff_bias_table(rb[:, wa // HEAD_DIM:], s)

    x2 = x.reshape(b * s, d)
    for layer in range(depth):
        x2 = _ffn(x2, ffn1_norm[layer], ffn1_w_gate[layer].astype(BF16),
                  ffn1_w_up[layer].astype(BF16), ffn1_w_down[layer].astype(BF16))

        qa, ka, va, kb, kc, qbt, vbt, qct, vct = _proj(
            x2, mix_norm[layer], w_in[layer], q_norm_a[layer], k_norm_a[layer],
            q_norm_b[layer], k_norm_b[layer], widths)
        seq3 = lambda t: t.reshape(b, s, t.shape[-1])

        out_a = _dilated_attention(seq3(qa), seq3(ka), seq3(va), bias_a)

        lam_init = 0.8 - 0.6 * math.exp(-0.3 * layer)
        lam = (jnp.exp(jnp.sum(lambda_q1[layer].astype(F32) * lambda_k1[layer].astype(F32)))
               - jnp.exp(jnp.sum(lambda_q2[layer].astype(F32) * lambda_k2[layer].astype(F32)))
               + lam_init)
        lam_row = jnp.full((1, ATT_TILE), lam, F32)
        post_gain = jnp.broadcast_to(
            (jnp.tile(diff_subln[layer].astype(F32), LANES // HEAD_DIM) * (1.0 - lam_init))[:, None],
            (LANES, ATT_TILE))
        out_b = _diff_attention(qbt, seq3(kb), vbt, bias_bt, lam_row, post_gain)

        out_c = _stick_attention(qct, seq3(kc), vct)

        x2 = _out_proj(x2, out_a.reshape(b * s, wa), out_b.reshape(b * s, wb),
                       out_c.reshape(b * s, wc), w_out[layer].astype(BF16), widths)

        x2 = _ffn(x2, ffn2_norm[layer], ffn2_w_gate[layer].astype(BF16),
                  ffn2_w_up[layer].astype(BF16), ffn2_w_down[layer].astype(BF16))
    return x2.reshape(b, s, d)
```

```python
import functools
import math

import jax
import jax.numpy as jnp
import numpy as np
from jax import lax
from jax.experimental import pallas as pl
from jax.experimental.pallas import tpu as pltpu

F32 = jnp.float32
BF16 = jnp.bfloat16

HEAD_DIM = 64
DIFF_HALF = HEAD_DIM // 2
N_BUCKETS = 32
MAX_DISTANCE = 2048
DILATED_BRANCHES = ((128, 1), (512, 4), (2048, 16))
Q_BLOCK = 128
EPS = 1e-6
NEG_INF = -1e30
SB_MASK = 1e4
LOG2E = math.log2(math.e)

LANES = 128
A_CHUNK = 2048
ATT_TILE = 256
ROW_TILE = 512
ONES_ROWS = 16
VMEM_LIMIT = 56 * 1024 * 1024


def _cparams(*sem):
    return pltpu.CompilerParams(dimension_semantics=sem, vmem_limit_bytes=VMEM_LIMIT)


def _rms(x, gain_row):
    ms = jnp.mean(x * x, axis=-1, keepdims=True)
    return x * lax.rsqrt(ms + EPS) * gain_row


def _dot(a, b):
    return jnp.dot(a, b, preferred_element_type=F32)


def _dot_nt(a, b):
    return lax.dot_general(a, b, (((1,), (1,)), ((), ())), preferred_element_type=F32)


def _ffn_kernel(x_ref, g_ref, wg_ref, wu_ref, wd_ref, o_ref):
    x = x_ref[...]
    h = _rms(x, g_ref[...]).astype(BF16)
    gate = _dot(h, wg_ref[...])
    up = _dot(h, wu_ref[...])
    act = (gate * jax.nn.sigmoid(gate) * up).astype(BF16)
    o_ref[...] = x + 0.5 * _dot(act, wd_ref[...])


def _ffn(x2, gain, wg, wu, wd):
    t, d = x2.shape
    dff = wg.shape[1]
    row = pl.BlockSpec((ROW_TILE, d), lambda i: (i, 0))
    full = lambda shape: pl.BlockSpec(shape, lambda i: (0, 0))
    return pl.pallas_call(
        _ffn_kernel,
        out_shape=jax.ShapeDtypeStruct((t, d), F32),
        grid=(t // ROW_TILE,),
        in_specs=[row, full((1, d)), full((d, dff)), full((d, dff)), full((dff, d))],
        out_specs=row,
        compiler_params=_cparams("parallel"),
        name="ffn_half_step",
    )(x2, gain.reshape(1, d), wg, wu, wd)


def _proj_kernel(widths, x_ref, g_ref, wn_ref, wt_ref, gqa_ref, gka_ref, gkb_ref, gqb_ref,
                 g64_ref, g32_ref, qa_ref, ka_ref, va_ref, kb_ref, kc_ref,
                 qbt_ref, vbt_ref, qct_ref, vct_ref):
    wa, wb, wc = widths
    h = _rms(x_ref[...], g_ref[...]).astype(BF16)
    sub = 256

    def group_norm(t, ones_ref, gain_ref, group, scale):
        ss = _dot((t * t).astype(BF16), ones_ref[...])
        return t * lax.rsqrt(ss * (1.0 / group) + EPS) * (gain_ref[...] * scale)

    def group_norm_t(t, ones_ref, gain_ref, group, scale):
        ss = _dot(ones_ref[...], (t * t).astype(BF16))
        return t * lax.rsqrt(ss * (1.0 / group) + EPS) * (gain_ref[...] * scale)

    col = 0
    for ref, width, post in (
            (qa_ref, wa, lambda t: group_norm(t, g64_ref, gqa_ref, HEAD_DIM, HEAD_DIM ** -0.5)),
            (ka_ref, wa, lambda t: group_norm(t, g64_ref, gka_ref, HEAD_DIM, 1.0)),
            (va_ref, wa, lambda t: t),
            (kb_ref, wb, lambda t: group_norm(t, g32_ref, gkb_ref, DIFF_HALF, 1.0)),
            (kc_ref, wc, lambda t: t)):
        for c in range(0, width, sub):
            t = _dot(h, wn_ref[:, col + c:col + c + sub])
            ref[:, c:c + sub] = post(t).astype(ref.dtype)
        col += width

    row = 0
    for ref, width, post in (
            (qbt_ref, wb, lambda t: group_norm_t(t, g32_ref, gqb_ref, DIFF_HALF,
                                                 DIFF_HALF ** -0.5 * LOG2E)),
            (vbt_ref, wb, lambda t: t),
            (qct_ref, wc, lambda t: t * (HEAD_DIM ** -0.5 * LOG2E)),
            (vct_ref, wc, lambda t: t)):
        for r in range(0, width, sub):
            t = post(_dot_nt(wt_ref[row + r:row + r + sub, :], h)).astype(BF16)
            for j in range(ROW_TILE // ATT_TILE):
                ref[j, r:r + sub, :] = t[:, j * ATT_TILE:(j + 1) * ATT_TILE]
        row += width


def _block_diag_ones(group):
    idx = np.arange(256) // group
    return jnp.asarray(idx[:, None] == idx[None, :], dtype=BF16)


def _proj(x2, gain, w_in, gqa, gka, gqb, gkb, widths):
    t, d = x2.shape
    wa, wb, wc = widths
    assert wa % 256 == 0 and wb % 256 == 0 and wc % 256 == 0
    w = w_in.astype(BF16)
    o = np.cumsum([0, wa, wa, wa, wb, wb, wb, wc, wc, wc])
    sec = lambda i: w[:, o[i]:o[i + 1]]
    w_nat = jnp.concatenate([sec(0), sec(1), sec(2), sec(4), sec(7)], axis=1)
    w_tr = jnp.concatenate([sec(3), sec(5), sec(6), sec(8)], axis=1).T
    row = lambda wd: pl.BlockSpec((ROW_TILE, wd), lambda i: (i, 0))
    full = lambda shape: pl.BlockSpec(shape, lambda i: (0,) * len(shape))
    slab = lambda wd: pl.BlockSpec((ROW_TILE // ATT_TILE, wd, ATT_TILE), lambda i: (i, 0, 0))
    tile256 = lambda g: jnp.tile(g.astype(F32), 256 // g.shape[0])
    out_shape = ([jax.ShapeDtypeStruct((t, wa), F32)] * 3
                 + [jax.ShapeDtypeStruct((t, wb), BF16), jax.ShapeDtypeStruct((t, wc), BF16)]
                 + [jax.ShapeDtypeStruct((t // ATT_TILE, wd, ATT_TILE), BF16)
                    for wd in (wb, wb, wc, wc)])
    return pl.pallas_call(
        functools.partial(_proj_kernel, widths),
        out_shape=out_shape,
        grid=(t // ROW_TILE,),
        in_specs=[row(d), full((1, d)), full(w_nat.shape), full(w_tr.shape)]
                 + [full((1, 256))] * 3 + [full((256, 1))] + [full((256, 256))] * 2,
        out_specs=[row(wa)] * 3 + [row(wb), row(wc)] + [slab(wb), slab(wb), slab(wc), slab(wc)],
        compiler_params=_cparams("parallel"),
        name="norm_in_proj",
    )(x2, gain.reshape(1, d), w_nat, w_tr, tile256(gqa).reshape(1, 256), tile256(gka).reshape(1, 256),
      tile256(gkb).reshape(1, 256), tile256(gqb).reshape(256, 1),
      _block_diag_ones(HEAD_DIM), _block_diag_ones(DIFF_HALF))


def _dilated_kernel(q_ref, kp_ref, kc_ref, vp_ref, vc_ref, bias_ref, o_ref,
                    kk_ref, vv_ref, m_ref, l_ref, acc_ref):
    chunk = pl.program_id(1)
    kk_ref[0:A_CHUNK] = kp_ref[0]
    kk_ref[A_CHUNK:] = kc_ref[0]
    vv_ref[0:A_CHUNK] = vp_ref[0]
    vv_ref[A_CHUNK:] = vc_ref[0]
    m_ref[...] = jnp.full(m_ref.shape, NEG_INF, F32)
    l_ref[...] = jnp.zeros(l_ref.shape, F32)
    acc_ref[...] = jnp.zeros(acc_ref.shape, F32)

    lane = lax.broadcasted_iota(jnp.int32, (Q_BLOCK, LANES), 1)
    head0 = lane < HEAD_DIM

    for bi, (window, dil) in enumerate(DILATED_BRANCHES):
        nblk = A_CHUNK // (Q_BLOCK * dil)
        shift = int(math.log2(nblk))
        n_iter = dil * nblk
        ds = (lambda start, size, dil=dil:
              pl.ds(start, size, stride=dil) if dil > 1 else pl.ds(start, size))

        def load(idx, bi=bi, dil=dil, nblk=nblk, shift=shift, ds=ds):
            rho = idx >> shift
            t = idx & (nblk - 1)
            q_start = rho + t * (Q_BLOCK * dil)
            rows = ds(q_start, Q_BLOCK)
            krows = ds(A_CHUNK + q_start - Q_BLOCK * dil, 2 * Q_BLOCK)
            first = jnp.logical_and(chunk == 0, t == 0).astype(jnp.int32)
            return dict(
                rows=rows, q=q_ref[0, rows, :], k=kk_ref[krows, :], v=vv_ref[krows, :],
                bias=bias_ref[bi, first, 0],
                m=jnp.concatenate([m_ref[0, rows, :], m_ref[1, rows, :]], axis=0),
                l=jnp.concatenate([l_ref[0, rows, :], l_ref[1, rows, :]], axis=0),
                acc=acc_ref[rows, :])

        def compute(b):
            q = b["q"]
            q2 = jnp.concatenate([jnp.where(head0, q, 0.0), jnp.where(head0, 0.0, q)],
                                 axis=0).astype(BF16)
            s = _dot_nt(q2, b["k"].astype(BF16)) + b["bias"]
            m_new = jnp.maximum(b["m"], jnp.max(s, axis=-1, keepdims=True))
            alpha = jnp.exp(b["m"] - m_new)
            p = jnp.exp(s - jnp.concatenate([m_new, m_new], axis=1))
            l_new = alpha * b["l"] + jnp.sum(p, axis=-1, keepdims=True)
            pv = _dot(p.astype(BF16), b["v"].astype(BF16))
            acc_new = jnp.where(head0, alpha[:Q_BLOCK] * b["acc"] + pv[:Q_BLOCK],
                                alpha[Q_BLOCK:] * b["acc"] + pv[Q_BLOCK:])
            return m_new, l_new, acc_new

        def store(b, res):
            m_new, l_new, acc_new = res
            rows = b["rows"]
            acc_ref[rows, :] = acc_new
            m_ref[0, rows, :] = m_new[:Q_BLOCK]
            m_ref[1, rows, :] = m_new[Q_BLOCK:]
            l_ref[0, rows, :] = l_new[:Q_BLOCK]
            l_ref[1, rows, :] = l_new[Q_BLOCK:]

        def two_blocks(i, carry, load=load, compute=compute, store=store, half=n_iter // 2):
            blocks = [load(i), load(i + half)]
            results = [compute(b) for b in blocks]
            for b, res in zip(blocks, results):
                store(b, res)
            return carry

        lax.fori_loop(0, n_iter // 2, two_blocks, 0)

    lane_c = lax.broadcasted_iota(jnp.int32, (A_CHUNK, LANES), 1)
    denom = jnp.where(lane_c < HEAD_DIM, l_ref[0], l_ref[1])
    o_ref[0] = (acc_ref[...] / denom).astype(BF16)


def _dilated_attention(qa, ka, va, bias_a):
    b, s, wa = qa.shape
    n_pairs = wa // LANES
    cur = pl.BlockSpec((1, A_CHUNK, LANES), lambda i, c, p: (i, c, p))
    prev = pl.BlockSpec((1, A_CHUNK, LANES), lambda i, c, p: (i, jnp.maximum(c - 1, 0), p))
    nb = len(DILATED_BRANCHES)
    bias = pl.BlockSpec((nb, 2, 1, 2 * Q_BLOCK, 2 * Q_BLOCK), lambda i, c, p: (0, 0, p, 0, 0))
    return pl.pallas_call(
        _dilated_kernel,
        out_shape=jax.ShapeDtypeStruct((b, s, wa), BF16),
        grid=(b, s // A_CHUNK, n_pairs),
        in_specs=[cur, prev, cur, prev, cur, bias],
        out_specs=cur,
        scratch_shapes=[pltpu.VMEM((2 * A_CHUNK, LANES), F32),
                        pltpu.VMEM((2 * A_CHUNK, LANES), F32),
                        pltpu.VMEM((2, A_CHUNK, LANES), F32),
                        pltpu.VMEM((2, A_CHUNK, LANES), F32),
                        pltpu.VMEM((A_CHUNK, LANES), F32)],
        compiler_params=_cparams("parallel", "parallel", "parallel"),
        name="dilated_attention",
    )(qa, ka, ka, va, va, bias_a)


def _diff_kernel(n_bias, qt_ref, k_ref, vt_ref, bias_ref, lam_ref, gain_ref, o_ref,
                 q4_ref, m_ref, mt_ref, acc_ref, s_ref, p_ref):
    qi = pl.program_id(2)
    t = ATT_TILE
    qt = qt_ref[0, 0]
    row = lax.broadcasted_iota(jnp.int32, (LANES, t), 0)
    for g in range(4):
        sel = jnp.logical_and(row >= g * DIFF_HALF, row < (g + 1) * DIFF_HALF)
        q4_ref[:, g * t:(g + 1) * t] = jnp.where(sel, qt, jnp.zeros_like(qt))
    m_ref[...] = jnp.full(m_ref.shape, NEG_INF, F32)
    acc_ref[...] = jnp.zeros(acc_ref.shape, F32)
    p_ref[...] = jnp.zeros(p_ref.shape, BF16)
    ones = jnp.ones((ONES_ROWS, 2 * t), BF16)

    def tile_of(step):
        return jnp.clip(qi - step, 0, qi)

    def values(i):
        vt = jnp.concatenate([vt_ref[0, tile_of(2 * i)], vt_ref[0, tile_of(2 * i + 1)]], axis=1)
        return _dot(jnp.concatenate([vt, ones], axis=0), p_ref[...])

    def logits(i):
        tile_max = None
        for half in range(2):
            step = 2 * i + half
            k = k_ref[0, pl.ds(pl.multiple_of(tile_of(step) * t, t), t), :]
            d = jnp.where(step > qi, n_bias, jnp.minimum(step, n_bias - 1))
            b0 = bias_ref[d, 0]
            b1 = bias_ref[d, 1]
            s = _dot(k, q4_ref[...]) + jnp.concatenate([b0, b0, b1, b1], axis=1)
            s_ref[half * t:(half + 1) * t] = s
            mx = jnp.max(s, axis=0, keepdims=True)
            tile_max = mx if tile_max is None else jnp.maximum(tile_max, mx)
        mt_ref[...] = tile_max

    def sweep(i, carry):
        pv = values(i - 1)
        m_old = m_ref[...]
        m_new = jnp.maximum(m_old, mt_ref[...])
        alpha = jnp.exp2(m_old - m_new)
        m_ref[...] = m_new
        p_ref[...] = jnp.exp2(s_ref[...] - m_new).astype(BF16)
        acc_ref[...] = alpha * (acc_ref[...] + pv)
        logits(i + 1)
        return carry

    logits(0)
    n_iter = (qi + 2) // 2
    lax.fori_loop(0, n_iter, sweep, 0)
    acc = acc_ref[...] + values(n_iter - 1)

    pv = acc[0:LANES] / acc[LANES:LANES + 1]
    lam = lam_ref[...]
    outs = []
    for h in range(2):
        rows = slice(h * HEAD_DIM, (h + 1) * HEAD_DIM)
        diff = pv[rows, (2 * h) * t:(2 * h + 1) * t] - lam * pv[rows, (2 * h + 1) * t:(2 * h + 2) * t]
        ms = jnp.mean(diff * diff, axis=0, keepdims=True)
        outs.append(diff * lax.rsqrt(ms + EPS))
    out_t = jnp.concatenate(outs, axis=0) * gain_ref[...]
    o_ref[0] = out_t.T.astype(BF16)


def _diff_attention(qbt, kb, vbt, bias_bt, lam, post_gain):
    b, s, wb = kb.shape
    n_pairs = wb // LANES
    n_bias = bias_bt.shape[0] - 1
    t = ATT_TILE
    nt = s // t
    qspec = pl.BlockSpec((1, 1, LANES, t), lambda i, p, j: (i, j, p, 0))
    kspec = pl.BlockSpec((1, s, LANES), lambda i, p, j: (i, 0, p))
    vspec = pl.BlockSpec((1, nt, LANES, t), lambda i, p, j: (i, 0, p, 0))
    bspec = pl.BlockSpec((n_bias + 1, 2, t, t), lambda i, p, j: (0, p, 0, 0))
    return pl.pallas_call(
        functools.partial(_diff_kernel, n_bias),
        out_shape=jax.ShapeDtypeStruct((b, s, wb), BF16),
        grid=(b, n_pairs, nt),
        in_specs=[qspec, kspec, vspec, bspec,
                  pl.BlockSpec((1, t), lambda i, p, j: (0, 0)),
                  pl.BlockSpec((LANES, t), lambda i, p, j: (0, 0))],
        out_specs=pl.BlockSpec((1, t, LANES), lambda i, p, j: (i, j, p)),
        scratch_shapes=[pltpu.VMEM((LANES, 4 * t), BF16),
                        pltpu.VMEM((1, 4 * t), F32),
                        pltpu.VMEM((1, 4 * t), F32),
                        pltpu.VMEM((LANES + ONES_ROWS, 4 * t), F32),
                        pltpu.VMEM((2 * t, 4 * t), F32),
                        pltpu.VMEM((2 * t, 4 * t), BF16)],
        compiler_params=_cparams("parallel", "parallel", "parallel"),
        name="diff_attention",
    )(qbt.reshape(b, nt, wb, t), kb, vbt.reshape(b, nt, wb, t), bias_bt, lam, post_gain)


def _stick_kernel(qt_ref, k_ref, vt_ref, tri_ref, o_ref, q2_ref, carry_ref, scale_ref, acc_ref,
                  z_ref, lw_ref, p_ref):
    qi = pl.program_id(2)
    t = ATT_TILE
    qt = qt_ref[0, 0]
    row = lax.broadcasted_iota(jnp.int32, (LANES, t), 0)
    zero = jnp.zeros_like(qt)
    q2_ref[:, 0:t] = jnp.where(row < HEAD_DIM, qt, zero)
    q2_ref[:, t:] = jnp.where(row < HEAD_DIM, zero, qt)
    carry_ref[...] = jnp.zeros(carry_ref.shape, F32)
    scale_ref[...] = jnp.ones(scale_ref.shape, F32)
    acc_ref[...] = jnp.zeros(acc_ref.shape, F32)
    p_ref[...] = jnp.zeros(p_ref.shape, BF16)

    def tile_of(step):
        return jnp.clip(qi - step, 0, qi)

    def logits(i, slot, diagonal=False):
        for half in range(2):
            k = k_ref[0, pl.ds(pl.multiple_of(tile_of(2 * i + half) * t, t), t), :]
            z = _dot(k, q2_ref[...])
            if diagonal and half == 0:
                key = lax.broadcasted_iota(jnp.int32, (t, t), 0)
                qry = lax.broadcasted_iota(jnp.int32, (t, t), 1)
                strict = jnp.concatenate([key < qry, key < qry], axis=1)
                z = jnp.where(strict, z, -SB_MASK * LOG2E)
            z_ref[slot, half] = z

    def log_weights(slot):
        for half in range(2):
            z = z_ref[slot, half]
            neg_abs = pltpu.bitcast(pltpu.bitcast(z, jnp.uint32) | jnp.uint32(0x80000000), F32)
            sp = jnp.maximum(z, 0.0) + jnp.log(1.0 + jnp.exp2(neg_abs)) * LOG2E
            w = _dot(tri_ref[...], sp.astype(BF16))
            lw_ref[slot, half, 0:t] = z + w[0:t]
            lw_ref[slot, half, t:] = w[t:]

    def weights(slot):
        carry = carry_ref[...]
        near_sum = lw_ref[slot, 0, t:t + 1]
        p_ref[0:t] = jnp.exp2(lw_ref[slot, 0, 0:t]).astype(BF16)
        p_ref[t:] = jnp.exp2(lw_ref[slot, 1, 0:t] + near_sum).astype(BF16)
        scale_ref[...] = jnp.exp2(carry)
        carry_ref[...] = carry + near_sum + lw_ref[slot, 1, t:t + 1]

    def values(i):
        vts = []
        for half in range(2):
            step = 2 * i + half
            valid = jnp.logical_and(step >= 0, step <= qi)
            vt = vt_ref[0, tile_of(step)]
            vts.append(jnp.where(valid, vt, jnp.zeros_like(vt)))
        acc_ref[...] += _dot(jnp.concatenate(vts, axis=1), p_ref[...]) * scale_ref[...]

    def sweep(i, c):
        values(i - 1)
        weights(0)
        log_weights(0)
        logits(i + 2, 0)
        return c

    logits(0, 0, diagonal=True)
    log_weights(0)
    logits(1, 0)
    n_iter = (qi + 2) // 2
    lax.fori_loop(0, n_iter, sweep, 0)
    values(n_iter - 1)
    out_t = jnp.concatenate([acc_ref[0:HEAD_DIM, 0:t], acc_ref[HEAD_DIM:, t:]], axis=0)
    o_ref[0] = out_t.T.astype(BF16)


def _stick_attention(qct, kc, vct):
    b, s, wc = kc.shape
    n_pairs = wc // LANES
    t = ATT_TILE
    nt = s // t
    idx = np.arange(t)
    tri = np.concatenate([idx[None, :] >= idx[:, None], np.ones((ONES_ROWS, t), bool)], axis=0)
    tri = -jnp.asarray(tri, dtype=BF16)
    return pl.pallas_call(
        _stick_kernel,
        out_shape=jax.ShapeDtypeStruct((b, s, wc), BF16),
        grid=(b, n_pairs, nt),
        in_specs=[pl.BlockSpec((1, 1, LANES, t), lambda i, p, j: (i, j, p, 0)),
                  pl.BlockSpec((1, s, LANES), lambda i, p, j: (i, 0, p)),
                  pl.BlockSpec((1, nt, LANES, t), lambda i, p, j: (i, 0, p, 0)),
                  pl.BlockSpec((t + ONES_ROWS, t), lambda i, p, j: (0, 0))],
        out_specs=pl.BlockSpec((1, t, LANES), lambda i, p, j: (i, j, p)),
        scratch_shapes=[pltpu.VMEM((LANES, 2 * t), BF16),
                        pltpu.VMEM((1, 2 * t), F32),
                        pltpu.VMEM((1, 2 * t), F32),
                        pltpu.VMEM((LANES, 2 * t), F32),
                        pltpu.VMEM((2, 2, t, 2 * t), F32),
                        pltpu.VMEM((2, 2, t + ONES_ROWS, 2 * t), F32),
                        pltpu.VMEM((2 * t, 2 * t), BF16)],
        compiler_params=_cparams("parallel", "parallel", "parallel"),
        name="stick_breaking_attention",
    )(qct.reshape(b, nt, wc, t), kc, vct.reshape(b, nt, wc, t), tri)


def _out_proj_kernel(widths, x_ref, a_ref, b_ref, c_ref, w_ref, o_ref):
    wa, wb, wc = widths
    y = _dot(a_ref[...], w_ref[0:wa])
    y += _dot(b_ref[...], w_ref[wa:wa + wb])
    y += _dot(c_ref[...], w_ref[wa + wb:wa + wb + wc])
    o_ref[...] = x_ref[...] + y


def _out_proj(x2, oa, ob, oc, w_out, widths):
    t, d = x2.shape
    row = lambda w: pl.BlockSpec((ROW_TILE, w), lambda i: (i, 0))
    return pl.pallas_call(
        functools.partial(_out_proj_kernel, widths),
        out_shape=jax.ShapeDtypeStruct((t, d), F32),
        grid=(t // ROW_TILE,),
        in_specs=[row(d), row(widths[0]), row(widths[1]), row(widths[2]),
                  pl.BlockSpec(w_out.shape, lambda i: (0, 0))],
        out_specs=row(d),
        compiler_params=_cparams("parallel"),
        name="out_proj_residual",
    )(x2, oa, ob, oc, w_out)


def _t5_bucket(dist):
    dist = jnp.maximum(dist, 0)
    max_exact = N_BUCKETS // 2
    d_f = jnp.maximum(dist, 1).astype(F32)
    large = max_exact + (jnp.log(d_f / max_exact) / math.log(MAX_DISTANCE / max_exact)
                         * (N_BUCKETS - max_exact)).astype(jnp.int32)
    large = jnp.minimum(large, N_BUCKETS - 1)
    return jnp.where(dist < max_exact, dist, large)


def _bias_of_distance(bias, dist):
    bucket = _t5_bucket(dist)[None]
    out = jnp.zeros((bias.shape[1],) + dist.shape, F32)
    for b in range(N_BUCKETS):
        out = jnp.where(bucket == b, bias[b].reshape((-1,) + (1,) * dist.ndim), out)
    return out


def _dilated_bias_table(bias_a):
    n_heads = bias_a.shape[1]
    i = jnp.arange(Q_BLOCK, dtype=jnp.int32)[:, None]
    j = jnp.arange(2 * Q_BLOCK, dtype=jnp.int32)[None, :]
    tables = []
    for window, dil in DILATED_BRANCHES:
        n = window // dil
        off = i + n - j
        band = (off >= 0) & (off <= n)
        bias = _bias_of_distance(bias_a, off * dil)
        variants = [jnp.where(valid[None], bias, NEG_INF) for valid in (band, band & (j >= n))]
        tables.append(jnp.stack(variants))
    table = jnp.stack(tables)
    return table.reshape(len(DILATED_BRANCHES), 2, n_heads // 2, 2 * Q_BLOCK, 2 * Q_BLOCK)


def _diff_bias_table(bias_b, seq):
    t = ATT_TILE
    n_bias = min(seq // t, MAX_DISTANCE // t + 2)
    key = jnp.arange(t, dtype=jnp.int32)[None, :, None]
    qry = jnp.arange(t, dtype=jnp.int32)[None, None, :]
    dist = jnp.arange(n_bias + 1, dtype=jnp.int32)[:, None, None] * t + qry - key
    valid = (dist >= 0) & (jnp.arange(n_bias + 1)[:, None, None] < n_bias)
    tiles = jnp.where(valid[None], _bias_of_distance(bias_b, dist) * LOG2E, NEG_INF)
    return jnp.swapaxes(tiles, 0, 1)


def kernel(x, rel_bias, ffn1_norm, ffn1_w_gate, ffn1_w_up, ffn1_w_down, mix_norm, w_in,
           q_norm_a, k_norm_a, q_norm_b, k_norm_b, lambda_q1, lambda_k1, lambda_q2, lambda_k2,
           diff_subln, w_out, ffn2_norm, ffn2_w_gate, ffn2_w_up, ffn2_w_down):
    b, s, d = x.shape
    depth = w_in.shape[0]
    n_heads = d // HEAD_DIM
    wa = (n_heads // 2) * HEAD_DIM
    wb = (n_heads // 4) * HEAD_DIM
    wc = d - wa - wb
    widths = (wa, wb, wc)
    assert w_in.shape[2] == 3 * d and s % A_CHUNK == 0 and (b * s) % ROW_TILE == 0

    rb = rel_bias.astype(F32)
    bias_a = _dilated_bias_table(rb[:, :wa // HEAD_DIM])
    bias_bt = _diff_bias_table(rb[:, wa // HEAD_DIM:], s)

    x2 = x.reshape(b * s, d)
    for layer in range(depth):
        x2 = _ffn(x2, ffn1_norm[layer], ffn1_w_gate[layer].astype(BF16),
                  ffn1_w_up[layer].astype(BF16), ffn1_w_down[layer].astype(BF16))

        qa, ka, va, kb, kc, qbt, vbt, qct, vct = _proj(
            x2, mix_norm[layer], w_in[layer], q_norm_a[layer], k_norm_a[layer],
            q_norm_b[layer], k_norm_b[layer], widths)
        seq3 = lambda t: t.reshape(b, s, t.shape[-1])

        out_a = _dilated_attention(seq3(qa), seq3(ka), seq3(va), bias_a)

        lam_init = 0.8 - 0.6 * math.exp(-0.3 * layer)
        lam = (jnp.exp(jnp.sum(lambda_q1[layer].astype(F32) * lambda_k1[layer].astype(F32)))
               - jnp.exp(jnp.sum(lambda_q2[layer].astype(F32) * lambda_k2[layer].astype(F32)))
               + lam_init)
        lam_row = jnp.full((1, ATT_TILE), lam, F32)
        post_gain = jnp.broadcast_to(
            (jnp.tile(diff_subln[layer].astype(F32), LANES // HEAD_DIM) * (1.0 - lam_init))[:, None],
            (LANES, ATT_TILE))
        out_b = _diff_attention(qbt, seq3(kb), vbt, bias_bt, lam_row, post_gain)

        out_c = _stick_attention(qct, seq3(kc), vct)

        x2 = _out_proj(x2, out_a.reshape(b * s, wa), out_b.reshape(b * s, wb),
                       out_c.reshape(b * s, wc), w_out[layer].astype(BF16), widths)

        x2 = _ffn(x2, ffn2_norm[layer], ffn2_w_gate[layer].astype(BF16),
                  ffn2_w_up[layer].astype(BF16), ffn2_w_down[layer].astype(BF16))
    return x2.reshape(b, s, d)
```

```python
import functools
import math

import jax
import jax.numpy as jnp
import numpy as np
from jax import lax
from jax.experimental import pallas as pl
from jax.experimental.pallas import tpu as pltpu

F32 = jnp.float32
BF16 = jnp.bfloat16

HEAD_DIM = 64
DIFF_HALF = HEAD_DIM // 2
N_BUCKETS = 32
MAX_DISTANCE = 2048
DILATED_BRANCHES = ((128, 1), (512, 4), (2048, 16))
Q_BLOCK = 128
EPS = 1e-6
NEG_INF = -1e30
SB_MASK = 1e4
LOG2E = math.log2(math.e)

LANES = 128
A_CHUNK = 2048
ATT_TILE = 256
ROW_TILE = 512
ONES_ROWS = 16
QUERY_TILES = 2
VMEM_LIMIT = 56 * 1024 * 1024


def _cparams(*sem):
    return pltpu.CompilerParams(dimension_semantics=sem, vmem_limit_bytes=VMEM_LIMIT)


def _rms(x, gain_row):
    ms = jnp.mean(x * x, axis=-1, keepdims=True)
    return x * lax.rsqrt(ms + EPS) * gain_row


def _dot(a, b):
    return jnp.dot(a, b, preferred_element_type=F32)


def _dot_nt(a, b):
    return lax.dot_general(a, b, (((1,), (1,)), ((), ())), preferred_element_type=F32)


def _ffn_kernel(x_ref, g_ref, wg_ref, wu_ref, wd_ref, o_ref):
    x = x_ref[...]
    h = _rms(x, g_ref[...]).astype(BF16)
    gate = _dot(h, wg_ref[...])
    up = _dot(h, wu_ref[...])
    act = (gate * jax.nn.sigmoid(gate) * up).astype(BF16)
    o_ref[...] = x + 0.5 * _dot(act, wd_ref[...])


def _ffn(x2, gain, wg, wu, wd):
    t, d = x2.shape
    dff = wg.shape[1]
    row = pl.BlockSpec((ROW_TILE, d), lambda i: (i, 0))
    full = lambda shape: pl.BlockSpec(shape, lambda i: (0, 0))
    return pl.pallas_call(
        _ffn_kernel,
        out_shape=jax.ShapeDtypeStruct((t, d), F32),
        grid=(t // ROW_TILE,),
        in_specs=[row, full((1, d)), full((d, dff)), full((d, dff)), full((dff, d))],
        out_specs=row,
        compiler_params=_cparams("parallel"),
        name="ffn_half_step",
    )(x2, gain.reshape(1, d), wg, wu, wd)


def _proj_kernel(widths, x_ref, g_ref, wn_ref, wt_ref, gqa_ref, gka_ref, gkb_ref, gqb_ref,
                 g64_ref, g32_ref, qa_ref, ka_ref, va_ref, kb_ref, kc_ref,
                 qbt_ref, vbt_ref, qct_ref, vct_ref):
    wa, wb, wc = widths
    h = _rms(x_ref[...], g_ref[...]).astype(BF16)
    sub = 256

    def group_norm(t, ones_ref, gain_ref, group, scale):
        ss = _dot((t * t).astype(BF16), ones_ref[...])
        return t * lax.rsqrt(ss * (1.0 / group) + EPS) * (gain_ref[...] * scale)

    def group_norm_t(t, ones_ref, gain_ref, group, scale):
        ss = _dot(ones_ref[...], (t * t).astype(BF16))
        return t * lax.rsqrt(ss * (1.0 / group) + EPS) * (gain_ref[...] * scale)

    col = 0
    for ref, width, post in (
            (qa_ref, wa, lambda t: group_norm(t, g64_ref, gqa_ref, HEAD_DIM, HEAD_DIM ** -0.5)),
            (ka_ref, wa, lambda t: group_norm(t, g64_ref, gka_ref, HEAD_DIM, 1.0)),
            (va_ref, wa, lambda t: t),
            (kb_ref, wb, lambda t: group_norm(t, g32_ref, gkb_ref, DIFF_HALF, 1.0)),
            (kc_ref, wc, lambda t: t)):
        for c in range(0, width, sub):
            t = _dot(h, wn_ref[:, col + c:col + c + sub])
            ref[:, c:c + sub] = post(t).astype(ref.dtype)
        col += width

    row = 0
    for ref, width, post in (
            (qbt_ref, wb, lambda t: group_norm_t(t, g32_ref, gqb_ref, DIFF_HALF,
                                                 DIFF_HALF ** -0.5 * LOG2E)),
            (vbt_ref, wb, lambda t: t),
            (qct_ref, wc, lambda t: t * (HEAD_DIM ** -0.5 * LOG2E)),
            (vct_ref, wc, lambda t: t)):
        for r in range(0, width, sub):
            t = post(_dot_nt(wt_ref[row + r:row + r + sub, :], h)).astype(BF16)
            for j in range(ROW_TILE // ATT_TILE):
                ref[j, r:r + sub, :] = t[:, j * ATT_TILE:(j + 1) * ATT_TILE]
        row += width


def _block_diag_ones(group):
    idx = np.arange(256) // group
    return jnp.asarray(idx[:, None] == idx[None, :], dtype=BF16)


def _proj(x2, gain, w_in, gqa, gka, gqb, gkb, widths):
    t, d = x2.shape
    wa, wb, wc = widths
    assert wa % 256 == 0 and wb % 256 == 0 and wc % 256 == 0
    w = w_in.astype(BF16)
    o = np.cumsum([0, wa, wa, wa, wb, wb, wb, wc, wc, wc])
    sec = lambda i: w[:, o[i]:o[i + 1]]
    w_nat = jnp.concatenate([sec(0), sec(1), sec(2), sec(4), sec(7)], axis=1)
    w_tr = jnp.concatenate([sec(3), sec(5), sec(6), sec(8)], axis=1).T
    row = lambda wd: pl.BlockSpec((ROW_TILE, wd), lambda i: (i, 0))
    full = lambda shape: pl.BlockSpec(shape, lambda i: (0,) * len(shape))
    slab = lambda wd: pl.BlockSpec((ROW_TILE // ATT_TILE, wd, ATT_TILE), lambda i: (i, 0, 0))
    tile256 = lambda g: jnp.tile(g.astype(F32), 256 // g.shape[0])
    out_shape = ([jax.ShapeDtypeStruct((t, wa), F32)] * 3
                 + [jax.ShapeDtypeStruct((t, wb), BF16), jax.ShapeDtypeStruct((t, wc), BF16)]
                 + [jax.ShapeDtypeStruct((t // ATT_TILE, wd, ATT_TILE), BF16)
                    for wd in (wb, wb, wc, wc)])
    return pl.pallas_call(
        functools.partial(_proj_kernel, widths),
        out_shape=out_shape,
        grid=(t // ROW_TILE,),
        in_specs=[row(d), full((1, d)), full(w_nat.shape), full(w_tr.shape)]
                 + [full((1, 256))] * 3 + [full((256, 1))] + [full((256, 256))] * 2,
        out_specs=[row(wa)] * 3 + [row(wb), row(wc)] + [slab(wb), slab(wb), slab(wc), slab(wc)],
        compiler_params=_cparams("parallel"),
        name="norm_in_proj",
    )(x2, gain.reshape(1, d), w_nat, w_tr, tile256(gqa).reshape(1, 256), tile256(gka).reshape(1, 256),
      tile256(gkb).reshape(1, 256), tile256(gqb).reshape(256, 1),
      _block_diag_ones(HEAD_DIM), _block_diag_ones(DIFF_HALF))


def _dilated_kernel(q_ref, kp_ref, kc_ref, vp_ref, vc_ref, bias_ref, o_ref,
                    kk_ref, vv_ref, m_ref, l_ref, acc_ref):
    chunk = pl.program_id(1)
    kk_ref[0:A_CHUNK] = kp_ref[0]
    kk_ref[A_CHUNK:] = kc_ref[0]
    vv_ref[0:A_CHUNK] = vp_ref[0]
    vv_ref[A_CHUNK:] = vc_ref[0]
    m_ref[...] = jnp.full(m_ref.shape, NEG_INF, F32)
    l_ref[...] = jnp.zeros(l_ref.shape, F32)
    acc_ref[...] = jnp.zeros(acc_ref.shape, F32)

    lane = lax.broadcasted_iota(jnp.int32, (Q_BLOCK, LANES), 1)
    head0 = lane < HEAD_DIM

    for bi, (window, dil) in enumerate(DILATED_BRANCHES):
        nblk = A_CHUNK // (Q_BLOCK * dil)
        shift = int(math.log2(nblk))
        n_iter = dil * nblk
        ds = (lambda start, size, dil=dil:
              pl.ds(start, size, stride=dil) if dil > 1 else pl.ds(start, size))

        def load(idx, bi=bi, dil=dil, nblk=nblk, shift=shift, ds=ds):
            rho = idx >> shift
            t = idx & (nblk - 1)
            q_start = rho + t * (Q_BLOCK * dil)
            rows = ds(q_start, Q_BLOCK)
            krows = ds(A_CHUNK + q_start - Q_BLOCK * dil, 2 * Q_BLOCK)
            first = jnp.logical_and(chunk == 0, t == 0).astype(jnp.int32)
            return dict(
                rows=rows, q=q_ref[0, rows, :], k=kk_ref[krows, :], v=vv_ref[krows, :],
                bias=bias_ref[bi, first, 0],
                m=jnp.concatenate([m_ref[0, rows, :], m_ref[1, rows, :]], axis=0),
                l=jnp.concatenate([l_ref[0, rows, :], l_ref[1, rows, :]], axis=0),
                acc=acc_ref[rows, :])

        def compute(b):
            q = b["q"]
            q2 = jnp.concatenate([jnp.where(head0, q, 0.0), jnp.where(head0, 0.0, q)],
                                 axis=0).astype(BF16)
            s = _dot_nt(q2, b["k"].astype(BF16)) + b["bias"]
            m_new = jnp.maximum(b["m"], jnp.max(s, axis=-1, keepdims=True))
            alpha = jnp.exp(b["m"] - m_new)
            p = jnp.exp(s - jnp.concatenate([m_new, m_new], axis=1))
            l_new = alpha * b["l"] + jnp.sum(p, axis=-1, keepdims=True)
            pv = _dot(p.astype(BF16), b["v"].astype(BF16))
            acc_new = jnp.where(head0, alpha[:Q_BLOCK] * b["acc"] + pv[:Q_BLOCK],
                                alpha[Q_BLOCK:] * b["acc"] + pv[Q_BLOCK:])
            return m_new, l_new, acc_new

        def store(b, res):
            m_new, l_new, acc_new = res
            rows = b["rows"]
            acc_ref[rows, :] = acc_new
            m_ref[0, rows, :] = m_new[:Q_BLOCK]
            m_ref[1, rows, :] = m_new[Q_BLOCK:]
            l_ref[0, rows, :] = l_new[:Q_BLOCK]
            l_ref[1, rows, :] = l_new[Q_BLOCK:]

        def two_blocks(i, carry, load=load, compute=compute, store=store, half=n_iter // 2):
            blocks = [load(i), load(i + half)]
            results = [compute(b) for b in blocks]
            for b, res in zip(blocks, results):
                store(b, res)
            return carry

        lax.fori_loop(0, n_iter // 2, two_blocks, 0)

    lane_c = lax.broadcasted_iota(jnp.int32, (A_CHUNK, LANES), 1)
    denom = jnp.where(lane_c < HEAD_DIM, l_ref[0], l_ref[1])
    o_ref[0] = (acc_ref[...] / denom).astype(BF16)


def _dilated_attention(qa, ka, va, bias_a):
    b, s, wa = qa.shape
    n_pairs = wa // LANES
    cur = pl.BlockSpec((1, A_CHUNK, LANES), lambda i, c, p: (i, c, p))
    prev = pl.BlockSpec((1, A_CHUNK, LANES), lambda i, c, p: (i, jnp.maximum(c - 1, 0), p))
    nb = len(DILATED_BRANCHES)
    bias = pl.BlockSpec((nb, 2, 1, 2 * Q_BLOCK, 2 * Q_BLOCK), lambda i, c, p: (0, 0, p, 0, 0))
    return pl.pallas_call(
        _dilated_kernel,
        out_shape=jax.ShapeDtypeStruct((b, s, wa), BF16),
        grid=(b, s // A_CHUNK, n_pairs),
        in_specs=[cur, prev, cur, prev, cur, bias],
        out_specs=cur,
        scratch_shapes=[pltpu.VMEM((2 * A_CHUNK, LANES), F32),
                        pltpu.VMEM((2 * A_CHUNK, LANES), F32),
                        pltpu.VMEM((2, A_CHUNK, LANES), F32),
                        pltpu.VMEM((2, A_CHUNK, LANES), F32),
                        pltpu.VMEM((A_CHUNK, LANES), F32)],
        compiler_params=_cparams("parallel", "parallel", "parallel"),
        name="dilated_attention",
    )(qa, ka, ka, va, va, bias_a)


def _diff_kernel(n_bias, qt_ref, k_ref, vt_ref, bias_ref, lam_ref, gain_ref, o_ref,
                 q4_ref, m_ref, mt_ref, acc_ref, s_ref, p_ref):
    t = ATT_TILE
    row = lax.broadcasted_iota(jnp.int32, (LANES, t), 0)
    ones = jnp.ones((ONES_ROWS, 2 * t), BF16)

    def pipeline(g):
        qi = pl.program_id(2) * QUERY_TILES + g

        def tile_of(step):
            return jnp.clip(qi - step, 0, qi)

        def init():
            qt = qt_ref[0, g]
            for c in range(4):
                sel = jnp.logical_and(row >= c * DIFF_HALF, row < (c + 1) * DIFF_HALF)
                q4_ref[g, :, c * t:(c + 1) * t] = jnp.where(sel, qt, jnp.zeros_like(qt))
            m_ref[g] = jnp.full(m_ref.shape[1:], NEG_INF, F32)
            acc_ref[g] = jnp.zeros(acc_ref.shape[1:], F32)
            p_ref[g] = jnp.zeros(p_ref.shape[1:], BF16)

        def values(i):
            vt = jnp.concatenate([vt_ref[0, tile_of(2 * i)], vt_ref[0, tile_of(2 * i + 1)]], axis=1)
            return _dot(jnp.concatenate([vt, ones], axis=0), p_ref[g])

        def logits(i):
            tile_max = None
            for half in range(2):
                step = 2 * i + half
                k = k_ref[0, pl.ds(pl.multiple_of(tile_of(step) * t, t), t), :]
                d = jnp.where(step > qi, n_bias, jnp.minimum(step, n_bias - 1))
                b0 = bias_ref[d, 0]
                b1 = bias_ref[d, 1]
                s = _dot(k, q4_ref[g]) + jnp.concatenate([b0, b0, b1, b1], axis=1)
                s_ref[g, half * t:(half + 1) * t] = s
                mx = jnp.max(s, axis=0, keepdims=True)
                tile_max = mx if tile_max is None else jnp.maximum(tile_max, mx)
            mt_ref[g] = tile_max

        def sweep(i):
            pv = values(i - 1)
            m_old = m_ref[g]
            m_new = jnp.maximum(m_old, mt_ref[g])
            alpha = jnp.exp2(m_old - m_new)
            m_ref[g] = m_new
            p_ref[g] = jnp.exp2(s_ref[g] - m_new).astype(BF16)
            acc_ref[g] = alpha * (acc_ref[g] + pv)
            logits(i + 1)

        def fill():
            init()
            logits(0)

        def drain(n_iter):
            acc = acc_ref[g] + values(n_iter - 1)
            pv = acc[0:LANES] / acc[LANES:LANES + 1]
            lam = lam_ref[...]
            outs = []
            for h in range(2):
                rows = slice(h * HEAD_DIM, (h + 1) * HEAD_DIM)
                diff = (pv[rows, (2 * h) * t:(2 * h + 1) * t]
                        - lam * pv[rows, (2 * h + 1) * t:(2 * h + 2) * t])
                ms = jnp.mean(diff * diff, axis=0, keepdims=True)
                outs.append(diff * lax.rsqrt(ms + EPS))
            out_t = jnp.concatenate(outs, axis=0) * gain_ref[...]
            o_ref[0, g * t:(g + 1) * t] = out_t.T.astype(BF16)

        return fill, sweep, drain

    stages = [pipeline(g) for g in range(QUERY_TILES)]
    for fill, _, _ in stages:
        fill()
    n_iter = (pl.program_id(2) * QUERY_TILES + QUERY_TILES + 1) // 2

    def body(i, c):
        for _, sweep, _ in stages:
            sweep(i)
        return c

    lax.fori_loop(0, n_iter, body, 0)
    for _, _, drain in stages:
        drain(n_iter)


def _diff_attention(qbt, kb, vbt, bias_bt, lam, post_gain):
    b, s, wb = kb.shape
    n_pairs = wb // LANES
    n_bias = bias_bt.shape[0] - 1
    t = ATT_TILE
    nt = s // t
    g = QUERY_TILES
    qspec = pl.BlockSpec((1, g, LANES, t), lambda i, p, j: (i, j, p, 0))
    kspec = pl.BlockSpec((1, s, LANES), lambda i, p, j: (i, 0, p))
    vspec = pl.BlockSpec((1, nt, LANES, t), lambda i, p, j: (i, 0, p, 0))
    bspec = pl.BlockSpec((n_bias + 1, 2, t, t), lambda i, p, j: (0, p, 0, 0))
    return pl.pallas_call(
        functools.partial(_diff_kernel, n_bias),
        out_shape=jax.ShapeDtypeStruct((b, s, wb), BF16),
        grid=(b, n_pairs, nt // g),
        in_specs=[qspec, kspec, vspec, bspec,
                  pl.BlockSpec((1, t), lambda i, p, j: (0, 0)),
                  pl.BlockSpec((LANES, t), lambda i, p, j: (0, 0))],
        out_specs=pl.BlockSpec((1, g * t, LANES), lambda i, p, j: (i, j, p)),
        scratch_shapes=[pltpu.VMEM((g, LANES, 4 * t), BF16),
                        pltpu.VMEM((g, 1, 4 * t), F32),
                        pltpu.VMEM((g, 1, 4 * t), F32),
                        pltpu.VMEM((g, LANES + ONES_ROWS, 4 * t), F32),
                        pltpu.VMEM((g, 2 * t, 4 * t), F32),
                        pltpu.VMEM((g, 2 * t, 4 * t), BF16)],
        compiler_params=_cparams("parallel", "parallel", "parallel"),
        name="diff_attention",
    )(qbt.reshape(b, nt, wb, t), kb, vbt.reshape(b, nt, wb, t), bias_bt, lam, post_gain)


def _stick_kernel(qt_ref, k_ref, vt_ref, tri_ref, o_ref, q2_ref, carry_ref, scale_ref, acc_ref,
                  z_ref, lw_ref, p_ref):
    t = ATT_TILE
    row = lax.broadcasted_iota(jnp.int32, (LANES, t), 0)

    def pipeline(g):
        qi = pl.program_id(2) * QUERY_TILES + g

        def tile_of(step):
            return jnp.clip(qi - step, 0, qi)

        def init():
            qt = qt_ref[0, g]
            zero = jnp.zeros_like(qt)
            q2_ref[g, :, 0:t] = jnp.where(row < HEAD_DIM, qt, zero)
            q2_ref[g, :, t:] = jnp.where(row < HEAD_DIM, zero, qt)
            carry_ref[g] = jnp.zeros(carry_ref.shape[1:], F32)
            scale_ref[g] = jnp.ones(scale_ref.shape[1:], F32)
            acc_ref[g] = jnp.zeros(acc_ref.shape[1:], F32)
            p_ref[g] = jnp.zeros(p_ref.shape[1:], BF16)

        def logits(i, diagonal=False):
            for half in range(2):
                k = k_ref[0, pl.ds(pl.multiple_of(tile_of(2 * i + half) * t, t), t), :]
                z = _dot(k, q2_ref[g])
                if diagonal and half == 0:
                    key = lax.broadcasted_iota(jnp.int32, (t, t), 0)
                    qry = lax.broadcasted_iota(jnp.int32, (t, t), 1)
                    strict = jnp.concatenate([key < qry, key < qry], axis=1)
                    z = jnp.where(strict, z, -SB_MASK * LOG2E)
                z_ref[g, half] = z

        def log_weights():
            for half in range(2):
                z = z_ref[g, half]
                neg_abs = pltpu.bitcast(pltpu.bitcast(z, jnp.uint32) | jnp.uint32(0x80000000), F32)
                sp = jnp.maximum(z, 0.0) + jnp.log(1.0 + jnp.exp2(neg_abs)) * LOG2E
                w = _dot(tri_ref[...], sp.astype(BF16))
                lw_ref[g, half, 0:t] = z + w[0:t]
                lw_ref[g, half, t:] = w[t:]

        def weights():
            carry = carry_ref[g]
            near_sum = lw_ref[g, 0, t:t + 1]
            p_ref[g, 0:t] = jnp.exp2(lw_ref[g, 0, 0:t]).astype(BF16)
            p_ref[g, t:] = jnp.exp2(lw_ref[g, 1, 0:t] + near_sum).astype(BF16)
            scale_ref[g] = jnp.exp2(carry)
            carry_ref[g] = carry + near_sum + lw_ref[g, 1, t:t + 1]

        def values(i):
            vts = []
            for half in range(2):
                step = 2 * i + half
                valid = jnp.logical_and(step >= 0, step <= qi)
                vt = vt_ref[0, tile_of(step)]
                vts.append(jnp.where(valid, vt, jnp.zeros_like(vt)))
            acc_ref[g] += _dot(jnp.concatenate(vts, axis=1), p_ref[g]) * scale_ref[g]

        def sweep(i):
            values(i - 1)
            weights()
            log_weights()
            logits(i + 2)

        def fill():
            init()
            logits(0, diagonal=True)
            log_weights()
            logits(1)

        def drain(n_iter):
            values(n_iter - 1)
            out_t = jnp.concatenate([acc_ref[g, 0:HEAD_DIM, 0:t], acc_ref[g, HEAD_DIM:, t:]], axis=0)
            o_ref[0, g * t:(g + 1) * t] = out_t.T.astype(BF16)

        return fill, sweep, drain

    stages = [pipeline(g) for g in range(QUERY_TILES)]
    for fill, _, _ in stages:
        fill()
    n_iter = (pl.program_id(2) * QUERY_TILES + QUERY_TILES + 1) // 2

    def body(i, c):
        for _, sweep, _ in stages:
            sweep(i)
        return c

    lax.fori_loop(0, n_iter, body, 0)
    for _, _, drain in stages:
        drain(n_iter)


def _stick_attention(qct, kc, vct):
    b, s, wc = kc.shape
    n_pairs = wc // LANES
    t = ATT_TILE
    nt = s // t
    g = QUERY_TILES
    idx = np.arange(t)
    tri = np.concatenate([idx[None, :] >= idx[:, None], np.ones((ONES_ROWS, t), bool)], axis=0)
    tri = -jnp.asarray(tri, dtype=BF16)
    return pl.pallas_call(
        _stick_kernel,
        out_shape=jax.ShapeDtypeStruct((b, s, wc), BF16),
        grid=(b, n_pairs, nt // g),
        in_specs=[pl.BlockSpec((1, g, LANES, t), lambda i, p, j: (i, j, p, 0)),
                  pl.BlockSpec((1, s, LANES), lambda i, p, j: (i, 0, p)),
                  pl.BlockSpec((1, nt, LANES, t), lambda i, p, j: (i, 0, p, 0)),
                  pl.BlockSpec((t + ONES_ROWS, t), lambda i, p, j: (0, 0))],
        out_specs=pl.BlockSpec((1, g * t, LANES), lambda i, p, j: (i, j, p)),
        scratch_shapes=[pltpu.VMEM((g, LANES, 2 * t), BF16),
                        pltpu.VMEM((g, 1, 2 * t), F32),
                        pltpu.VMEM((g, 1, 2 * t), F32),
                        pltpu.VMEM((g, LANES, 2 * t), F32),
                        pltpu.VMEM((g, 2, t, 2 * t), F32),
                        pltpu.VMEM((g, 2, t + ONES_ROWS, 2 * t), F32),
                        pltpu.VMEM((g, 2 * t, 2 * t), BF16)],
        compiler_params=_cparams("parallel", "parallel", "parallel"),
        name="stick_breaking_attention",
    )(qct.reshape(b, nt, wc, t), kc, vct.reshape(b, nt, wc, t), tri)


def _out_proj_kernel(widths, x_ref, a_ref, b_ref, c_ref, w_ref, o_ref):
    wa, wb, wc = widths
    y = _dot(a_ref[...], w_ref[0:wa])
    y += _dot(b_ref[...], w_ref[wa:wa + wb])
    y += _dot(c_ref[...], w_ref[wa + wb:wa + wb + wc])
    o_ref[...] = x_ref[...] + y


def _out_proj(x2, oa, ob, oc, w_out, widths):
    t, d = x2.shape
    row = lambda w: pl.BlockSpec((ROW_TILE, w), lambda i: (i, 0))
    return pl.pallas_call(
        functools.partial(_out_proj_kernel, widths),
        out_shape=jax.ShapeDtypeStruct((t, d), F32),
        grid=(t // ROW_TILE,),
        in_specs=[row(d), row(widths[0]), row(widths[1]), row(widths[2]),
                  pl.BlockSpec(w_out.shape, lambda i: (0, 0))],
        out_specs=row(d),
        compiler_params=_cparams("parallel"),
        name="out_proj_residual",
    )(x2, oa, ob, oc, w_out)


def _t5_bucket(dist):
    dist = jnp.maximum(dist, 0)
    max_exact = N_BUCKETS // 2
    d_f = jnp.maximum(dist, 1).astype(F32)
    large = max_exact + (jnp.log(d_f / max_exact) / math.log(MAX_DISTANCE / max_exact)
                         * (N_BUCKETS - max_exact)).astype(jnp.int32)
    large = jnp.minimum(large, N_BUCKETS - 1)
    return jnp.where(dist < max_exact, dist, large)


def _bias_of_distance(bias, dist):
    bucket = _t5_bucket(dist)[None]
    out = jnp.zeros((bias.shape[1],) + dist.shape, F32)
    for b in range(N_BUCKETS):
        out = jnp.where(bucket == b, bias[b].reshape((-1,) + (1,) * dist.ndim), out)
    return out


def _dilated_bias_table(bias_a):
    n_heads = bias_a.shape[1]
    i = jnp.arange(Q_BLOCK, dtype=jnp.int32)[:, None]
    j = jnp.arange(2 * Q_BLOCK, dtype=jnp.int32)[None, :]
    tables = []
    for window, dil in DILATED_BRANCHES:
        n = window // dil
        off = i + n - j
        band = (off >= 0) & (off <= n)
        bias = _bias_of_distance(bias_a, off * dil)
        variants = [jnp.where(valid[None], bias, NEG_INF) for valid in (band, band & (j >= n))]
        tables.append(jnp.stack(variants))
    table = jnp.stack(tables)
    return table.reshape(len(DILATED_BRANCHES), 2, n_heads // 2, 2 * Q_BLOCK, 2 * Q_BLOCK)


def _diff_bias_table(bias_b, seq):
    t = ATT_TILE
    n_bias = min(seq // t, MAX_DISTANCE // t + 2)
    key = jnp.arange(t, dtype=jnp.int32)[None, :, None]
    qry = jnp.arange(t, dtype=jnp.int32)[None, None, :]
    dist = jnp.arange(n_bias + 1, dtype=jnp.int32)[:, None, None] * t + qry - key
    valid = (dist >= 0) & (jnp.arange(n_bias + 1)[:, None, None] < n_bias)
    tiles = jnp.where(valid[None], _bias_of_distance(bias_b, dist) * LOG2E, NEG_INF)
    return jnp.swapaxes(tiles, 0, 1)


def kernel(x, rel_bias, ffn1_norm, ffn1_w_gate, ffn1_w_up, ffn1_w_down, mix_norm, w_in,
           q_norm_a, k_norm_a, q_norm_b, k_norm_b, lambda_q1, lambda_k1, lambda_q2, lambda_k2,
           diff_subln, w_out, ffn2_norm, ffn2_w_gate, ffn2_w_up, ffn2_w_down):
    b, s, d = x.shape
    depth = w_in.shape[0]
    n_heads = d // HEAD_DIM
    wa = (n_heads // 2) * HEAD_DIM
    wb = (n_heads // 4) * HEAD_DIM
    wc = d - wa - wb
    widths = (wa, wb, wc)
    assert w_in.shape[2] == 3 * d and s % A_CHUNK == 0 and (b * s) % ROW_TILE == 0

    rb = rel_bias.astype(F32)
    bias_a = _dilated_bias_table(rb[:, :wa // HEAD_DIM])
    bias_bt = _diff_bias_table(rb[:, wa // HEAD_DIM:], s)

    x2 = x.reshape(b * s, d)
    for layer in range(depth):
        x2 = _ffn(x2, ffn1_norm[layer], ffn1_w_gate[layer].astype(BF16),
                  ffn1_w_up[layer].astype(BF16), ffn1_w_down[layer].astype(BF16))

        qa, ka, va, kb, kc, qbt, vbt, qct, vct = _proj(
            x2, mix_norm[layer], w_in[layer], q_norm_a[layer], k_norm_a[layer],
            q_norm_b[layer], k_norm_b[layer], widths)
        seq3 = lambda t: t.reshape(b, s, t.shape[-1])

        out_a = _dilated_attention(seq3(qa), seq3(ka), seq3(va), bias_a)

        lam_init = 0.8 - 0.6 * math.exp(-0.3 * layer)
        lam = (jnp.exp(jnp.sum(lambda_q1[layer].astype(F32) * lambda_k1[layer].astype(F32)))
               - jnp.exp(jnp.sum(lambda_q2[layer].astype(F32) * lambda_k2[layer].astype(F32)))
               + lam_init)
        lam_row = jnp.full((1, ATT_TILE), lam, F32)
        post_gain = jnp.broadcast_to(
            (jnp.tile(diff_subln[layer].astype(F32), LANES // HEAD_DIM) * (1.0 - lam_init))[:, None],
            (LANES, ATT_TILE))
        out_b = _diff_attention(qbt, seq3(kb), vbt, bias_bt, lam_row, post_gain)

        out_c = _stick_attention(qct, seq3(kc), vct)

        x2 = _out_proj(x2, out_a.reshape(b * s, wa), out_b.reshape(b * s, wb),
                       out_c.reshape(b * s, wc), w_out[layer].astype(BF16), widths)

        x2 = _ffn(x2, ffn2_norm[layer], ffn2_w_gate[layer].astype(BF16),
                  ffn2_w_up[layer].astype(BF16), ffn2_w_down[layer].astype(BF16))
    return x2.reshape(b, s, d)
```

```python
import functools
import math

import jax
import jax.numpy as jnp
import numpy as np
from jax import lax
from jax.experimental import pallas as pl
from jax.experimental.pallas import tpu as pltpu

F32 = jnp.float32
BF16 = jnp.bfloat16

HEAD_DIM = 64
DIFF_HALF = HEAD_DIM // 2
N_BUCKETS = 32
MAX_DISTANCE = 2048
DILATED_BRANCHES = ((128, 1), (512, 4), (2048, 16))
Q_BLOCK = 128
EPS = 1e-6
NEG_INF = -1e30
SB_MASK = 1e4
LOG2E = math.log2(math.e)

LANES = 128
A_CHUNK = 2048
A_BLOCKS_PER_ITER = 4
ATT_TILE = 256
ROW_TILE = 512
ONES_ROWS = 16
QUERY_TILES = 2
VMEM_LIMIT = 56 * 1024 * 1024


def _cparams(*sem):
    return pltpu.CompilerParams(dimension_semantics=sem, vmem_limit_bytes=VMEM_LIMIT)


def _rms(x, gain_row):
    ms = jnp.mean(x * x, axis=-1, keepdims=True)
    return x * lax.rsqrt(ms + EPS) * gain_row


def _dot(a, b):
    return jnp.dot(a, b, preferred_element_type=F32)


def _dot_nt(a, b):
    return lax.dot_general(a, b, (((1,), (1,)), ((), ())), preferred_element_type=F32)


def _ffn_kernel(x_ref, g_ref, wg_ref, wu_ref, wd_ref, o_ref):
    x = x_ref[...]
    h = _rms(x, g_ref[...]).astype(BF16)
    gate = _dot(h, wg_ref[...])
    up = _dot(h, wu_ref[...])
    act = (gate * jax.nn.sigmoid(gate) * up).astype(BF16)
    o_ref[...] = x + 0.5 * _dot(act, wd_ref[...])


def _ffn(x2, gain, wg, wu, wd):
    t, d = x2.shape
    dff = wg.shape[1]
    row = pl.BlockSpec((ROW_TILE, d), lambda i: (i, 0))
    full = lambda shape: pl.BlockSpec(shape, lambda i: (0, 0))
    return pl.pallas_call(
        _ffn_kernel,
        out_shape=jax.ShapeDtypeStruct((t, d), F32),
        grid=(t // ROW_TILE,),
        in_specs=[row, full((1, d)), full((d, dff)), full((d, dff)), full((dff, d))],
        out_specs=row,
        compiler_params=_cparams("parallel"),
        name="ffn_half_step",
    )(x2, gain.reshape(1, d), wg, wu, wd)


def _proj_kernel(widths, x_ref, g_ref, wn_ref, wt_ref, gqa_ref, gka_ref, gkb_ref, gqb_ref,
                 g64_ref, g32_ref, qa_ref, ka_ref, va_ref, kb_ref, kc_ref,
                 qbt_ref, vbt_ref, qct_ref, vct_ref):
    wa, wb, wc = widths
    h = _rms(x_ref[...], g_ref[...]).astype(BF16)
    sub = 256

    def group_norm(t, ones_ref, gain_ref, group, scale):
        ss = _dot((t * t).astype(BF16), ones_ref[...])
        return t * lax.rsqrt(ss * (1.0 / group) + EPS) * (gain_ref[...] * scale)

    def group_norm_t(t, ones_ref, gain_ref, group, scale):
        ss = _dot(ones_ref[...], (t * t).astype(BF16))
        return t * lax.rsqrt(ss * (1.0 / group) + EPS) * (gain_ref[...] * scale)

    col = 0
    for ref, width, post in (
            (qa_ref, wa, lambda t: group_norm(t, g64_ref, gqa_ref, HEAD_DIM, HEAD_DIM ** -0.5)),
            (ka_ref, wa, lambda t: group_norm(t, g64_ref, gka_ref, HEAD_DIM, 1.0)),
            (va_ref, wa, lambda t: t),
            (kb_ref, wb, lambda t: group_norm(t, g32_ref, gkb_ref, DIFF_HALF, 1.0)),
            (kc_ref, wc, lambda t: t)):
        for c in range(0, width, sub):
            t = _dot(h, wn_ref[:, col + c:col + c + sub])
            ref[:, c:c + sub] = post(t).astype(ref.dtype)
        col += width

    row = 0
    for ref, width, post in (
            (qbt_ref, wb, lambda t: group_norm_t(t, g32_ref, gqb_ref, DIFF_HALF,
                                                 DIFF_HALF ** -0.5 * LOG2E)),
            (vbt_ref, wb, lambda t: t),
            (qct_ref, wc, lambda t: t * (HEAD_DIM ** -0.5 * LOG2E)),
            (vct_ref, wc, lambda t: t)):
        for r in range(0, width, sub):
            t = post(_dot_nt(wt_ref[row + r:row + r + sub, :], h)).astype(BF16)
            for j in range(ROW_TILE // ATT_TILE):
                ref[j, r:r + sub, :] = t[:, j * ATT_TILE:(j + 1) * ATT_TILE]
        row += width


def _block_diag_ones(group):
    idx = np.arange(256) // group
    return jnp.asarray(idx[:, None] == idx[None, :], dtype=BF16)


def _proj(x2, gain, w_in, gqa, gka, gqb, gkb, widths):
    t, d = x2.shape
    wa, wb, wc = widths
    assert wa % 256 == 0 and wb % 256 == 0 and wc % 256 == 0
    w = w_in.astype(BF16)
    o = np.cumsum([0, wa, wa, wa, wb, wb, wb, wc, wc, wc])
    sec = lambda i: w[:, o[i]:o[i + 1]]
    w_nat = jnp.concatenate([sec(0), sec(1), sec(2), sec(4), sec(7)], axis=1)
    w_tr = jnp.concatenate([sec(3), sec(5), sec(6), sec(8)], axis=1).T
    row = lambda wd: pl.BlockSpec((ROW_TILE, wd), lambda i: (i, 0))
    full = lambda shape: pl.BlockSpec(shape, lambda i: (0,) * len(shape))
    slab = lambda wd: pl.BlockSpec((ROW_TILE // ATT_TILE, wd, ATT_TILE), lambda i: (i, 0, 0))
    tile256 = lambda g: jnp.tile(g.astype(F32), 256 // g.shape[0])
    out_shape = ([jax.ShapeDtypeStruct((t, wa), F32)] * 3
                 + [jax.ShapeDtypeStruct((t, wb), BF16), jax.ShapeDtypeStruct((t, wc), BF16)]
                 + [jax.ShapeDtypeStruct((t // ATT_TILE, wd, ATT_TILE), BF16)
                    for wd in (wb, wb, wc, wc)])
    return pl.pallas_call(
        functools.partial(_proj_kernel, widths),
        out_shape=out_shape,
        grid=(t // ROW_TILE,),
        in_specs=[row(d), full((1, d)), full(w_nat.shape), full(w_tr.shape)]
                 + [full((1, 256))] * 3 + [full((256, 1))] + [full((256, 256))] * 2,
        out_specs=[row(wa)] * 3 + [row(wb), row(wc)] + [slab(wb), slab(wb), slab(wc), slab(wc)],
        compiler_params=_cparams("parallel"),
        name="norm_in_proj",
    )(x2, gain.reshape(1, d), w_nat, w_tr, tile256(gqa).reshape(1, 256), tile256(gka).reshape(1, 256),
      tile256(gkb).reshape(1, 256), tile256(gqb).reshape(256, 1),
      _block_diag_ones(HEAD_DIM), _block_diag_ones(DIFF_HALF))


def _dilated_kernel(q_ref, kp_ref, kc_ref, vp_ref, vc_ref, bias_ref, o_ref,
                    kk_ref, vv_ref, m_ref, l_ref, acc_ref):
    chunk = pl.program_id(1)
    kk_ref[0:A_CHUNK] = kp_ref[0]
    kk_ref[A_CHUNK:] = kc_ref[0]
    vv_ref[0:A_CHUNK] = vp_ref[0]
    vv_ref[A_CHUNK:] = vc_ref[0]

    lane = lax.broadcasted_iota(jnp.int32, (Q_BLOCK, LANES), 1)
    head0 = lane < HEAD_DIM

    order = sorted(range(len(DILATED_BRANCHES)), key=lambda b: -DILATED_BRANCHES[b][1])
    for bi in order:
        window, dil = DILATED_BRANCHES[bi]
        fresh = bi == order[0]
        nblk = A_CHUNK // (Q_BLOCK * dil)
        shift = int(math.log2(nblk))
        n_iter = dil * nblk
        ds = (lambda start, size, dil=dil:
              pl.ds(start, size, stride=dil) if dil > 1 else pl.ds(start, size))

        def load(idx, bi=bi, dil=dil, nblk=nblk, shift=shift, ds=ds, fresh=fresh):
            rho = idx >> shift
            t = idx & (nblk - 1)
            q_start = rho + t * (Q_BLOCK * dil)
            rows = ds(q_start, Q_BLOCK)
            krows = ds(A_CHUNK + q_start - Q_BLOCK * dil, 2 * Q_BLOCK)
            first = jnp.logical_and(chunk == 0, t == 0).astype(jnp.int32)
            blk = dict(rows=rows, q=q_ref[0, rows, :], k=kk_ref[krows, :], v=vv_ref[krows, :],
                       bias=bias_ref[bi, first, 0])
            if not fresh:
                blk.update(m=jnp.concatenate([m_ref[0, rows, :], m_ref[1, rows, :]], axis=0),
                           l=jnp.concatenate([l_ref[0, rows, :], l_ref[1, rows, :]], axis=0),
                           acc=acc_ref[rows, :])
            return blk

        def compute(b):
            q = b["q"]
            q2 = jnp.concatenate([jnp.where(head0, q, 0.0), jnp.where(head0, 0.0, q)],
                                 axis=0).astype(BF16)
            s = _dot_nt(q2, b["k"].astype(BF16)) + b["bias"]
            row_max = jnp.max(s, axis=-1, keepdims=True)
            if "m" not in b:
                m_new = jnp.broadcast_to(row_max, (2 * Q_BLOCK, LANES))
                p = jnp.exp(s - row_max)
                l_new = jnp.broadcast_to(jnp.sum(p, axis=-1, keepdims=True), (2 * Q_BLOCK, LANES))
                pv = _dot(p.astype(BF16), b["v"].astype(BF16))
                return m_new, l_new, jnp.where(head0, pv[:Q_BLOCK], pv[Q_BLOCK:])
            m_new = jnp.maximum(b["m"], row_max)
            alpha = jnp.exp(b["m"] - m_new)
            p = jnp.exp(s - jnp.concatenate([m_new, m_new], axis=1))
            l_new = alpha * b["l"] + jnp.sum(p, axis=-1, keepdims=True)
            pv = _dot(p.astype(BF16), b["v"].astype(BF16))
            acc_new = jnp.where(head0, alpha[:Q_BLOCK] * b["acc"] + pv[:Q_BLOCK],
                                alpha[Q_BLOCK:] * b["acc"] + pv[Q_BLOCK:])
            return m_new, l_new, acc_new

        def store(b, res):
            m_new, l_new, acc_new = res
            rows = b["rows"]
            acc_ref[rows, :] = acc_new
            m_ref[0, rows, :] = m_new[:Q_BLOCK]
            m_ref[1, rows, :] = m_new[Q_BLOCK:]
            l_ref[0, rows, :] = l_new[:Q_BLOCK]
            l_ref[1, rows, :] = l_new[Q_BLOCK:]

        def some_blocks(i, carry, load=load, compute=compute, store=store,
                        part=n_iter // A_BLOCKS_PER_ITER):
            blocks = [load(i + c * part) for c in range(A_BLOCKS_PER_ITER)]
            results = [compute(b) for b in blocks]
            for b, res in zip(blocks, results):
                store(b, res)
            return carry

        lax.fori_loop(0, n_iter // A_BLOCKS_PER_ITER, some_blocks, 0)

    lane_c = lax.broadcasted_iota(jnp.int32, (A_CHUNK, LANES), 1)
    denom = jnp.where(lane_c < HEAD_DIM, l_ref[0], l_ref[1])
    o_ref[0] = (acc_ref[...] / denom).astype(BF16)


def _dilated_attention(qa, ka, va, bias_a):
    b, s, wa = qa.shape
    n_pairs = wa // LANES
    cur = pl.BlockSpec((1, A_CHUNK, LANES), lambda i, c, p: (i, c, p))
    prev = pl.BlockSpec((1, A_CHUNK, LANES), lambda i, c, p: (i, jnp.maximum(c - 1, 0), p))
    nb = len(DILATED_BRANCHES)
    bias = pl.BlockSpec((nb, 2, 1, 2 * Q_BLOCK, 2 * Q_BLOCK), lambda i, c, p: (0, 0, p, 0, 0))
    return pl.pallas_call(
        _dilated_kernel,
        out_shape=jax.ShapeDtypeStruct((b, s, wa), BF16),
        grid=(b, s // A_CHUNK, n_pairs),
        in_specs=[cur, prev, cur, prev, cur, bias],
        out_specs=cur,
        scratch_shapes=[pltpu.VMEM((2 * A_CHUNK, LANES), F32),
                        pltpu.VMEM((2 * A_CHUNK, LANES), F32),
                        pltpu.VMEM((2, A_CHUNK, LANES), F32),
                        pltpu.VMEM((2, A_CHUNK, LANES), F32),
                        pltpu.VMEM((A_CHUNK, LANES), F32)],
        compiler_params=_cparams("parallel", "parallel", "parallel"),
        name="dilated_attention",
    )(qa, ka, ka, va, va, bias_a)


def _diff_kernel(n_bias, qt_ref, k_ref, vt_ref, bias_ref, lam_ref, gain_ref, o_ref,
                 q4_ref, m_ref, mt_ref, acc_ref, s_ref, p_ref):
    t = ATT_TILE
    row = lax.broadcasted_iota(jnp.int32, (LANES, t), 0)
    ones = jnp.ones((ONES_ROWS, 2 * t), BF16)

    def pipeline(g):
        qi = pl.program_id(2) * QUERY_TILES + g

        def tile_of(step):
            return jnp.clip(qi - step, 0, qi)

        def init():
            qt = qt_ref[0, g]
            for c in range(4):
                sel = jnp.logical_and(row >= c * DIFF_HALF, row < (c + 1) * DIFF_HALF)
                q4_ref[g, :, c * t:(c + 1) * t] = jnp.where(sel, qt, jnp.zeros_like(qt))
            m_ref[g] = jnp.full(m_ref.shape[1:], NEG_INF, F32)
            acc_ref[g] = jnp.zeros(acc_ref.shape[1:], F32)
            p_ref[g] = jnp.zeros(p_ref.shape[1:], BF16)

        def values(i):
            vt = jnp.concatenate([vt_ref[0, tile_of(2 * i)], vt_ref[0, tile_of(2 * i + 1)]], axis=1)
            return _dot(jnp.concatenate([vt, ones], axis=0), p_ref[g])

        def logits(i):
            tile_max = None
            for half in range(2):
                step = 2 * i + half
                k = k_ref[0, pl.ds(pl.multiple_of(tile_of(step) * t, t), t), :]
                d = jnp.where(step > qi, n_bias, jnp.minimum(step, n_bias - 1))
                b0 = bias_ref[d, 0]
                b1 = bias_ref[d, 1]
                s = _dot(k, q4_ref[g]) + jnp.concatenate([b0, b0, b1, b1], axis=1)
                s_ref[g, half * t:(half + 1) * t] = s
                mx = jnp.max(s, axis=0, keepdims=True)
                tile_max = mx if tile_max is None else jnp.maximum(tile_max, mx)
            mt_ref[g] = tile_max

        def sweep(i):
            pv = values(i - 1)
            m_old = m_ref[g]
            m_new = jnp.maximum(m_old, mt_ref[g])
            alpha = jnp.exp2(m_old - m_new)
            m_ref[g] = m_new
            p_ref[g] = jnp.exp2(s_ref[g] - m_new).astype(BF16)
            acc_ref[g] = alpha * (acc_ref[g] + pv)
            logits(i + 1)

        def fill():
            init()
            logits(0)

        def drain(n_iter):
            acc = acc_ref[g] + values(n_iter - 1)
            pv = acc[0:LANES] / acc[LANES:LANES + 1]
            lam = lam_ref[...]
            outs = []
            for h in range(2):
                rows = slice(h * HEAD_DIM, (h + 1) * HEAD_DIM)
                diff = (pv[rows, (2 * h) * t:(2 * h + 1) * t]
                        - lam * pv[rows, (2 * h + 1) * t:(2 * h + 2) * t])
                ms = jnp.mean(diff * diff, axis=0, keepdims=True)
                outs.append(diff * lax.rsqrt(ms + EPS))
            out_t = jnp.concatenate(outs, axis=0) * gain_ref[...]
            o_ref[0, g * t:(g + 1) * t] = out_t.T.astype(BF16)

        return fill, sweep, drain

    stages = [pipeline(g) for g in range(QUERY_TILES)]
    for fill, _, _ in stages:
        fill()
    n_iter = (pl.program_id(2) * QUERY_TILES + QUERY_TILES + 1) // 2

    def body(i, c):
        for _, sweep, _ in stages:
            sweep(i)
        return c

    lax.fori_loop(0, n_iter, body, 0)
    for _, _, drain in stages:
        drain(n_iter)


def _diff_attention(qbt, kb, vbt, bias_bt, lam, post_gain):
    b, s, wb = kb.shape
    n_pairs = wb // LANES
    n_bias = bias_bt.shape[0] - 1
    t = ATT_TILE
    nt = s // t
    g = QUERY_TILES
    qspec = pl.BlockSpec((1, g, LANES, t), lambda i, p, j: (i, j, p, 0))
    kspec = pl.BlockSpec((1, s, LANES), lambda i, p, j: (i, 0, p))
    vspec = pl.BlockSpec((1, nt, LANES, t), lambda i, p, j: (i, 0, p, 0))
    bspec = pl.BlockSpec((n_bias + 1, 2, t, t), lambda i, p, j: (0, p, 0, 0))
    return pl.pallas_call(
        functools.partial(_diff_kernel, n_bias),
        out_shape=jax.ShapeDtypeStruct((b, s, wb), BF16),
        grid=(b, n_pairs, nt // g),
        in_specs=[qspec, kspec, vspec, bspec,
                  pl.BlockSpec((1, t), lambda i, p, j: (0, 0)),
                  pl.BlockSpec((LANES, t), lambda i, p, j: (0, 0))],
        out_specs=pl.BlockSpec((1, g * t, LANES), lambda i, p, j: (i, j, p)),
        scratch_shapes=[pltpu.VMEM((g, LANES, 4 * t), BF16),
                        pltpu.VMEM((g, 1, 4 * t), F32),
                        pltpu.VMEM((g, 1, 4 * t), F32),
                        pltpu.VMEM((g, LANES + ONES_ROWS, 4 * t), F32),
                        pltpu.VMEM((g, 2 * t, 4 * t), F32),
                        pltpu.VMEM((g, 2 * t, 4 * t), BF16)],
        compiler_params=_cparams("parallel", "parallel", "parallel"),
        name="diff_attention",
    )(qbt.reshape(b, nt, wb, t), kb, vbt.reshape(b, nt, wb, t), bias_bt, lam, post_gain)


def _stick_kernel(qt_ref, k_ref, vt_ref, tri_ref, o_ref, q2_ref, carry_ref, scale_ref, acc_ref,
                  z_ref, lw_ref, p_ref):
    t = ATT_TILE
    row = lax.broadcasted_iota(jnp.int32, (LANES, t), 0)

    def pipeline(g):
        qi = pl.program_id(2) * QUERY_TILES + g

        def tile_of(step):
            return jnp.clip(qi - step, 0, qi)

        def init():
            qt = qt_ref[0, g]
            zero = jnp.zeros_like(qt)
            q2_ref[g, :, 0:t] = jnp.where(row < HEAD_DIM, qt, zero)
            q2_ref[g, :, t:] = jnp.where(row < HEAD_DIM, zero, qt)
            carry_ref[g] = jnp.zeros(carry_ref.shape[1:], F32)
            scale_ref[g] = jnp.ones(scale_ref.shape[1:], F32)
            acc_ref[g] = jnp.zeros(acc_ref.shape[1:], F32)
            p_ref[g] = jnp.zeros(p_ref.shape[1:], BF16)

        def logits(i, diagonal=False):
            for half in range(2):
                k = k_ref[0, pl.ds(pl.multiple_of(tile_of(2 * i + half) * t, t), t), :]
                z = _dot(k, q2_ref[g])
                if diagonal and half == 0:
                    key = lax.broadcasted_iota(jnp.int32, (t, t), 0)
                    qry = lax.broadcasted_iota(jnp.int32, (t, t), 1)
                    strict = jnp.concatenate([key < qry, key < qry], axis=1)
                    z = jnp.where(strict, z, -SB_MASK * LOG2E)
                z_ref[g, half] = z

        def log_weights():
            for half in range(2):
                z = z_ref[g, half]
                neg_abs = pltpu.bitcast(pltpu.bitcast(z, jnp.uint32) | jnp.uint32(0x80000000), F32)
                sp = jnp.maximum(z, 0.0) + jnp.log(1.0 + jnp.exp2(neg_abs)) * LOG2E
                w = _dot(tri_ref[...], sp.astype(BF16))
                lw_ref[g, half, 0:t] = z + w[0:t]
                lw_ref[g, half, t:] = w[t:]

        def weights():
            carry = carry_ref[g]
            near_sum = lw_ref[g, 0, t:t + 1]
            p_ref[g, 0:t] = jnp.exp2(lw_ref[g, 0, 0:t]).astype(BF16)
            p_ref[g, t:] = jnp.exp2(lw_ref[g, 1, 0:t] + near_sum).astype(BF16)
            scale_ref[g] = jnp.exp2(carry)
            carry_ref[g] = carry + near_sum + lw_ref[g, 1, t:t + 1]

        def values(i):
            vts = []
            for half in range(2):
                step = 2 * i + half
                valid = jnp.logical_and(step >= 0, step <= qi)
                vt = vt_ref[0, tile_of(step)]
                vts.append(jnp.where(valid, vt, jnp.zeros_like(vt)))
            acc_ref[g] += _dot(jnp.concatenate(vts, axis=1), p_ref[g]) * scale_ref[g]

        def sweep(i):
            values(i - 1)
            weights()
            log_weights()
            logits(i + 2)

        def fill():
            init()
            logits(0, diagonal=True)
            log_weights()
            logits(1)

        def drain(n_iter):
            values(n_iter - 1)
            out_t = jnp.concatenate([acc_ref[g, 0:HEAD_DIM, 0:t], acc_ref[g, HEAD_DIM:, t:]], axis=0)
            o_ref[0, g * t:(g + 1) * t] = out_t.T.astype(BF16)

        return fill, sweep, drain

    stages = [pipeline(g) for g in range(QUERY_TILES)]
    for fill, _, _ in stages:
        fill()
    n_iter = (pl.program_id(2) * QUERY_TILES + QUERY_TILES + 1) // 2

    def body(i, c):
        for _, sweep, _ in stages:
            sweep(i)
        return c

    lax.fori_loop(0, n_iter, body, 0)
    for _, _, drain in stages:
        drain(n_iter)


def _stick_attention(qct, kc, vct):
    b, s, wc = kc.shape
    n_pairs = wc // LANES
    t = ATT_TILE
    nt = s // t
    g = QUERY_TILES
    idx = np.arange(t)
    tri = np.concatenate([idx[None, :] >= idx[:, None], np.ones((ONES_ROWS, t), bool)], axis=0)
    tri = -jnp.asarray(tri, dtype=BF16)
    return pl.pallas_call(
        _stick_kernel,
        out_shape=jax.ShapeDtypeStruct((b, s, wc), BF16),
        grid=(b, n_pairs, nt // g),
        in_specs=[pl.BlockSpec((1, g, LANES, t), lambda i, p, j: (i, j, p, 0)),
                  pl.BlockSpec((1, s, LANES), lambda i, p, j: (i, 0, p)),
                  pl.BlockSpec((1, nt, LANES, t), lambda i, p, j: (i, 0, p, 0)),
                  pl.BlockSpec((t + ONES_ROWS, t), lambda i, p, j: (0, 0))],
        out_specs=pl.BlockSpec((1, g * t, LANES), lambda i, p, j: (i, j, p)),
        scratch_shapes=[pltpu.VMEM((g, LANES, 2 * t), BF16),
                        pltpu.VMEM((g, 1, 2 * t), F32),
                        pltpu.VMEM((g, 1, 2 * t), F32),
                        pltpu.VMEM((g, LANES, 2 * t), F32),
                        pltpu.VMEM((g, 2, t, 2 * t), F32),
                        pltpu.VMEM((g, 2, t + ONES_ROWS, 2 * t), F32),
                        pltpu.VMEM((g, 2 * t, 2 * t), BF16)],
        compiler_params=_cparams("parallel", "parallel", "parallel"),
        name="stick_breaking_attention",
    )(qct.reshape(b, nt, wc, t), kc, vct.reshape(b, nt, wc, t), tri)


def _out_proj_kernel(widths, x_ref, a_ref, b_ref, c_ref, w_ref, o_ref):
    wa, wb, wc = widths
    y = _dot(a_ref[...], w_ref[0:wa])
    y += _dot(b_ref[...], w_ref[wa:wa + wb])
    y += _dot(c_ref[...], w_ref[wa + wb:wa + wb + wc])
    o_ref[...] = x_ref[...] + y


def _out_proj(x2, oa, ob, oc, w_out, widths):
    t, d = x2.shape
    row = lambda w: pl.BlockSpec((ROW_TILE, w), lambda i: (i, 0))
    return pl.pallas_call(
        functools.partial(_out_proj_kernel, widths),
        out_shape=jax.ShapeDtypeStruct((t, d), F32),
        grid=(t // ROW_TILE,),
        in_specs=[row(d), row(widths[0]), row(widths[1]), row(widths[2]),
                  pl.BlockSpec(w_out.shape, lambda i: (0, 0))],
        out_specs=row(d),
        compiler_params=_cparams("parallel"),
        name="out_proj_residual",
    )(x2, oa, ob, oc, w_out)


def _t5_bucket(dist):
    dist = jnp.maximum(dist, 0)
    max_exact = N_BUCKETS // 2
    d_f = jnp.maximum(dist, 1).astype(F32)
    large = max_exact + (jnp.log(d_f / max_exact) / math.log(MAX_DISTANCE / max_exact)
                         * (N_BUCKETS - max_exact)).astype(jnp.int32)
    large = jnp.minimum(large, N_BUCKETS - 1)
    return jnp.where(dist < max_exact, dist, large)


def _bias_of_distance(bias, dist):
    bucket = _t5_bucket(dist)[None]
    out = jnp.zeros((bias.shape[1],) + dist.shape, F32)
    for b in range(N_BUCKETS):
        out = jnp.where(bucket == b, bias[b].reshape((-1,) + (1,) * dist.ndim), out)
    return out


def _dilated_bias_table(bias_a):
    n_heads = bias_a.shape[1]
    i = jnp.arange(Q_BLOCK, dtype=jnp.int32)[:, None]
    j = jnp.arange(2 * Q_BLOCK, dtype=jnp.int32)[None, :]
    tables = []
    for window, dil in DILATED_BRANCHES:
        n = window // dil
        off = i + n - j
        band = (off >= 0) & (off <= n)
        bias = _bias_of_distance(bias_a, off * dil)
        variants = [jnp.where(valid[None], bias, NEG_INF) for valid in (band, band & (j >= n))]
        tables.append(jnp.stack(variants))
    table = jnp.stack(tables)
    return table.reshape(len(DILATED_BRANCHES), 2, n_heads // 2, 2 * Q_BLOCK, 2 * Q_BLOCK)


def _diff_bias_table(bias_b, seq):
    t = ATT_TILE
    n_bias = min(seq // t, MAX_DISTANCE // t + 2)
    key = jnp.arange(t, dtype=jnp.int32)[None, :, None]
    qry = jnp.arange(t, dtype=jnp.int32)[None, None, :]
    dist = jnp.arange(n_bias + 1, dtype=jnp.int32)[:, None, None] * t + qry - key
    valid = (dist >= 0) & (jnp.arange(n_bias + 1)[:, None, None] < n_bias)
    tiles = jnp.where(valid[None], _bias_of_distance(bias_b, dist) * LOG2E, NEG_INF)
    return jnp.swapaxes(tiles, 0, 1)


def kernel(x, rel_bias, ffn1_norm, ffn1_w_gate, ffn1_w_up, ffn1_w_down, mix_norm, w_in,
           q_norm_a, k_norm_a, q_norm_b, k_norm_b, lambda_q1, lambda_k1, lambda_q2, lambda_k2,
           diff_subln, w_out, ffn2_norm, ffn2_w_gate, ffn2_w_up, ffn2_w_down):
    b, s, d = x.shape
    depth = w_in.shape[0]
    n_heads = d // HEAD_DIM
    wa = (n_heads // 2) * HEAD_DIM
    wb = (n_heads // 4) * HEAD_DIM
    wc = d - wa - wb
    widths = (wa, wb, wc)
    assert w_in.shape[2] == 3 * d and s % A_CHUNK == 0 and (b * s) % ROW_TILE == 0

    rb = rel_bias.astype(F32)
    bias_a = _dilated_bias_table(rb[:, :wa // HEAD_DIM])
    bias_bt = _diff_bias_table(rb[:, wa // HEAD_DIM:], s)

    x2 = x.reshape(b * s, d)
    for layer in range(depth):
        x2 = _ffn(x2, ffn1_norm[layer], ffn1_w_gate[layer].astype(BF16),
                  ffn1_w_up[layer].astype(BF16), ffn1_w_down[layer].astype(BF16))

        qa, ka, va, kb, kc, qbt, vbt, qct, vct = _proj(
            x2, mix_norm[layer], w_in[layer], q_norm_a[layer], k_norm_a[layer],
            q_norm_b[layer], k_norm_b[layer], widths)
        seq3 = lambda t: t.reshape(b, s, t.shape[-1])

        out_a = _dilated_attention(seq3(qa), seq3(ka), seq3(va), bias_a)

        lam_init = 0.8 - 0.6 * math.exp(-0.3 * layer)
        lam = (jnp.exp(jnp.sum(lambda_q1[layer].astype(F32) * lambda_k1[layer].astype(F32)))
               - jnp.exp(jnp.sum(lambda_q2[layer].astype(F32) * lambda_k2[layer].astype(F32)))
               + lam_init)
        lam_row = jnp.full((1, ATT_TILE), lam, F32)
        post_gain = jnp.broadcast_to(
            (jnp.tile(diff_subln[layer].astype(F32), LANES // HEAD_DIM) * (1.0 - lam_init))[:, None],
            (LANES, ATT_TILE))
        out_b = _diff_attention(qbt, seq3(kb), vbt, bias_bt, lam_row, post_gain)

        out_c = _stick_attention(qct, seq3(kc), vct)

        x2 = _out_proj(x2, out_a.reshape(b * s, wa), out_b.reshape(b * s, wb),
                       out_c.reshape(b * s, wc), w_out[layer].astype(BF16), widths)

        x2 = _ffn(x2, ffn2_norm[layer], ffn2_w_gate[layer].astype(BF16),
                  ffn2_w_up[layer].astype(BF16), ffn2_w_down[layer].astype(BF16))
    return x2.reshape(b, s, d)
```

```python
import functools
import math

import jax
import jax.numpy as jnp
import numpy as np
from jax import lax
from jax.experimental import pallas as pl
from jax.experimental.pallas import tpu as pltpu

F32 = jnp.float32
BF16 = jnp.bfloat16

HEAD_DIM = 64
DIFF_HALF = HEAD_DIM // 2
N_BUCKETS = 32
MAX_DISTANCE = 2048
DILATED_BRANCHES = ((128, 1), (512, 4), (2048, 16))
Q_BLOCK = 128
EPS = 1e-6
NEG_INF = -1e30
SB_MASK = 1e4
LOG2E = math.log2(math.e)

LANES = 128
A_CHUNK = 2048
A_BLOCKS_PER_ITER = 4
ATT_TILE = 256
ROW_TILE = 512
ONES_ROWS = 16
QUERY_TILES = 2
VMEM_LIMIT = 56 * 1024 * 1024


def _cparams(*sem):
    return pltpu.CompilerParams(dimension_semantics=sem, vmem_limit_bytes=VMEM_LIMIT)


def _rms(x, gain_row):
    ms = jnp.mean(x * x, axis=-1, keepdims=True)
    return x * lax.rsqrt(ms + EPS) * gain_row


def _dot(a, b):
    return jnp.dot(a, b, preferred_element_type=F32)


def _dot_nt(a, b):
    return lax.dot_general(a, b, (((1,), (1,)), ((), ())), preferred_element_type=F32)


def _ffn_kernel(x_ref, g_ref, wg_ref, wu_ref, wd_ref, o_ref):
    x = x_ref[...]
    h = _rms(x, g_ref[...]).astype(BF16)
    gate = _dot(h, wg_ref[...])
    up = _dot(h, wu_ref[...])
    act = (gate * jax.nn.sigmoid(gate) * up).astype(BF16)
    o_ref[...] = x + 0.5 * _dot(act, wd_ref[...])


def _ffn(x2, gain, wg, wu, wd):
    t, d = x2.shape
    dff = wg.shape[1]
    row = pl.BlockSpec((ROW_TILE, d), lambda i: (i, 0))
    full = lambda shape: pl.BlockSpec(shape, lambda i: (0, 0))
    return pl.pallas_call(
        _ffn_kernel,
        out_shape=jax.ShapeDtypeStruct((t, d), F32),
        grid=(t // ROW_TILE,),
        in_specs=[row, full((1, d)), full((d, dff)), full((d, dff)), full((dff, d))],
        out_specs=row,
        compiler_params=_cparams("parallel"),
        name="ffn_half_step",
    )(x2, gain.reshape(1, d), wg, wu, wd)


def _proj_kernel(widths, x_ref, g_ref, wn_ref, wt_ref, gqa_ref, gka_ref, gkb_ref, gqb_ref,
                 g64_ref, g32_ref, qa_ref, ka_ref, va_ref, kb_ref, kc_ref,
                 qbt_ref, vbt_ref, qct_ref, vct_ref):
    wa, wb, wc = widths
    h = _rms(x_ref[...], g_ref[...]).astype(BF16)
    sub = 256

    def group_norm(t, ones_ref, gain_ref, group, scale):
        ss = _dot((t * t).astype(BF16), ones_ref[...])
        return t * lax.rsqrt(ss * (1.0 / group) + EPS) * (gain_ref[...] * scale)

    def group_norm_t(t, ones_ref, gain_ref, group, scale):
        ss = _dot(ones_ref[...], (t * t).astype(BF16))
        return t * lax.rsqrt(ss * (1.0 / group) + EPS) * (gain_ref[...] * scale)

    nat = _dot(h, wn_ref[...])
    col = 0
    for ref, width, post in (
            (qa_ref, wa, lambda t: group_norm(t, g64_ref, gqa_ref, HEAD_DIM, HEAD_DIM ** -0.5)),
            (ka_ref, wa, lambda t: group_norm(t, g64_ref, gka_ref, HEAD_DIM, 1.0)),
            (va_ref, wa, lambda t: t),
            (kb_ref, wb, lambda t: group_norm(t, g32_ref, gkb_ref, DIFF_HALF, 1.0)),
            (kc_ref, wc, lambda t: t)):
        for c in range(0, width, sub):
            ref[:, c:c + sub] = post(nat[:, col + c:col + c + sub]).astype(ref.dtype)
        col += width

    tra = _dot_nt(wt_ref[...], h)
    row = 0
    for ref, width, post in (
            (qbt_ref, wb, lambda t: group_norm_t(t, g32_ref, gqb_ref, DIFF_HALF,
                                                 DIFF_HALF ** -0.5 * LOG2E)),
            (vbt_ref, wb, lambda t: t),
            (qct_ref, wc, lambda t: t * (HEAD_DIM ** -0.5 * LOG2E)),
            (vct_ref, wc, lambda t: t)):
        for r in range(0, width, sub):
            t = post(tra[row + r:row + r + sub]).astype(BF16)
            for j in range(ROW_TILE // ATT_TILE):
                ref[j, r:r + sub, :] = t[:, j * ATT_TILE:(j + 1) * ATT_TILE]
        row += width


def _block_diag_ones(group):
    idx = np.arange(256) // group
    return jnp.asarray(idx[:, None] == idx[None, :], dtype=BF16)


def _proj(x2, gain, w_in, gqa, gka, gqb, gkb, widths):
    t, d = x2.shape
    wa, wb, wc = widths
    assert wa % 256 == 0 and wb % 256 == 0 and wc % 256 == 0
    w = w_in.astype(BF16)
    o = np.cumsum([0, wa, wa, wa, wb, wb, wb, wc, wc, wc])
    sec = lambda i: w[:, o[i]:o[i + 1]]
    w_nat = jnp.concatenate([sec(0), sec(1), sec(2), sec(4), sec(7)], axis=1)
    w_tr = jnp.concatenate([sec(3), sec(5), sec(6), sec(8)], axis=1).T
    row = lambda wd: pl.BlockSpec((ROW_TILE, wd), lambda i: (i, 0))
    full = lambda shape: pl.BlockSpec(shape, lambda i: (0,) * len(shape))
    slab = lambda wd: pl.BlockSpec((ROW_TILE // ATT_TILE, wd, ATT_TILE), lambda i: (i, 0, 0))
    tile256 = lambda g: jnp.tile(g.astype(F32), 256 // g.shape[0])
    out_shape = ([jax.ShapeDtypeStruct((t, wa), F32)] * 3
                 + [jax.ShapeDtypeStruct((t, wb), BF16), jax.ShapeDtypeStruct((t, wc), BF16)]
                 + [jax.ShapeDtypeStruct((t // ATT_TILE, wd, ATT_TILE), BF16)
                    for wd in (wb, wb, wc, wc)])
    return pl.pallas_call(
        functools.partial(_proj_kernel, widths),
        out_shape=out_shape,
        grid=(t // ROW_TILE,),
        in_specs=[row(d), full((1, d)), full(w_nat.shape), full(w_tr.shape)]
                 + [full((1, 256))] * 3 + [full((256, 1))] + [full((256, 256))] * 2,
        out_specs=[row(wa)] * 3 + [row(wb), row(wc)] + [slab(wb), slab(wb), slab(wc), slab(wc)],
        compiler_params=_cparams("parallel"),
        name="norm_in_proj",
    )(x2, gain.reshape(1, d), w_nat, w_tr, tile256(gqa).reshape(1, 256), tile256(gka).reshape(1, 256),
      tile256(gkb).reshape(1, 256), tile256(gqb).reshape(256, 1),
      _block_diag_ones(HEAD_DIM), _block_diag_ones(DIFF_HALF))


def _dilated_kernel(q_ref, kp_ref, kc_ref, vp_ref, vc_ref, bias_ref, o_ref,
                    kk_ref, vv_ref, m_ref, l_ref, acc_ref):
    chunk = pl.program_id(1)
    kk_ref[0:A_CHUNK] = kp_ref[0]
    kk_ref[A_CHUNK:] = kc_ref[0]
    vv_ref[0:A_CHUNK] = vp_ref[0]
    vv_ref[A_CHUNK:] = vc_ref[0]

    lane = lax.broadcasted_iota(jnp.int32, (Q_BLOCK, LANES), 1)
    head0 = lane < HEAD_DIM

    order = sorted(range(len(DILATED_BRANCHES)), key=lambda b: -DILATED_BRANCHES[b][1])
    for bi in order:
        window, dil = DILATED_BRANCHES[bi]
        fresh = bi == order[0]
        nblk = A_CHUNK // (Q_BLOCK * dil)
        shift = int(math.log2(nblk))
        n_iter = dil * nblk
        ds = (lambda start, size, dil=dil:
              pl.ds(start, size, stride=dil) if dil > 1 else pl.ds(start, size))

        def load(idx, bi=bi, dil=dil, nblk=nblk, shift=shift, ds=ds, fresh=fresh):
            rho = idx >> shift
            t = idx & (nblk - 1)
            q_start = rho + t * (Q_BLOCK * dil)
            rows = ds(q_start, Q_BLOCK)
            krows = ds(A_CHUNK + q_start - Q_BLOCK * dil, 2 * Q_BLOCK)
            first = jnp.logical_and(chunk == 0, t == 0).astype(jnp.int32)
            blk = dict(rows=rows, q=q_ref[0, rows, :], k=kk_ref[krows, :], v=vv_ref[krows, :],
                       bias=bias_ref[bi, first, 0])
            if not fresh:
                blk.update(m=jnp.concatenate([m_ref[0, rows, :], m_ref[1, rows, :]], axis=0),
                           l=jnp.concatenate([l_ref[0, rows, :], l_ref[1, rows, :]], axis=0),
                           acc=acc_ref[rows, :])
            return blk

        def compute(b):
            q = b["q"]
            q2 = jnp.concatenate([jnp.where(head0, q, 0.0), jnp.where(head0, 0.0, q)],
                                 axis=0).astype(BF16)
            s = _dot_nt(q2, b["k"].astype(BF16)) + b["bias"]
            row_max = jnp.max(s, axis=-1, keepdims=True)
            if "m" not in b:
                m_new = jnp.broadcast_to(row_max, (2 * Q_BLOCK, LANES))
                p = jnp.exp(s - row_max)
                l_new = jnp.broadcast_to(jnp.sum(p, axis=-1, keepdims=True), (2 * Q_BLOCK, LANES))
                pv = _dot(p.astype(BF16), b["v"].astype(BF16))
                return m_new, l_new, jnp.where(head0, pv[:Q_BLOCK], pv[Q_BLOCK:])
            m_new = jnp.maximum(b["m"], row_max)
            alpha = jnp.exp(b["m"] - m_new)
            p = jnp.exp(s - jnp.concatenate([m_new, m_new], axis=1))
            l_new = alpha * b["l"] + jnp.sum(p, axis=-1, keepdims=True)
            pv = _dot(p.astype(BF16), b["v"].astype(BF16))
            acc_new = jnp.where(head0, alpha[:Q_BLOCK] * b["acc"] + pv[:Q_BLOCK],
                                alpha[Q_BLOCK:] * b["acc"] + pv[Q_BLOCK:])
            return m_new, l_new, acc_new

        def store(b, res):
            m_new, l_new, acc_new = res
            rows = b["rows"]
            acc_ref[rows, :] = acc_new
            m_ref[0, rows, :] = m_new[:Q_BLOCK]
            m_ref[1, rows, :] = m_new[Q_BLOCK:]
            l_ref[0, rows, :] = l_new[:Q_BLOCK]
            l_ref[1, rows, :] = l_new[Q_BLOCK:]

        def some_blocks(i, carry, load=load, compute=compute, store=store,
                        part=n_iter // A_BLOCKS_PER_ITER):
            blocks = [load(i + c * part) for c in range(A_BLOCKS_PER_ITER)]
            results = [compute(b) for b in blocks]
            for b, res in zip(blocks, results):
                store(b, res)
            return carry

        lax.fori_loop(0, n_iter // A_BLOCKS_PER_ITER, some_blocks, 0)

    lane_c = lax.broadcasted_iota(jnp.int32, (A_CHUNK, LANES), 1)
    denom = jnp.where(lane_c < HEAD_DIM, l_ref[0], l_ref[1])
    o_ref[0] = (acc_ref[...] / denom).astype(BF16)


def _dilated_attention(qa, ka, va, bias_a):
    b, s, wa = qa.shape
    n_pairs = wa // LANES
    cur = pl.BlockSpec((1, A_CHUNK, LANES), lambda i, c, p: (i, c, p))
    prev = pl.BlockSpec((1, A_CHUNK, LANES), lambda i, c, p: (i, jnp.maximum(c - 1, 0), p))
    nb = len(DILATED_BRANCHES)
    bias = pl.BlockSpec((nb, 2, 1, 2 * Q_BLOCK, 2 * Q_BLOCK), lambda i, c, p: (0, 0, p, 0, 0))
    return pl.pallas_call(
        _dilated_kernel,
        out_shape=jax.ShapeDtypeStruct((b, s, wa), BF16),
        grid=(b, s // A_CHUNK, n_pairs),
        in_specs=[cur, prev, cur, prev, cur, bias],
        out_specs=cur,
        scratch_shapes=[pltpu.VMEM((2 * A_CHUNK, LANES), F32),
                        pltpu.VMEM((2 * A_CHUNK, LANES), F32),
                        pltpu.VMEM((2, A_CHUNK, LANES), F32),
                        pltpu.VMEM((2, A_CHUNK, LANES), F32),
                        pltpu.VMEM((A_CHUNK, LANES), F32)],
        compiler_params=_cparams("parallel", "parallel", "parallel"),
        name="dilated_attention",
    )(qa, ka, ka, va, va, bias_a)


def _diff_kernel(n_bias, qt_ref, k_ref, vt_ref, bias_ref, lam_ref, gain_ref, o_ref,
                 q4_ref, m_ref, mt_ref, acc_ref, s_ref, p_ref):
    t = ATT_TILE
    row = lax.broadcasted_iota(jnp.int32, (LANES, t), 0)
    ones = jnp.ones((ONES_ROWS, 2 * t), BF16)

    def pipeline(g):
        qi = pl.program_id(2) * QUERY_TILES + g

        def tile_of(step):
            return jnp.clip(qi - step, 0, qi)

        def init():
            qt = qt_ref[0, g]
            for c in range(4):
                sel = jnp.logical_and(row >= c * DIFF_HALF, row < (c + 1) * DIFF_HALF)
                q4_ref[g, :, c * t:(c + 1) * t] = jnp.where(sel, qt, jnp.zeros_like(qt))
            m_ref[g] = jnp.full(m_ref.shape[1:], NEG_INF, F32)
            acc_ref[g] = jnp.zeros(acc_ref.shape[1:], F32)
            p_ref[g] = jnp.zeros(p_ref.shape[1:], BF16)

        def values(i):
            vt = jnp.concatenate([vt_ref[0, tile_of(2 * i)], vt_ref[0, tile_of(2 * i + 1)]], axis=1)
            return _dot(jnp.concatenate([vt, ones], axis=0), p_ref[g])

        def logits(i):
            tile_max = None
            for half in range(2):
                step = 2 * i + half
                k = k_ref[0, pl.ds(pl.multiple_of(tile_of(step) * t, t), t), :]
                d = jnp.where(step > qi, n_bias, jnp.minimum(step, n_bias - 1))
                b0 = bias_ref[d, 0]
                b1 = bias_ref[d, 1]
                s = _dot(k, q4_ref[g]) + jnp.concatenate([b0, b0, b1, b1], axis=1)
                s_ref[g, half * t:(half + 1) * t] = s
                mx = jnp.max(s, axis=0, keepdims=True)
                tile_max = mx if tile_max is None else jnp.maximum(tile_max, mx)
            mt_ref[g] = tile_max

        def sweep(i):
            pv = values(i - 1)
            m_old = m_ref[g]
            m_new = jnp.maximum(m_old, mt_ref[g])
            alpha = jnp.exp2(m_old - m_new)
            m_ref[g] = m_new
            p_ref[g] = jnp.exp2(s_ref[g] - m_new).astype(BF16)
            acc_ref[g] = alpha * (acc_ref[g] + pv)
            logits(i + 1)

        def fill():
            init()
            logits(0)

        def drain(n_iter):
            acc = acc_ref[g] + values(n_iter - 1)
            pv = acc[0:LANES] / acc[LANES:LANES + 1]
            lam = lam_ref[...]
            outs = []
            for h in range(2):
                rows = slice(h * HEAD_DIM, (h + 1) * HEAD_DIM)
                diff = (pv[rows, (2 * h) * t:(2 * h + 1) * t]
                        - lam * pv[rows, (2 * h + 1) * t:(2 * h + 2) * t])
                ms = jnp.mean(diff * diff, axis=0, keepdims=True)
                outs.append(diff * lax.rsqrt(ms + EPS))
            out_t = jnp.concatenate(outs, axis=0) * gain_ref[...]
            o_ref[0, g * t:(g + 1) * t] = out_t.T.astype(BF16)

        return fill, sweep, drain

    stages = [pipeline(g) for g in range(QUERY_TILES)]
    for fill, _, _ in stages:
        fill()
    n_iter = (pl.program_id(2) * QUERY_TILES + QUERY_TILES + 1) // 2

    def body(i, c):
        for _, sweep, _ in stages:
            sweep(i)
        return c

    lax.fori_loop(0, n_iter, body, 0)
    for _, _, drain in stages:
        drain(n_iter)


def _diff_attention(qbt, kb, vbt, bias_bt, lam, post_gain):
    b, s, wb = kb.shape
    n_pairs = wb // LANES
    n_bias = bias_bt.shape[0] - 1
    t = ATT_TILE
    nt = s // t
    g = QUERY_TILES
    qspec = pl.BlockSpec((1, g, LANES, t), lambda i, p, j: (i, j, p, 0))
    kspec = pl.BlockSpec((1, s, LANES), lambda i, p, j: (i, 0, p))
    vspec = pl.BlockSpec((1, nt, LANES, t), lambda i, p, j: (i, 0, p, 0))
    bspec = pl.BlockSpec((n_bias + 1, 2, t, t), lambda i, p, j: (0, p, 0, 0))
    return pl.pallas_call(
        functools.partial(_diff_kernel, n_bias),
        out_shape=jax.ShapeDtypeStruct((b, s, wb), BF16),
        grid=(b, n_pairs, nt // g),
        in_specs=[qspec, kspec, vspec, bspec,
                  pl.BlockSpec((1, t), lambda i, p, j: (0, 0)),
                  pl.BlockSpec((LANES, t), lambda i, p, j: (0, 0))],
        out_specs=pl.BlockSpec((1, g * t, LANES), lambda i, p, j: (i, j, p)),
        scratch_shapes=[pltpu.VMEM((g, LANES, 4 * t), BF16),
                        pltpu.VMEM((g, 1, 4 * t), F32),
                        pltpu.VMEM((g, 1, 4 * t), F32),
                        pltpu.VMEM((g, LANES + ONES_ROWS, 4 * t), F32),
                        pltpu.VMEM((g, 2 * t, 4 * t), F32),
                        pltpu.VMEM((g, 2 * t, 4 * t), BF16)],
        compiler_params=_cparams("parallel", "parallel", "parallel"),
        name="diff_attention",
    )(qbt.reshape(b, nt, wb, t), kb, vbt.reshape(b, nt, wb, t), bias_bt, lam, post_gain)


def _stick_kernel(qt_ref, k_ref, vt_ref, tri_ref, o_ref, q2_ref, carry_ref, scale_ref, acc_ref,
                  z_ref, lw_ref, p_ref):
    t = ATT_TILE
    row = lax.broadcasted_iota(jnp.int32, (LANES, t), 0)

    def pipeline(g):
        qi = pl.program_id(2) * QUERY_TILES + g

        def tile_of(step):
            return jnp.clip(qi - step, 0, qi)

        def init():
            qt = qt_ref[0, g]
            zero = jnp.zeros_like(qt)
            q2_ref[g, :, 0:t] = jnp.where(row < HEAD_DIM, qt, zero)
            q2_ref[g, :, t:] = jnp.where(row < HEAD_DIM, zero, qt)
            carry_ref[g] = jnp.zeros(carry_ref.shape[1:], F32)
            scale_ref[g] = jnp.ones(scale_ref.shape[1:], F32)
            acc_ref[g] = jnp.zeros(acc_ref.shape[1:], F32)
            p_ref[g] = jnp.zeros(p_ref.shape[1:], BF16)

        def logits(i, diagonal=False):
            for half in range(2):
                k = k_ref[0, pl.ds(pl.multiple_of(tile_of(2 * i + half) * t, t), t), :]
                z = _dot(k, q2_ref[g])
                if diagonal and half == 0:
                    key = lax.broadcasted_iota(jnp.int32, (t, t), 0)
                    qry = lax.broadcasted_iota(jnp.int32, (t, t), 1)
                    strict = jnp.concatenate([key < qry, key < qry], axis=1)
                    z = jnp.where(strict, z, -SB_MASK * LOG2E)
                z_ref[g, half] = z

        def log_weights():
            for half in range(2):
                z = z_ref[g, half]
                neg_abs = pltpu.bitcast(pltpu.bitcast(z, jnp.uint32) | jnp.uint32(0x80000000), F32)
                sp = jnp.maximum(z, 0.0) + jnp.log(1.0 + jnp.exp2(neg_abs)) * LOG2E
                w = _dot(tri_ref[...], sp.astype(BF16))
                lw_ref[g, half, 0:t] = z + w[0:t]
                lw_ref[g, half, t:] = w[t:]

        def weights():
            carry = carry_ref[g]
            near_sum = lw_ref[g, 0, t:t + 1]
            p_ref[g, 0:t] = jnp.exp2(lw_ref[g, 0, 0:t]).astype(BF16)
            p_ref[g, t:] = jnp.exp2(lw_ref[g, 1, 0:t] + near_sum).astype(BF16)
            scale_ref[g] = jnp.exp2(carry)
            carry_ref[g] = carry + near_sum + lw_ref[g, 1, t:t + 1]

        def values(i):
            vts = []
            for half in range(2):
                step = 2 * i + half
                valid = jnp.logical_and(step >= 0, step <= qi)
                vt = vt_ref[0, tile_of(step)]
                vts.append(jnp.where(valid, vt, jnp.zeros_like(vt)))
            acc_ref[g] += _dot(jnp.concatenate(vts, axis=1), p_ref[g]) * scale_ref[g]

        def sweep(i):
            values(i - 1)
            weights()
            log_weights()
            logits(i + 2)

        def fill():
            init()
            logits(0, diagonal=True)
            log_weights()
            logits(1)

        def drain(n_iter):
            values(n_iter - 1)
            out_t = jnp.concatenate([acc_ref[g, 0:HEAD_DIM, 0:t], acc_ref[g, HEAD_DIM:, t:]], axis=0)
            o_ref[0, g * t:(g + 1) * t] = out_t.T.astype(BF16)

        return fill, sweep, drain

    stages = [pipeline(g) for g in range(QUERY_TILES)]
    for fill, _, _ in stages:
        fill()
    n_iter = (pl.program_id(2) * QUERY_TILES + QUERY_TILES + 1) // 2

    def body(i, c):
        for _, sweep, _ in stages:
            sweep(i)
        return c

    lax.fori_loop(0, n_iter, body, 0)
    for _, _, drain in stages:
        drain(n_iter)


def _stick_attention(qct, kc, vct):
    b, s, wc = kc.shape
    n_pairs = wc // LANES
    t = ATT_TILE
    nt = s // t
    g = QUERY_TILES
    idx = np.arange(t)
    tri = np.concatenate([idx[None, :] >= idx[:, None], np.ones((ONES_ROWS, t), bool)], axis=0)
    tri = -jnp.asarray(tri, dtype=BF16)
    return pl.pallas_call(
        _stick_kernel,
        out_shape=jax.ShapeDtypeStruct((b, s, wc), BF16),
        grid=(b, n_pairs, nt // g),
        in_specs=[pl.BlockSpec((1, g, LANES, t), lambda i, p, j: (i, j, p, 0)),
                  pl.BlockSpec((1, s, LANES), lambda i, p, j: (i, 0, p)),
                  pl.BlockSpec((1, nt, LANES, t), lambda i, p, j: (i, 0, p, 0)),
                  pl.BlockSpec((t + ONES_ROWS, t), lambda i, p, j: (0, 0))],
        out_specs=pl.BlockSpec((1, g * t, LANES), lambda i, p, j: (i, j, p)),
        scratch_shapes=[pltpu.VMEM((g, LANES, 2 * t), BF16),
                        pltpu.VMEM((g, 1, 2 * t), F32),
                        pltpu.VMEM((g, 1, 2 * t), F32),
                        pltpu.VMEM((g, LANES, 2 * t), F32),
                        pltpu.VMEM((g, 2, t, 2 * t), F32),
                        pltpu.VMEM((g, 2, t + ONES_ROWS, 2 * t), F32),
                        pltpu.VMEM((g, 2 * t, 2 * t), BF16)],
        compiler_params=_cparams("parallel", "parallel", "parallel"),
        name="stick_breaking_attention",
    )(qct.reshape(b, nt, wc, t), kc, vct.reshape(b, nt, wc, t), tri)


def _out_proj_kernel(widths, x_ref, a_ref, b_ref, c_ref, w_ref, o_ref):
    wa, wb, wc = widths
    y = _dot(a_ref[...], w_ref[0:wa])
    y += _dot(b_ref[...], w_ref[wa:wa + wb])
    y += _dot(c_ref[...], w_ref[wa + wb:wa + wb + wc])
    o_ref[...] = x_ref[...] + y


def _out_proj(x2, oa, ob, oc, w_out, widths):
    t, d = x2.shape
    row = lambda w: pl.BlockSpec((ROW_TILE, w), lambda i: (i, 0))
    return pl.pallas_call(
        functools.partial(_out_proj_kernel, widths),
        out_shape=jax.ShapeDtypeStruct((t, d), F32),
        grid=(t // ROW_TILE,),
        in_specs=[row(d), row(widths[0]), row(widths[1]), row(widths[2]),
                  pl.BlockSpec(w_out.shape, lambda i: (0, 0))],
        out_specs=row(d),
        compiler_params=_cparams("parallel"),
        name="out_proj_residual",
    )(x2, oa, ob, oc, w_out)


def _t5_bucket(dist):
    dist = jnp.maximum(dist, 0)
    max_exact = N_BUCKETS // 2
    d_f = jnp.maximum(dist, 1).astype(F32)
    large = max_exact + (jnp.log(d_f / max_exact) / math.log(MAX_DISTANCE / max_exact)
                         * (N_BUCKETS - max_exact)).astype(jnp.int32)
    large = jnp.minimum(large, N_BUCKETS - 1)
    return jnp.where(dist < max_exact, dist, large)


def _bias_of_distance(bias, dist):
    bucket = _t5_bucket(dist)[None]
    out = jnp.zeros((bias.shape[1],) + dist.shape, F32)
    for b in range(N_BUCKETS):
        out = jnp.where(bucket == b, bias[b].reshape((-1,) + (1,) * dist.ndim), out)
    return out


def _dilated_bias_table(bias_a):
    n_heads = bias_a.shape[1]
    i = jnp.arange(Q_BLOCK, dtype=jnp.int32)[:, None]
    j = jnp.arange(2 * Q_BLOCK, dtype=jnp.int32)[None, :]
    tables = []
    for window, dil in DILATED_BRANCHES:
        n = window // dil
        off = i + n - j
        band = (off >= 0) & (off <= n)
        bias = _bias_of_distance(bias_a, off * dil)
        variants = [jnp.where(valid[None], bias, NEG_INF) for valid in (band, band & (j >= n))]
        tables.append(jnp.stack(variants))
    table = jnp.stack(tables)
    return table.reshape(len(DILATED_BRANCHES), 2, n_heads // 2, 2 * Q_BLOCK, 2 * Q_BLOCK)


def _diff_bias_table(bias_b, seq):
    t = ATT_TILE
    n_bias = min(seq // t, MAX_DISTANCE // t + 2)
    key = jnp.arange(t, dtype=jnp.int32)[None, :, None]
    qry = jnp.arange(t, dtype=jnp.int32)[None, None, :]
    dist = jnp.arange(n_bias + 1, dtype=jnp.int32)[:, None, None] * t + qry - key
    valid = (dist >= 0) & (jnp.arange(n_bias + 1)[:, None, None] < n_bias)
    tiles = jnp.where(valid[None], _bias_of_distance(bias_b, dist) * LOG2E, NEG_INF)
    return jnp.swapaxes(tiles, 0, 1)


def kernel(x, rel_bias, ffn1_norm, ffn1_w_gate, ffn1_w_up, ffn1_w_down, mix_norm, w_in,
           q_norm_a, k_norm_a, q_norm_b, k_norm_b, lambda_q1, lambda_k1, lambda_q2, lambda_k2,
           diff_subln, w_out, ffn2_norm, ffn2_w_gate, ffn2_w_up, ffn2_w_down):
    b, s, d = x.shape
    depth = w_in.shape[0]
    n_heads = d // HEAD_DIM
    wa = (n_heads // 2) * HEAD_DIM
    wb = (n_heads // 4) * HEAD_DIM
    wc = d - wa - wb
    widths = (wa, wb, wc)
    assert w_in.shape[2] == 3 * d and s % A_CHUNK == 0 and (b * s) % ROW_TILE == 0

    rb = rel_bias.astype(F32)
    bias_a = _dilated_bias_table(rb[:, :wa // HEAD_DIM])
    bias_bt = _diff_bias_table(rb[:, wa // HEAD_DIM:], s)

    x2 = x.reshape(b * s, d)
    for layer in range(depth):
        x2 = _ffn(x2, ffn1_norm[layer], ffn1_w_gate[layer].astype(BF16),
                  ffn1_w_up[layer].astype(BF16), ffn1_w_down[layer].astype(BF16))

        qa, ka, va, kb, kc, qbt, vbt, qct, vct = _proj(
            x2, mix_norm[layer], w_in[layer], q_norm_a[layer], k_norm_a[layer],
            q_norm_b[layer], k_norm_b[layer], widths)
        seq3 = lambda t: t.reshape(b, s, t.shape[-1])

        out_a = _dilated_attention(seq3(qa), seq3(ka), seq3(va), bias_a)

        lam_init = 0.8 - 0.6 * math.exp(-0.3 * layer)
        lam = (jnp.exp(jnp.sum(lambda_q1[layer].astype(F32) * lambda_k1[layer].astype(F32)))
               - jnp.exp(jnp.sum(lambda_q2[layer].astype(F32) * lambda_k2[layer].astype(F32)))
               + lam_init)
        lam_row = jnp.full((1, ATT_TILE), lam, F32)
        post_gain = jnp.broadcast_to(
            (jnp.tile(diff_subln[layer].astype(F32), LANES // HEAD_DIM) * (1.0 - lam_init))[:, None],
            (LANES, ATT_TILE))
        out_b = _diff_attention(qbt, seq3(kb), vbt, bias_bt, lam_row, post_gain)

        out_c = _stick_attention(qct, seq3(kc), vct)

        x2 = _out_proj(x2, out_a.reshape(b * s, wa), out_b.reshape(b * s, wb),
                       out_c.reshape(b * s, wc), w_out[layer].astype(BF16), widths)

        x2 = _ffn(x2, ffn2_norm[layer], ffn2_w_gate[layer].astype(BF16),
                  ffn2_w_up[layer].astype(BF16), ffn2_w_down[layer].astype(BF16))
    return x2.reshape(b, s, d)
```

```python
import functools
import math

import jax
import jax.numpy as jnp
import numpy as np
from jax import lax
from jax.experimental import pallas as pl
from jax.experimental.pallas import tpu as pltpu

F32 = jnp.float32
BF16 = jnp.bfloat16

HEAD_DIM = 64
DIFF_HALF = HEAD_DIM // 2
N_BUCKETS = 32
MAX_DISTANCE = 2048
DILATED_BRANCHES = ((128, 1), (512, 4), (2048, 16))
Q_BLOCK = 128
EPS = 1e-6
NEG_INF = -1e30
SB_MASK = 1e4
LOG2E = math.log2(math.e)

LANES = 128
A_CHUNK = 2048
A_BLOCKS_PER_ITER = 4
ATT_TILE = 256
ROW_TILE = 512
ONES_ROWS = 16
QUERY_TILES = 2
VMEM_LIMIT = 56 * 1024 * 1024


def _cparams(*sem):
    return pltpu.CompilerParams(dimension_semantics=sem, vmem_limit_bytes=VMEM_LIMIT)


def _rms(x, gain_row):
    ms = jnp.mean(x * x, axis=-1, keepdims=True)
    return x * lax.rsqrt(ms + EPS) * gain_row


def _dot(a, b):
    return jnp.dot(a, b, preferred_element_type=F32)


def _dot_nt(a, b):
    return lax.dot_general(a, b, (((1,), (1,)), ((), ())), preferred_element_type=F32)


def _ffn_kernel(x_ref, g_ref, wg_ref, wu_ref, wd_ref, o_ref):
    x = x_ref[...]
    h = _rms(x, g_ref[...]).astype(BF16)
    gate = _dot(h, wg_ref[...])
    up = _dot(h, wu_ref[...])
    act = (gate * jax.nn.sigmoid(gate) * up).astype(BF16)
    o_ref[...] = x + 0.5 * _dot(act, wd_ref[...])


def _ffn(x2, gain, wg, wu, wd):
    t, d = x2.shape
    dff = wg.shape[1]
    row = pl.BlockSpec((ROW_TILE, d), lambda i: (i, 0))
    full = lambda shape: pl.BlockSpec(shape, lambda i: (0, 0))
    return pl.pallas_call(
        _ffn_kernel,
        out_shape=jax.ShapeDtypeStruct((t, d), F32),
        grid=(t // ROW_TILE,),
        in_specs=[row, full((1, d)), full((d, dff)), full((d, dff)), full((dff, d))],
        out_specs=row,
        compiler_params=_cparams("parallel"),
        name="ffn_half_step",
    )(x2, gain.reshape(1, d), wg, wu, wd)


def _proj_kernel(widths, x_ref, g_ref, wn_ref, wt_ref, gqa_ref, gka_ref, gkb_ref, gqb_ref,
                 g64_ref, g32_ref, qa_ref, ka_ref, va_ref, kb_ref, kc_ref,
                 qbt_ref, vbt_ref, qct_ref, vct_ref):
    wa, wb, wc = widths
    h = _rms(x_ref[...], g_ref[...]).astype(BF16)
    sub = 256

    def group_norm(t, ones_ref, gain_ref, group, scale):
        ss = _dot((t * t).astype(BF16), ones_ref[...])
        return t * lax.rsqrt(ss * (1.0 / group) + EPS) * (gain_ref[...] * scale)

    def group_norm_t(t, ones_ref, gain_ref, group, scale):
        ss = _dot(ones_ref[...], (t * t).astype(BF16))
        return t * lax.rsqrt(ss * (1.0 / group) + EPS) * (gain_ref[...] * scale)

    nat = _dot(h, wn_ref[...])
    col = 0
    for ref, width, post in (
            (qa_ref, wa, lambda t: group_norm(t, g64_ref, gqa_ref, HEAD_DIM, HEAD_DIM ** -0.5)),
            (ka_ref, wa, lambda t: group_norm(t, g64_ref, gka_ref, HEAD_DIM, 1.0)),
            (va_ref, wa, lambda t: t),
            (kb_ref, wb, lambda t: group_norm(t, g32_ref, gkb_ref, DIFF_HALF, 1.0)),
            (kc_ref, wc, lambda t: t)):
        for c in range(0, width, sub):
            ref[:, c:c + sub] = post(nat[:, col + c:col + c + sub]).astype(ref.dtype)
        col += width

    tra = _dot_nt(wt_ref[...], h)
    row = 0
    for ref, width, post in (
            (qbt_ref, wb, lambda t: group_norm_t(t, g32_ref, gqb_ref, DIFF_HALF,
                                                 DIFF_HALF ** -0.5 * LOG2E)),
            (vbt_ref, wb, lambda t: t),
            (qct_ref, wc, lambda t: t * (HEAD_DIM ** -0.5 * LOG2E)),
            (vct_ref, wc, lambda t: t)):
        for r in range(0, width, sub):
            t = post(tra[row + r:row + r + sub]).astype(BF16)
            for j in range(ROW_TILE // ATT_TILE):
                ref[j, r:r + sub, :] = t[:, j * ATT_TILE:(j + 1) * ATT_TILE]
        row += width


def _block_diag_ones(group):
    idx = np.arange(256) // group
    return jnp.asarray(idx[:, None] == idx[None, :], dtype=BF16)


def _proj(x2, gain, w_in, gqa, gka, gqb, gkb, widths):
    t, d = x2.shape
    wa, wb, wc = widths
    assert wa % 256 == 0 and wb % 256 == 0 and wc % 256 == 0
    w = w_in.astype(BF16)
    o = np.cumsum([0, wa, wa, wa, wb, wb, wb, wc, wc, wc])
    sec = lambda i: w[:, o[i]:o[i + 1]]
    w_nat = jnp.concatenate([sec(0), sec(1), sec(2), sec(4), sec(7)], axis=1)
    w_tr = jnp.concatenate([sec(3), sec(5), sec(6), sec(8)], axis=1).T
    row = lambda wd: pl.BlockSpec((ROW_TILE, wd), lambda i: (i, 0))
    full = lambda shape: pl.BlockSpec(shape, lambda i: (0,) * len(shape))
    slab = lambda wd: pl.BlockSpec((ROW_TILE // ATT_TILE, wd, ATT_TILE), lambda i: (i, 0, 0))
    tile256 = lambda g: jnp.tile(g.astype(F32), 256 // g.shape[0])
    out_shape = ([jax.ShapeDtypeStruct((t, wa), F32)] * 3
                 + [jax.ShapeDtypeStruct((t, wb), BF16), jax.ShapeDtypeStruct((t, wc), BF16)]
                 + [jax.ShapeDtypeStruct((t // ATT_TILE, wd, ATT_TILE), BF16)
                    for wd in (wb, wb, wc, wc)])
    return pl.pallas_call(
        functools.partial(_proj_kernel, widths),
        out_shape=out_shape,
        grid=(t // ROW_TILE,),
        in_specs=[row(d), full((1, d)), full(w_nat.shape), full(w_tr.shape)]
                 + [full((1, 256))] * 3 + [full((256, 1))] + [full((256, 256))] * 2,
        out_specs=[row(wa)] * 3 + [row(wb), row(wc)] + [slab(wb), slab(wb), slab(wc), slab(wc)],
        compiler_params=_cparams("parallel"),
        name="norm_in_proj",
    )(x2, gain.reshape(1, d), w_nat, w_tr, tile256(gqa).reshape(1, 256), tile256(gka).reshape(1, 256),
      tile256(gkb).reshape(1, 256), tile256(gqb).reshape(256, 1),
      _block_diag_ones(HEAD_DIM), _block_diag_ones(DIFF_HALF))


def _dilated_kernel(q_ref, kp_ref, kc_ref, vp_ref, vc_ref, bias_ref, o_ref,
                    kk_ref, vv_ref, m_ref, l_ref, acc_ref):
    chunk = pl.program_id(1)
    kk_ref[0:A_CHUNK] = kp_ref[0]
    kk_ref[A_CHUNK:] = kc_ref[0]
    vv_ref[0:A_CHUNK] = vp_ref[0]
    vv_ref[A_CHUNK:] = vc_ref[0]

    lane = lax.broadcasted_iota(jnp.int32, (Q_BLOCK, LANES), 1)
    head0 = lane < HEAD_DIM

    order = sorted(range(len(DILATED_BRANCHES)), key=lambda b: -DILATED_BRANCHES[b][1])
    for bi in order:
        window, dil = DILATED_BRANCHES[bi]
        fresh = bi == order[0]
        nblk = A_CHUNK // (Q_BLOCK * dil)
        shift = int(math.log2(nblk))
        n_iter = dil * nblk
        ds = (lambda start, size, dil=dil:
              pl.ds(start, size, stride=dil) if dil > 1 else pl.ds(start, size))

        def load(idx, bi=bi, dil=dil, nblk=nblk, shift=shift, ds=ds, fresh=fresh):
            rho = idx >> shift
            t = idx & (nblk - 1)
            q_start = rho + t * (Q_BLOCK * dil)
            rows = ds(q_start, Q_BLOCK)
            krows = ds(A_CHUNK + q_start - Q_BLOCK * dil, 2 * Q_BLOCK)
            first = jnp.logical_and(chunk == 0, t == 0).astype(jnp.int32)
            blk = dict(rows=rows, q=q_ref[0, rows, :], k=kk_ref[krows, :], v=vv_ref[krows, :],
                       bias=bias_ref[bi, first, 0])
            if not fresh:
                blk.update(m=jnp.concatenate([m_ref[0, rows, :], m_ref[1, rows, :]], axis=0),
                           l=jnp.concatenate([l_ref[0, rows, :], l_ref[1, rows, :]], axis=0),
                           acc=acc_ref[rows, :])
            return blk

        def compute(b):
            q = b["q"]
            q2 = jnp.concatenate([jnp.where(head0, q, 0.0), jnp.where(head0, 0.0, q)],
                                 axis=0).astype(BF16)
            s = _dot_nt(q2, b["k"].astype(BF16)) + b["bias"]
            row_max = jnp.max(s, axis=-1, keepdims=True)
            if "m" not in b:
                m_new = jnp.broadcast_to(row_max, (2 * Q_BLOCK, LANES))
                p = jnp.exp(s - row_max)
                l_new = jnp.broadcast_to(jnp.sum(p, axis=-1, keepdims=True), (2 * Q_BLOCK, LANES))
                pv = _dot(p.astype(BF16), b["v"].astype(BF16))
                return m_new, l_new, jnp.where(head0, pv[:Q_BLOCK], pv[Q_BLOCK:])
            m_new = jnp.maximum(b["m"], row_max)
            alpha = jnp.exp(b["m"] - m_new)
            p = jnp.exp(s - jnp.concatenate([m_new, m_new], axis=1))
            l_new = alpha * b["l"] + jnp.sum(p, axis=-1, keepdims=True)
            pv = _dot(p.astype(BF16), b["v"].astype(BF16))
            acc_new = jnp.where(head0, alpha[:Q_BLOCK] * b["acc"] + pv[:Q_BLOCK],
                                alpha[Q_BLOCK:] * b["acc"] + pv[Q_BLOCK:])
            return m_new, l_new, acc_new

        def store(b, res):
            m_new, l_new, acc_new = res
            rows = b["rows"]
            acc_ref[rows, :] = acc_new
            m_ref[0, rows, :] = m_new[:Q_BLOCK]
            m_ref[1, rows, :] = m_new[Q_BLOCK:]
            l_ref[0, rows, :] = l_new[:Q_BLOCK]
            l_ref[1, rows, :] = l_new[Q_BLOCK:]

        def some_blocks(i, carry, load=load, compute=compute, store=store,
                        part=n_iter // A_BLOCKS_PER_ITER):
            blocks = [load(i + c * part) for c in range(A_BLOCKS_PER_ITER)]
            results = [compute(b) for b in blocks]
            for b, res in zip(blocks, results):
                store(b, res)
            return carry

        lax.fori_loop(0, n_iter // A_BLOCKS_PER_ITER, some_blocks, 0)

    lane_c = lax.broadcasted_iota(jnp.int32, (A_CHUNK, LANES), 1)
    denom = jnp.where(lane_c < HEAD_DIM, l_ref[0], l_ref[1])
    o_ref[0] = (acc_ref[...] / denom).astype(BF16)


def _dilated_attention(qa, ka, va, bias_a):
    b, s, wa = qa.shape
    n_pairs = wa // LANES
    cur = pl.BlockSpec((1, A_CHUNK, LANES), lambda i, c, p: (i, c, p))
    prev = pl.BlockSpec((1, A_CHUNK, LANES), lambda i, c, p: (i, jnp.maximum(c - 1, 0), p))
    nb = len(DILATED_BRANCHES)
    bias = pl.BlockSpec((nb, 2, 1, 2 * Q_BLOCK, 2 * Q_BLOCK), lambda i, c, p: (0, 0, p, 0, 0))
    return pl.pallas_call(
        _dilated_kernel,
        out_shape=jax.ShapeDtypeStruct((b, s, wa), BF16),
        grid=(b, s // A_CHUNK, n_pairs),
        in_specs=[cur, prev, cur, prev, cur, bias],
        out_specs=cur,
        scratch_shapes=[pltpu.VMEM((2 * A_CHUNK, LANES), F32),
                        pltpu.VMEM((2 * A_CHUNK, LANES), F32),
                        pltpu.VMEM((2, A_CHUNK, LANES), F32),
                        pltpu.VMEM((2, A_CHUNK, LANES), F32),
                        pltpu.VMEM((A_CHUNK, LANES), F32)],
        compiler_params=_cparams("parallel", "parallel", "parallel"),
        name="dilated_attention",
    )(qa, ka, ka, va, va, bias_a)


def _diff_kernel(n_bias, qt_ref, k_ref, vt_ref, bias_ref, lam_ref, gain_ref, o_ref,
                 q4_ref, m_ref, mt_ref, acc_ref, s_ref, p_ref):
    t = ATT_TILE
    row = lax.broadcasted_iota(jnp.int32, (LANES, t), 0)
    ones = jnp.ones((ONES_ROWS, 2 * t), BF16)

    def pipeline(g):
        qi = pl.program_id(2) * QUERY_TILES + g

        def tile_of(step):
            return jnp.clip(qi - step, 0, qi)

        def init():
            qt = qt_ref[0, g]
            for c in range(4):
                sel = jnp.logical_and(row >= c * DIFF_HALF, row < (c + 1) * DIFF_HALF)
                q4_ref[g, :, c * t:(c + 1) * t] = jnp.where(sel, qt, jnp.zeros_like(qt))
            m_ref[g] = jnp.full(m_ref.shape[1:], NEG_INF, F32)
            acc_ref[g] = jnp.zeros(acc_ref.shape[1:], F32)
            p_ref[g] = jnp.zeros(p_ref.shape[1:], BF16)

        def values(i):
            vt = jnp.concatenate([vt_ref[0, tile_of(2 * i)], vt_ref[0, tile_of(2 * i + 1)]], axis=1)
            return _dot(jnp.concatenate([vt, ones], axis=0), p_ref[g])

        def logits(i):
            tile_max = None
            for half in range(2):
                step = 2 * i + half
                k = k_ref[0, pl.ds(pl.multiple_of(tile_of(step) * t, t), t), :]
                d = jnp.where(step > qi, n_bias, jnp.minimum(step, n_bias - 1))
                b0 = bias_ref[d, 0]
                b1 = bias_ref[d, 1]
                s = _dot(k, q4_ref[g]) + jnp.concatenate([b0, b0, b1, b1], axis=1)
                s_ref[g, half * t:(half + 1) * t] = s
                mx = jnp.max(s, axis=0, keepdims=True)
                tile_max = mx if tile_max is None else jnp.maximum(tile_max, mx)
            mt_ref[g] = tile_max

        def sweep(i):
            pv = values(i - 1)
            m_old = m_ref[g]
            m_new = jnp.maximum(m_old, mt_ref[g])
            alpha = jnp.exp2(m_old - m_new)
            m_ref[g] = m_new
            p_ref[g] = jnp.exp2(s_ref[g] - m_new).astype(BF16)
            acc_ref[g] = alpha * (acc_ref[g] + pv)
            logits(i + 1)

        def fill():
            init()
            logits(0)

        def drain(n_iter):
            acc = acc_ref[g] + values(n_iter - 1)
            pv = acc[0:LANES] / acc[LANES:LANES + 1]
            lam = lam_ref[...]
            outs = []
            for h in range(2):
                rows = slice(h * HEAD_DIM, (h + 1) * HEAD_DIM)
                diff = (pv[rows, (2 * h) * t:(2 * h + 1) * t]
                        - lam * pv[rows, (2 * h + 1) * t:(2 * h + 2) * t])
                ms = jnp.mean(diff * diff, axis=0, keepdims=True)
                outs.append(diff * lax.rsqrt(ms + EPS))
            out_t = jnp.concatenate(outs, axis=0) * gain_ref[...]
            o_ref[0, g * t:(g + 1) * t] = out_t.T.astype(BF16)

        return fill, sweep, drain

    stages = [pipeline(g) for g in range(QUERY_TILES)]
    for fill, _, _ in stages:
        fill()
    n_iter = (pl.program_id(2) * QUERY_TILES + QUERY_TILES + 1) // 2

    def body(i, c):
        for _, sweep, _ in stages:
            sweep(i)
        return c

    lax.fori_loop(0, n_iter, body, 0)
    for _, _, drain in stages:
        drain(n_iter)


def _diff_attention(qbt, kb, vbt, bias_bt, lam, post_gain):
    b, s, wb = kb.shape
    n_pairs = wb // LANES
    n_bias = bias_bt.shape[0] - 1
    t = ATT_TILE
    nt = s // t
    g = QUERY_TILES
    qspec = pl.BlockSpec((1, g, LANES, t), lambda i, p, j: (i, j, p, 0))
    kspec = pl.BlockSpec((1, s, LANES), lambda i, p, j: (i, 0, p))
    vspec = pl.BlockSpec((1, nt, LANES, t), lambda i, p, j: (i, 0, p, 0))
    bspec = pl.BlockSpec((n_bias + 1, 2, t, t), lambda i, p, j: (0, p, 0, 0))
    return pl.pallas_call(
        functools.partial(_diff_kernel, n_bias),
        out_shape=jax.ShapeDtypeStruct((b, s, wb), BF16),
        grid=(b, n_pairs, nt // g),
        in_specs=[qspec, kspec, vspec, bspec,
                  pl.BlockSpec((1, t), lambda i, p, j: (0, 0)),
                  pl.BlockSpec((LANES, t), lambda i, p, j: (0, 0))],
        out_specs=pl.BlockSpec((1, g * t, LANES), lambda i, p, j: (i, j, p)),
        scratch_shapes=[pltpu.VMEM((g, LANES, 4 * t), BF16),
                        pltpu.VMEM((g, 1, 4 * t), F32),
                        pltpu.VMEM((g, 1, 4 * t), F32),
                        pltpu.VMEM((g, LANES + ONES_ROWS, 4 * t), F32),
                        pltpu.VMEM((g, 2 * t, 4 * t), F32),
                        pltpu.VMEM((g, 2 * t, 4 * t), BF16)],
        compiler_params=_cparams("parallel", "parallel", "parallel"),
        name="diff_attention",
    )(qbt.reshape(b, nt, wb, t), kb, vbt.reshape(b, nt, wb, t), bias_bt, lam, post_gain)


def _stick_kernel(qt_ref, k_ref, vt_ref, tri_ref, o_ref, q2_ref, carry_ref, scale_ref, acc_ref,
                  z_ref, lw_ref, p_ref):
    t = ATT_TILE
    row = lax.broadcasted_iota(jnp.int32, (LANES, t), 0)

    def pipeline(g):
        qi = pl.program_id(2) * QUERY_TILES + g

        def tile_of(step):
            return jnp.clip(qi - step, 0, qi)

        def init():
            qt = qt_ref[0, g]
            zero = jnp.zeros_like(qt)
            q2_ref[g, :, 0:t] = jnp.where(row < HEAD_DIM, qt, zero)
            q2_ref[g, :, t:] = jnp.where(row < HEAD_DIM, zero, qt)
            carry_ref[g] = jnp.zeros(carry_ref.shape[1:], F32)
            scale_ref[g] = jnp.ones(scale_ref.shape[1:], F32)
            acc_ref[g] = jnp.zeros(acc_ref.shape[1:], F32)
            p_ref[g] = jnp.zeros(p_ref.shape[1:], BF16)

        def logits(i, diagonal=False):
            for half in range(2):
                k = k_ref[0, pl.ds(pl.multiple_of(tile_of(2 * i + half) * t, t), t), :]
                z = _dot(k, q2_ref[g])
                if diagonal and half == 0:
                    key = lax.broadcasted_iota(jnp.int32, (t, t), 0)
                    qry = lax.broadcasted_iota(jnp.int32, (t, t), 1)
                    strict = jnp.concatenate([key < qry, key < qry], axis=1)
                    z = jnp.where(strict, z, -SB_MASK * LOG2E)
                z_ref[g, half] = z

        def log_weights():
            for half in range(2):
                z = z_ref[g, half]
                neg_abs = pltpu.bitcast(pltpu.bitcast(z, jnp.uint32) | jnp.uint32(0x80000000), F32)
                sp = jnp.maximum(z, 0.0) + jnp.log(1.0 + jnp.exp2(neg_abs)) * LOG2E
                w = _dot(tri_ref[...], sp.astype(BF16))
                lw_ref[g, half, 0:t] = z + w[0:t]
                lw_ref[g, half, t:] = w[t:]

        def weights():
            carry = carry_ref[g]
            near_sum = lw_ref[g, 0, t:t + 1]
            p_ref[g, 0:t] = jnp.exp2(lw_ref[g, 0, 0:t]).astype(BF16)
            p_ref[g, t:] = jnp.exp2(lw_ref[g, 1, 0:t] + near_sum).astype(BF16)
            scale_ref[g] = jnp.exp2(carry)
            carry_ref[g] = carry + near_sum + lw_ref[g, 1, t:t + 1]

        def values(i):
            vts = []
            for half in range(2):
                step = 2 * i + half
                valid = jnp.logical_and(step >= 0, step <= qi)
                vt = vt_ref[0, tile_of(step)]
                vts.append(jnp.where(valid, vt, jnp.zeros_like(vt)))
            acc_ref[g] += _dot(jnp.concatenate(vts, axis=1), p_ref[g]) * scale_ref[g]

        def sweep(i):
            values(i - 1)
            weights()
            log_weights()
            logits(i + 2)

        def fill():
            init()
            logits(0, diagonal=True)
            log_weights()
            logits(1)

        def drain(n_iter):
            values(n_iter - 1)
            out_t = jnp.concatenate([acc_ref[g, 0:HEAD_DIM, 0:t], acc_ref[g, HEAD_DIM:, t:]], axis=0)
            o_ref[0, g * t:(g + 1) * t] = out_t.T.astype(BF16)

        return fill, sweep, drain

    stages = [pipeline(g) for g in range(QUERY_TILES)]
    for fill, _, _ in stages:
        fill()
    n_iter = (pl.program_id(2) * QUERY_TILES + QUERY_TILES + 1) // 2

    def body(state):
        i, _ = state
        for _, sweep, _ in stages:
            sweep(i)
        return i + 1, (jnp.max(scale_ref[...]) > 0.0).astype(jnp.int32)

    n_done, _ = lax.while_loop(lambda st: jnp.logical_and(st[0] < n_iter, st[1] > 0), body,
                               (jnp.int32(0), jnp.int32(1)))
    for _, _, drain in stages:
        drain(n_done)


def _stick_attention(qct, kc, vct):
    b, s, wc = kc.shape
    n_pairs = wc // LANES
    t = ATT_TILE
    nt = s // t
    g = QUERY_TILES
    idx = np.arange(t)
    tri = np.concatenate([idx[None, :] >= idx[:, None], np.ones((ONES_ROWS, t), bool)], axis=0)
    tri = -jnp.asarray(tri, dtype=BF16)
    return pl.pallas_call(
        _stick_kernel,
        out_shape=jax.ShapeDtypeStruct((b, s, wc), BF16),
        grid=(b, n_pairs, nt // g),
        in_specs=[pl.BlockSpec((1, g, LANES, t), lambda i, p, j: (i, j, p, 0)),
                  pl.BlockSpec((1, s, LANES), lambda i, p, j: (i, 0, p)),
                  pl.BlockSpec((1, nt, LANES, t), lambda i, p, j: (i, 0, p, 0)),
                  pl.BlockSpec((t + ONES_ROWS, t), lambda i, p, j: (0, 0))],
        out_specs=pl.BlockSpec((1, g * t, LANES), lambda i, p, j: (i, j, p)),
        scratch_shapes=[pltpu.VMEM((g, LANES, 2 * t), BF16),
                        pltpu.VMEM((g, 1, 2 * t), F32),
                        pltpu.VMEM((g, 1, 2 * t), F32),
                        pltpu.VMEM((g, LANES, 2 * t), F32),
                        pltpu.VMEM((g, 2, t, 2 * t), F32),
                        pltpu.VMEM((g, 2, t + ONES_ROWS, 2 * t), F32),
                        pltpu.VMEM((g, 2 * t, 2 * t), BF16)],
        compiler_params=_cparams("parallel", "parallel", "parallel"),
        name="stick_breaking_attention",
    )(qct.reshape(b, nt, wc, t), kc, vct.reshape(b, nt, wc, t), tri)


def _out_proj_kernel(widths, x_ref, a_ref, b_ref, c_ref, w_ref, o_ref):
    wa, wb, wc = widths
    y = _dot(a_ref[...], w_ref[0:wa])
    y += _dot(b_ref[...], w_ref[wa:wa + wb])
    y += _dot(c_ref[...], w_ref[wa + wb:wa + wb + wc])
    o_ref[...] = x_ref[...] + y


def _out_proj(x2, oa, ob, oc, w_out, widths):
    t, d = x2.shape
    row = lambda w: pl.BlockSpec((ROW_TILE, w), lambda i: (i, 0))
    return pl.pallas_call(
        functools.partial(_out_proj_kernel, widths),
        out_shape=jax.ShapeDtypeStruct((t, d), F32),
        grid=(t // ROW_TILE,),
        in_specs=[row(d), row(widths[0]), row(widths[1]), row(widths[2]),
                  pl.BlockSpec(w_out.shape, lambda i: (0, 0))],
        out_specs=row(d),
        compiler_params=_cparams("parallel"),
        name="out_proj_residual",
    )(x2, oa, ob, oc, w_out)


def _t5_bucket(dist):
    dist = jnp.maximum(dist, 0)
    max_exact = N_BUCKETS // 2
    d_f = jnp.maximum(dist, 1).astype(F32)
    large = max_exact + (jnp.log(d_f / max_exact) / math.log(MAX_DISTANCE / max_exact)
                         * (N_BUCKETS - max_exact)).astype(jnp.int32)
    large = jnp.minimum(large, N_BUCKETS - 1)
    return jnp.where(dist < max_exact, dist, large)


def _bias_of_distance(bias, dist):
    bucket = _t5_bucket(dist)[None]
    out = jnp.zeros((bias.shape[1],) + dist.shape, F32)
    for b in range(N_BUCKETS):
        out = jnp.where(bucket == b, bias[b].reshape((-1,) + (1,) * dist.ndim), out)
    return out


def _dilated_bias_table(bias_a):
    n_heads = bias_a.shape[1]
    i = jnp.arange(Q_BLOCK, dtype=jnp.int32)[:, None]
    j = jnp.arange(2 * Q_BLOCK, dtype=jnp.int32)[None, :]
    tables = []
    for window, dil in DILATED_BRANCHES:
        n = window // dil
        off = i + n - j
        band = (off >= 0) & (off <= n)
        bias = _bias_of_distance(bias_a, off * dil)
        variants = [jnp.where(valid[None], bias, NEG_INF) for valid in (band, band & (j >= n))]
        tables.append(jnp.stack(variants))
    table = jnp.stack(tables)
    return table.reshape(len(DILATED_BRANCHES), 2, n_heads // 2, 2 * Q_BLOCK, 2 * Q_BLOCK)


def _diff_bias_table(bias_b, seq):
    t = ATT_TILE
    n_bias = min(seq // t, MAX_DISTANCE // t + 2)
    key = jnp.arange(t, dtype=jnp.int32)[None, :, None]
    qry = jnp.arange(t, dtype=jnp.int32)[None, None, :]
    dist = jnp.arange(n_bias + 1, dtype=jnp.int32)[:, None, None] * t + qry - key
    valid = (dist >= 0) & (jnp.arange(n_bias + 1)[:, None, None] < n_bias)
    tiles = jnp.where(valid[None], _bias_of_distance(bias_b, dist) * LOG2E, NEG_INF)
    return jnp.swapaxes(tiles, 0, 1)


def kernel(x, rel_bias, ffn1_norm, ffn1_w_gate, ffn1_w_up, ffn1_w_down, mix_norm, w_in,
           q_norm_a, k_norm_a, q_norm_b, k_norm_b, lambda_q1, lambda_k1, lambda_q2, lambda_k2,
           diff_subln, w_out, ffn2_norm, ffn2_w_gate, ffn2_w_up, ffn2_w_down):
    b, s, d = x.shape
    depth = w_in.shape[0]
    n_heads = d // HEAD_DIM
    wa = (n_heads // 2) * HEAD_DIM
    wb = (n_heads // 4) * HEAD_DIM
    wc = d - wa - wb
    widths = (wa, wb, wc)
    assert w_in.shape[2] == 3 * d and s % A_CHUNK == 0 and (b * s) % ROW_TILE == 0

    rb = rel_bias.astype(F32)
    bias_a = _dilated_bias_table(rb[:, :wa // HEAD_DIM])
    bias_bt = _diff_bias_table(rb[:, wa // HEAD_DIM:], s)

    x2 = x.reshape(b * s, d)
    for layer in range(depth):
        x2 = _ffn(x2, ffn1_norm[layer], ffn1_w_gate[layer].astype(BF16),
                  ffn1_w_up[layer].astype(BF16), ffn1_w_down[layer].astype(BF16))

        qa, ka, va, kb, kc, qbt, vbt, qct, vct = _proj(
            x2, mix_norm[layer], w_in[layer], q_norm_a[layer], k_norm_a[layer],
            q_norm_b[layer], k_norm_b[layer], widths)
        seq3 = lambda t: t.reshape(b, s, t.shape[-1])

        out_a = _dilated_attention(seq3(qa), seq3(ka), seq3(va), bias_a)

        lam_init = 0.8 - 0.6 * math.exp(-0.3 * layer)
        lam = (jnp.exp(jnp.sum(lambda_q1[layer].astype(F32) * lambda_k1[layer].astype(F32)))
               - jnp.exp(jnp.sum(lambda_q2[layer].astype(F32) * lambda_k2[layer].astype(F32)))
               + lam_init)
        lam_row = jnp.full((1, ATT_TILE), lam, F32)
        post_gain = jnp.broadcast_to(
            (jnp.tile(diff_subln[layer].astype(F32), LANES // HEAD_DIM) * (1.0 - lam_init))[:, None],
            (LANES, ATT_TILE))
        out_b = _diff_attention(qbt, seq3(kb), vbt, bias_bt, lam_row, post_gain)

        out_c = _stick_attention(qct, seq3(kc), vct)

        x2 = _out_proj(x2, out_a.reshape(b * s, wa), out_b.reshape(b * s, wb),
                       out_c.reshape(b * s, wc), w_out[layer].astype(BF16), widths)

        x2 = _ffn(x2, ffn2_norm[layer], ffn2_w_gate[layer].astype(BF16),
                  ffn2_w_up[layer].astype(BF16), ffn2_w_down[layer].astype(BF16))
    return x2.reshape(b, s, d)
```

```python
import functools
import math

import jax
import jax.numpy as jnp
import numpy as np
from jax import lax
from jax.experimental import pallas as pl
from jax.experimental.pallas import tpu as pltpu

F32 = jnp.float32
BF16 = jnp.bfloat16

HEAD_DIM = 64
DIFF_HALF = HEAD_DIM // 2
N_BUCKETS = 32
MAX_DISTANCE = 2048
DILATED_BRANCHES = ((128, 1), (512, 4), (2048, 16))
Q_BLOCK = 128
EPS = 1e-6
NEG_INF = -1e30
SB_MASK = 1e4
LOG2E = math.log2(math.e)

LANES = 128
A_CHUNK = 2048
A_BLOCKS_PER_ITER = 4
ATT_TILE = 256
ROW_TILE = 512
ONES_ROWS = 16
QUERY_TILES = 2
VMEM_LIMIT = 56 * 1024 * 1024


def _cparams(*sem):
    return pltpu.CompilerParams(dimension_semantics=sem, vmem_limit_bytes=VMEM_LIMIT)


def _rms(x, gain_row):
    ms = jnp.mean(x * x, axis=-1, keepdims=True)
    return x * lax.rsqrt(ms + EPS) * gain_row


def _dot(a, b):
    return jnp.dot(a, b, preferred_element_type=F32)


def _dot_nt(a, b):
    return lax.dot_general(a, b, (((1,), (1,)), ((), ())), preferred_element_type=F32)


def _ffn_kernel(x_ref, g_ref, wg_ref, wu_ref, wd_ref, o_ref):
    x = x_ref[...]
    h = _rms(x, g_ref[...]).astype(BF16)
    gate = _dot(h, wg_ref[...])
    up = _dot(h, wu_ref[...])
    act = (gate * jax.nn.sigmoid(gate) * up).astype(BF16)
    o_ref[...] = x + 0.5 * _dot(act, wd_ref[...])


def _ffn(x2, gain, wg, wu, wd):
    t, d = x2.shape
    dff = wg.shape[1]
    row = pl.BlockSpec((ROW_TILE, d), lambda i: (i, 0))
    full = lambda shape: pl.BlockSpec(shape, lambda i: (0, 0))
    return pl.pallas_call(
        _ffn_kernel,
        out_shape=jax.ShapeDtypeStruct((t, d), F32),
        grid=(t // ROW_TILE,),
        in_specs=[row, full((1, d)), full((d, dff)), full((d, dff)), full((dff, d))],
        out_specs=row,
        compiler_params=_cparams("parallel"),
        name="ffn_half_step",
    )(x2, gain.reshape(1, d), wg, wu, wd)


def _proj_kernel(widths, x_ref, g_ref, wn_ref, wt_ref, gqa_ref, gka_ref, gkb_ref, gqb_ref,
                 g64_ref, g32_ref, qa_ref, ka_ref, va_ref, kb_ref, kc_ref,
                 qbt_ref, vbt_ref, qct_ref, vct_ref):
    wa, wb, wc = widths
    h = _rms(x_ref[...], g_ref[...]).astype(BF16)
    sub = 256

    def group_norm(t, ones_ref, gain_ref, group, scale):
        ss = _dot((t * t).astype(BF16), ones_ref[...])
        return t * lax.rsqrt(ss * (1.0 / group) + EPS) * (gain_ref[...] * scale)

    def group_norm_t(t, ones_ref, gain_ref, group, scale):
        ss = _dot(ones_ref[...], (t * t).astype(BF16))
        return t * lax.rsqrt(ss * (1.0 / group) + EPS) * (gain_ref[...] * scale)

    nat = _dot(h, wn_ref[...])
    col = 0
    for ref, width, post in (
            (qa_ref, wa, lambda t: group_norm(t, g64_ref, gqa_ref, HEAD_DIM, HEAD_DIM ** -0.5)),
            (ka_ref, wa, lambda t: group_norm(t, g64_ref, gka_ref, HEAD_DIM, 1.0)),
            (va_ref, wa, lambda t: t),
            (kb_ref, wb, lambda t: group_norm(t, g32_ref, gkb_ref, DIFF_HALF, 1.0)),
            (kc_ref, wc, lambda t: t)):
        for c in range(0, width, sub):
            ref[:, c:c + sub] = post(nat[:, col + c:col + c + sub]).astype(ref.dtype)
        col += width

    tra = _dot_nt(wt_ref[...], h)
    row = 0
    for ref, width, post in (
            (qbt_ref, wb, lambda t: group_norm_t(t, g32_ref, gqb_ref, DIFF_HALF,
                                                 DIFF_HALF ** -0.5 * LOG2E)),
            (vbt_ref, wb, lambda t: t),
            (qct_ref, wc, lambda t: t * (HEAD_DIM ** -0.5 * LOG2E)),
            (vct_ref, wc, lambda t: t)):
        for r in range(0, width, sub):
            t = post(tra[row + r:row + r + sub]).astype(BF16)
            for j in range(ROW_TILE // ATT_TILE):
                ref[j, r:r + sub, :] = t[:, j * ATT_TILE:(j + 1) * ATT_TILE]
        row += width


def _block_diag_ones(group):
    idx = np.arange(256) // group
    return jnp.asarray(idx[:, None] == idx[None, :], dtype=BF16)


def _proj(x2, gain, w_in, gqa, gka, gqb, gkb, widths):
    t, d = x2.shape
    wa, wb, wc = widths
    assert wa % 256 == 0 and wb % 256 == 0 and wc % 256 == 0
    w = w_in.astype(BF16)
    o = np.cumsum([0, wa, wa, wa, wb, wb, wb, wc, wc, wc])
    sec = lambda i: w[:, o[i]:o[i + 1]]
    w_nat = jnp.concatenate([sec(0), sec(1), sec(2), sec(4), sec(7)], axis=1)
    w_tr = jnp.concatenate([sec(3), sec(5), sec(6), sec(8)], axis=1).T
    row = lambda wd: pl.BlockSpec((ROW_TILE, wd), lambda i: (i, 0))
    full = lambda shape: pl.BlockSpec(shape, lambda i: (0,) * len(shape))
    slab = lambda wd: pl.BlockSpec((ROW_TILE // ATT_TILE, wd, ATT_TILE), lambda i: (i, 0, 0))
    tile256 = lambda g: jnp.tile(g.astype(F32), 256 // g.shape[0])
    out_shape = ([jax.ShapeDtypeStruct((t, wa), F32)] * 3
                 + [jax.ShapeDtypeStruct((t, wb), BF16), jax.ShapeDtypeStruct((t, wc), BF16)]
                 + [jax.ShapeDtypeStruct((t // ATT_TILE, wd, ATT_TILE), BF16)
                    for wd in (wb, wb, wc, wc)])
    return pl.pallas_call(
        functools.partial(_proj_kernel, widths),
        out_shape=out_shape,
        grid=(t // ROW_TILE,),
        in_specs=[row(d), full((1, d)), full(w_nat.shape), full(w_tr.shape)]
                 + [full((1, 256))] * 3 + [full((256, 1))] + [full((256, 256))] * 2,
        out_specs=[row(wa)] * 3 + [row(wb), row(wc)] + [slab(wb), slab(wb), slab(wc), slab(wc)],
        compiler_params=_cparams("parallel"),
        name="norm_in_proj",
    )(x2, gain.reshape(1, d), w_nat, w_tr, tile256(gqa).reshape(1, 256), tile256(gka).reshape(1, 256),
      tile256(gkb).reshape(1, 256), tile256(gqb).reshape(256, 1),
      _block_diag_ones(HEAD_DIM), _block_diag_ones(DIFF_HALF))


def _dilated_kernel(q_ref, kp_ref, kc_ref, vp_ref, vc_ref, bias_ref, o_ref,
                    kk_ref, vv_ref, m_ref, l_ref, acc_ref):
    chunk = pl.program_id(1)
    kk_ref[0:A_CHUNK] = kp_ref[0]
    kk_ref[A_CHUNK:] = kc_ref[0]
    vv_ref[0:A_CHUNK] = vp_ref[0]
    vv_ref[A_CHUNK:] = vc_ref[0]

    lane = lax.broadcasted_iota(jnp.int32, (Q_BLOCK, LANES), 1)
    head0 = lane < HEAD_DIM

    order = sorted(range(len(DILATED_BRANCHES)), key=lambda b: -DILATED_BRANCHES[b][1])
    for bi in order:
        window, dil = DILATED_BRANCHES[bi]
        fresh = bi == order[0]
        nblk = A_CHUNK // (Q_BLOCK * dil)
        shift = int(math.log2(nblk))
        n_iter = dil * nblk
        ds = (lambda start, size, dil=dil:
              pl.ds(start, size, stride=dil) if dil > 1 else pl.ds(start, size))

        def load(idx, bi=bi, dil=dil, nblk=nblk, shift=shift, ds=ds, fresh=fresh):
            rho = idx >> shift
            t = idx & (nblk - 1)
            q_start = rho + t * (Q_BLOCK * dil)
            rows = ds(q_start, Q_BLOCK)
            krows = ds(A_CHUNK + q_start - Q_BLOCK * dil, 2 * Q_BLOCK)
            first = jnp.logical_and(chunk == 0, t == 0).astype(jnp.int32)
            blk = dict(rows=rows, q=q_ref[0, rows, :], k=kk_ref[krows, :], v=vv_ref[krows, :],
                       bias=bias_ref[bi, first, 0])
            if not fresh:
                blk.update(m=jnp.concatenate([m_ref[0, rows, :], m_ref[1, rows, :]], axis=0),
                           l=jnp.concatenate([l_ref[0, rows, :], l_ref[1, rows, :]], axis=0),
                           acc=acc_ref[rows, :])
            return blk

        def compute(b):
            q = b["q"]
            q2 = jnp.concatenate([jnp.where(head0, q, 0.0), jnp.where(head0, 0.0, q)],
                                 axis=0).astype(BF16)
            s = _dot_nt(q2, b["k"].astype(BF16)) + b["bias"]
            row_max = jnp.max(s, axis=-1, keepdims=True)
            if "m" not in b:
                m_new = jnp.broadcast_to(row_max, (2 * Q_BLOCK, LANES))
                p = jnp.exp(s - row_max)
                l_new = jnp.broadcast_to(jnp.sum(p, axis=-1, keepdims=True), (2 * Q_BLOCK, LANES))
                pv = _dot(p.astype(BF16), b["v"].astype(BF16))
                return m_new, l_new, jnp.where(head0, pv[:Q_BLOCK], pv[Q_BLOCK:])
            m_new = jnp.maximum(b["m"], row_max)
            alpha = jnp.exp(b["m"] - m_new)
            p = jnp.exp(s - jnp.concatenate([m_new, m_new], axis=1))
            l_new = alpha * b["l"] + jnp.sum(p, axis=-1, keepdims=True)
            pv = _dot(p.astype(BF16), b["v"].astype(BF16))
            acc_new = jnp.where(head0, alpha[:Q_BLOCK] * b["acc"] + pv[:Q_BLOCK],
                                alpha[Q_BLOCK:] * b["acc"] + pv[Q_BLOCK:])
            return m_new, l_new, acc_new

        def store(b, res):
            m_new, l_new, acc_new = res
            rows = b["rows"]
            acc_ref[rows, :] = acc_new
            m_ref[0, rows, :] = m_new[:Q_BLOCK]
            m_ref[1, rows, :] = m_new[Q_BLOCK:]
            l_ref[0, rows, :] = l_new[:Q_BLOCK]
            l_ref[1, rows, :] = l_new[Q_BLOCK:]

        def some_blocks(i, carry, load=load, compute=compute, store=store,
                        part=n_iter // A_BLOCKS_PER_ITER):
            blocks = [load(i + c * part) for c in range(A_BLOCKS_PER_ITER)]
            results = [compute(b) for b in blocks]
            for b, res in zip(blocks, results):
                store(b, res)
            return carry

        lax.fori_loop(0, n_iter // A_BLOCKS_PER_ITER, some_blocks, 0)

    lane_c = lax.broadcasted_iota(jnp.int32, (A_CHUNK, LANES), 1)
    denom = jnp.where(lane_c < HEAD_DIM, l_ref[0], l_ref[1])
    o_ref[0] = (acc_ref[...] / denom).astype(BF16)


def _dilated_attention(qa, ka, va, bias_a):
    b, s, wa = qa.shape
    n_pairs = wa // LANES
    cur = pl.BlockSpec((1, A_CHUNK, LANES), lambda i, c, p: (i, c, p))
    prev = pl.BlockSpec((1, A_CHUNK, LANES), lambda i, c, p: (i, jnp.maximum(c - 1, 0), p))
    nb = len(DILATED_BRANCHES)
    bias = pl.BlockSpec((nb, 2, 1, 2 * Q_BLOCK, 2 * Q_BLOCK), lambda i, c, p: (0, 0, p, 0, 0))
    return pl.pallas_call(
        _dilated_kernel,
        out_shape=jax.ShapeDtypeStruct((b, s, wa), BF16),
        grid=(b, s // A_CHUNK, n_pairs),
        in_specs=[cur, prev, cur, prev, cur, bias],
        out_specs=cur,
        scratch_shapes=[pltpu.VMEM((2 * A_CHUNK, LANES), F32),
                        pltpu.VMEM((2 * A_CHUNK, LANES), F32),
                        pltpu.VMEM((2, A_CHUNK, LANES), F32),
                        pltpu.VMEM((2, A_CHUNK, LANES), F32),
                        pltpu.VMEM((A_CHUNK, LANES), F32)],
        compiler_params=_cparams("parallel", "parallel", "parallel"),
        name="dilated_attention",
    )(qa, ka, ka, va, va, bias_a)


def _diff_kernel(n_bias, qt_ref, k_ref, vt_ref, bias_ref, lam_ref, gain_ref, o_ref,
                 q4_ref, m_ref, mt_ref, acc_ref, s_ref, p_ref):
    t = ATT_TILE
    row = lax.broadcasted_iota(jnp.int32, (LANES, t), 0)
    ones = jnp.ones((ONES_ROWS, 2 * t), BF16)

    def pipeline(g):
        qi = pl.program_id(2) * QUERY_TILES + g

        def tile_of(step):
            return jnp.clip(qi - step, 0, qi)

        def init():
            qt = qt_ref[0, g]
            for c in range(4):
                sel = jnp.logical_and(row >= c * DIFF_HALF, row < (c + 1) * DIFF_HALF)
                q4_ref[g, :, c * t:(c + 1) * t] = jnp.where(sel, qt, jnp.zeros_like(qt))
            m_ref[g] = jnp.full(m_ref.shape[1:], NEG_INF, F32)
            acc_ref[g] = jnp.zeros(acc_ref.shape[1:], F32)
            p_ref[g] = jnp.zeros(p_ref.shape[1:], BF16)

        def values(i):
            vt = jnp.concatenate([vt_ref[0, tile_of(2 * i)], vt_ref[0, tile_of(2 * i + 1)]], axis=1)
            return _dot(jnp.concatenate([vt, ones], axis=0), p_ref[g])

        def logits(i):
            tile_max = None
            for half in range(2):
                step = 2 * i + half
                k = k_ref[0, pl.ds(pl.multiple_of(tile_of(step) * t, t), t), :]
                d = jnp.where(step > qi, n_bias, jnp.minimum(step, n_bias - 1))
                b0 = bias_ref[d, 0]
                b1 = bias_ref[d, 1]
                s = _dot(k, q4_ref[g]) + jnp.concatenate([b0, b0, b1, b1], axis=1)
                s_ref[g, half * t:(half + 1) * t] = s
                mx = jnp.max(s, axis=0, keepdims=True)
                tile_max = mx if tile_max is None else jnp.maximum(tile_max, mx)
            mt_ref[g] = tile_max

        def sweep(i):
            pv = values(i - 1)
            m_old = m_ref[g]
            m_new = jnp.maximum(m_old, mt_ref[g])
            alpha = jnp.exp2(m_old - m_new)
            m_ref[g] = m_new
            p_ref[g] = jnp.exp2(s_ref[g] - m_new).astype(BF16)
            acc_ref[g] = alpha * (acc_ref[g] + pv)
            logits(i + 1)

        def fill():
            init()
            logits(0)

        def drain(n_iter):
            acc = acc_ref[g] + values(n_iter - 1)
            pv = acc[0:LANES] / acc[LANES:LANES + 1]
            lam = lam_ref[...]
            outs = []
            for h in range(2):
                rows = slice(h * HEAD_DIM, (h + 1) * HEAD_DIM)
                diff = (pv[rows, (2 * h) * t:(2 * h + 1) * t]
                        - lam * pv[rows, (2 * h + 1) * t:(2 * h + 2) * t])
                ms = jnp.mean(diff * diff, axis=0, keepdims=True)
                outs.append(diff * lax.rsqrt(ms + EPS))
            out_t = jnp.concatenate(outs, axis=0) * gain_ref[...]
            o_ref[0, g * t:(g + 1) * t] = out_t.T.astype(BF16)

        return fill, sweep, drain

    stages = [pipeline(g) for g in range(QUERY_TILES)]
    for fill, _, _ in stages:
        fill()
    n_iter = (pl.program_id(2) * QUERY_TILES + QUERY_TILES + 1) // 2

    def body(i, c):
        for _, sweep, _ in stages:
            sweep(i)
        return c

    lax.fori_loop(0, n_iter, body, 0)
    for _, _, drain in stages:
        drain(n_iter)


def _diff_attention(qbt, kb, vbt, bias_bt, lam, post_gain):
    b, s, wb = kb.shape
    n_pairs = wb // LANES
    n_bias = bias_bt.shape[0] - 1
    t = ATT_TILE
    nt = s // t
    g = QUERY_TILES
    qspec = pl.BlockSpec((1, g, LANES, t), lambda i, p, j: (i, j, p, 0))
    kspec = pl.BlockSpec((1, s, LANES), lambda i, p, j: (i, 0, p))
    vspec = pl.BlockSpec((1, nt, LANES, t), lambda i, p, j: (i, 0, p, 0))
    bspec = pl.BlockSpec((n_bias + 1, 2, t, t), lambda i, p, j: (0, p, 0, 0))
    return pl.pallas_call(
        functools.partial(_diff_kernel, n_bias),
        out_shape=jax.ShapeDtypeStruct((b, s, wb), BF16),
        grid=(b, n_pairs, nt // g),
        in_specs=[qspec, kspec, vspec, bspec,
                  pl.BlockSpec((1, t), lambda i, p, j: (0, 0)),
                  pl.BlockSpec((LANES, t), lambda i, p, j: (0, 0))],
        out_specs=pl.BlockSpec((1, g * t, LANES), lambda i, p, j: (i, j, p)),
        scratch_shapes=[pltpu.VMEM((g, LANES, 4 * t), BF16),
                        pltpu.VMEM((g, 1, 4 * t), F32),
                        pltpu.VMEM((g, 1, 4 * t), F32),
                        pltpu.VMEM((g, LANES + ONES_ROWS, 4 * t), F32),
                        pltpu.VMEM((g, 2 * t, 4 * t), F32),
                        pltpu.VMEM((g, 2 * t, 4 * t), BF16)],
        compiler_params=_cparams("parallel", "parallel", "parallel"),
        name="diff_attention",
    )(qbt.reshape(b, nt, wb, t), kb, vbt.reshape(b, nt, wb, t), bias_bt, lam, post_gain)


def _stick_kernel(qt_ref, k_ref, vt_ref, tri_ref, o_ref, q2_ref, carry_ref, scale_ref, acc_ref,
                  z_ref, lw_ref, p_ref):
    t = ATT_TILE
    row = lax.broadcasted_iota(jnp.int32, (LANES, t), 0)

    def pipeline(g):
        qi = pl.program_id(2) * QUERY_TILES + g

        def tile_of(step):
            return jnp.clip(qi - step, 0, qi)

        def init():
            qt = qt_ref[0, g]
            zero = jnp.zeros_like(qt)
            q2_ref[g, :, 0:t] = jnp.where(row < HEAD_DIM, qt, zero)
            q2_ref[g, :, t:] = jnp.where(row < HEAD_DIM, zero, qt)
            carry_ref[g] = jnp.zeros(carry_ref.shape[1:], F32)
            scale_ref[g] = jnp.ones(scale_ref.shape[1:], F32)
            acc_ref[g] = jnp.zeros(acc_ref.shape[1:], F32)
            p_ref[g] = jnp.zeros(p_ref.shape[1:], BF16)

        def logits(i, diagonal=False):
            for half in range(2):
                k = k_ref[0, pl.ds(pl.multiple_of(tile_of(2 * i + half) * t, t), t), :]
                z = _dot(k, q2_ref[g])
                if diagonal and half == 0:
                    key = lax.broadcasted_iota(jnp.int32, (t, t), 0)
                    qry = lax.broadcasted_iota(jnp.int32, (t, t), 1)
                    strict = jnp.concatenate([key < qry, key < qry], axis=1)
                    z = jnp.where(strict, z, -SB_MASK * LOG2E)
                z_ref[g, half] = z

        def log_weights():
            for half in range(2):
                z = z_ref[g, half]
                neg_abs = pltpu.bitcast(pltpu.bitcast(z, jnp.uint32) | jnp.uint32(0x80000000), F32)
                sp = jnp.maximum(z, 0.0) + jnp.log(1.0 + jnp.exp2(neg_abs)) * LOG2E
                w = _dot(tri_ref[...], sp.astype(BF16))
                lw_ref[g, half, 0:t] = z + w[0:t]
                lw_ref[g, half, t:] = w[t:]

        def weights():
            carry = carry_ref[g]
            near_sum = lw_ref[g, 0, t:t + 1]
            p_ref[g, 0:t] = jnp.exp2(lw_ref[g, 0, 0:t]).astype(BF16)
            p_ref[g, t:] = jnp.exp2(lw_ref[g, 1, 0:t] + near_sum).astype(BF16)
            scale_ref[g] = jnp.exp2(carry)
            carry_ref[g] = carry + near_sum + lw_ref[g, 1, t:t + 1]

        def values(i):
            vts = []
            for half in range(2):
                step = 2 * i + half
                valid = jnp.logical_and(step >= 0, step <= qi)
                vt = vt_ref[0, tile_of(step)]
                vts.append(jnp.where(valid, vt, jnp.zeros_like(vt)))
            acc_ref[g] += _dot(jnp.concatenate(vts, axis=1), p_ref[g]) * scale_ref[g]

        def sweep(i):
            values(i - 1)
            weights()
            log_weights()
            logits(i + 2)

        def first_pair():
            init()
            logits(0, diagonal=True)
            log_weights()
            weights()

        def catch_up():
            logits(1)
            log_weights()
            logits(2)

        def drain(n_iter):
            values(n_iter - 1)
            out_t = jnp.concatenate([acc_ref[g, 0:HEAD_DIM, 0:t], acc_ref[g, HEAD_DIM:, t:]], axis=0)
            o_ref[0, g * t:(g + 1) * t] = out_t.T.astype(BF16)

        return first_pair, catch_up, sweep, drain

    stages = [pipeline(g) for g in range(QUERY_TILES)]
    for first_pair, _, _, _ in stages:
        first_pair()
    n_iter = (pl.program_id(2) * QUERY_TILES + QUERY_TILES + 1) // 2

    def live():
        return (jnp.max(jnp.exp2(carry_ref[...])) > 0.0).astype(jnp.int32)

    live_0 = live()

    @pl.when(live_0 > 0)
    def _():
        for _, catch_up, _, _ in stages:
            catch_up()

    def body(state):
        i, _ = state
        for _, _, sweep, _ in stages:
            sweep(i)
        return i + 1, live()

    n_done, _ = lax.while_loop(lambda st: jnp.logical_and(st[0] < n_iter, st[1] > 0), body,
                               (jnp.int32(1), live_0))
    for _, _, _, drain in stages:
        drain(n_done)


def _stick_attention(qct, kc, vct):
    b, s, wc = kc.shape
    n_pairs = wc // LANES
    t = ATT_TILE
    nt = s // t
    g = QUERY_TILES
    idx = np.arange(t)
    tri = np.concatenate([idx[None, :] >= idx[:, None], np.ones((ONES_ROWS, t), bool)], axis=0)
    tri = -jnp.asarray(tri, dtype=BF16)
    return pl.pallas_call(
        _stick_kernel,
        out_shape=jax.ShapeDtypeStruct((b, s, wc), BF16),
        grid=(b, n_pairs, nt // g),
        in_specs=[pl.BlockSpec((1, g, LANES, t), lambda i, p, j: (i, j, p, 0)),
                  pl.BlockSpec((1, s, LANES), lambda i, p, j: (i, 0, p)),
                  pl.BlockSpec((1, nt, LANES, t), lambda i, p, j: (i, 0, p, 0)),
                  pl.BlockSpec((t + ONES_ROWS, t), lambda i, p, j: (0, 0))],
        out_specs=pl.BlockSpec((1, g * t, LANES), lambda i, p, j: (i, j, p)),
        scratch_shapes=[pltpu.VMEM((g, LANES, 2 * t), BF16),
                        pltpu.VMEM((g, 1, 2 * t), F32),
                        pltpu.VMEM((g, 1, 2 * t), F32),
                        pltpu.VMEM((g, LANES, 2 * t), F32),
                        pltpu.VMEM((g, 2, t, 2 * t), F32),
                        pltpu.VMEM((g, 2, t + ONES_ROWS, 2 * t), F32),
                        pltpu.VMEM((g, 2 * t, 2 * t), BF16)],
        compiler_params=_cparams("parallel", "parallel", "parallel"),
        name="stick_breaking_attention",
    )(qct.reshape(b, nt, wc, t), kc, vct.reshape(b, nt, wc, t), tri)


def _out_proj_kernel(widths, x_ref, a_ref, b_ref, c_ref, w_ref, o_ref):
    wa, wb, wc = widths
    y = _dot(a_ref[...], w_ref[0:wa])
    y += _dot(b_ref[...], w_ref[wa:wa + wb])
    y += _dot(c_ref[...], w_ref[wa + wb:wa + wb + wc])
    o_ref[...] = x_ref[...] + y


def _out_proj(x2, oa, ob, oc, w_out, widths):
    t, d = x2.shape
    row = lambda w: pl.BlockSpec((ROW_TILE, w), lambda i: (i, 0))
    return pl.pallas_call(
        functools.partial(_out_proj_kernel, widths),
        out_shape=jax.ShapeDtypeStruct((t, d), F32),
        grid=(t // ROW_TILE,),
        in_specs=[row(d), row(widths[0]), row(widths[1]), row(widths[2]),
                  pl.BlockSpec(w_out.shape, lambda i: (0, 0))],
        out_specs=row(d),
        compiler_params=_cparams("parallel"),
        name="out_proj_residual",
    )(x2, oa, ob, oc, w_out)


def _t5_bucket(dist):
    dist = jnp.maximum(dist, 0)
    max_exact = N_BUCKETS // 2
    d_f = jnp.maximum(dist, 1).astype(F32)
    large = max_exact + (jnp.log(d_f / max_exact) / math.log(MAX_DISTANCE / max_exact)
                         * (N_BUCKETS - max_exact)).astype(jnp.int32)
    large = jnp.minimum(large, N_BUCKETS - 1)
    return jnp.where(dist < max_exact, dist, large)


def _bias_of_distance(bias, dist):
    bucket = _t5_bucket(dist)[None]
    out = jnp.zeros((bias.shape[1],) + dist.shape, F32)
    for b in range(N_BUCKETS):
        out = jnp.where(bucket == b, bias[b].reshape((-1,) + (1,) * dist.ndim), out)
    return out


def _dilated_bias_table(bias_a):
    n_heads = bias_a.shape[1]
    i = jnp.arange(Q_BLOCK, dtype=jnp.int32)[:, None]
    j = jnp.arange(2 * Q_BLOCK, dtype=jnp.int32)[None, :]
    tables = []
    for window, dil in DILATED_BRANCHES:
        n = window // dil
        off = i + n - j
        band = (off >= 0) & (off <= n)
        bias = _bias_of_distance(bias_a, off * dil)
        variants = [jnp.where(valid[None], bias, NEG_INF) for valid in (band, band & (j >= n))]
        tables.append(jnp.stack(variants))
    table = jnp.stack(tables)
    return table.reshape(len(DILATED_BRANCHES), 2, n_heads // 2, 2 * Q_BLOCK, 2 * Q_BLOCK)


def _diff_bias_table(bias_b, seq):
    t = ATT_TILE
    n_bias = min(seq // t, MAX_DISTANCE // t + 2)
    key = jnp.arange(t, dtype=jnp.int32)[None, :, None]
    qry = jnp.arange(t, dtype=jnp.int32)[None, None, :]
    dist = jnp.arange(n_bias + 1, dtype=jnp.int32)[:, None, None] * t + qry - key
    valid = (dist >= 0) & (jnp.arange(n_bias + 1)[:, None, None] < n_bias)
    tiles = jnp.where(valid[None], _bias_of_distance(bias_b, dist) * LOG2E, NEG_INF)
    return jnp.swapaxes(tiles, 0, 1)


def kernel(x, rel_bias, ffn1_norm, ffn1_w_gate, ffn1_w_up, ffn1_w_down, mix_norm, w_in,
           q_norm_a, k_norm_a, q_norm_b, k_norm_b, lambda_q1, lambda_k1, lambda_q2, lambda_k2,
           diff_subln, w_out, ffn2_norm, ffn2_w_gate, ffn2_w_up, ffn2_w_down):
    b, s, d = x.shape
    depth = w_in.shape[0]
    n_heads = d // HEAD_DIM
    wa = (n_heads // 2) * HEAD_DIM
    wb = (n_heads // 4) * HEAD_DIM
    wc = d - wa - wb
    widths = (wa, wb, wc)
    assert w_in.shape[2] == 3 * d and s % A_CHUNK == 0 and (b * s) % ROW_TILE == 0

    rb = rel_bias.astype(F32)
    bias_a = _dilated_bias_table(rb[:, :wa // HEAD_DIM])
    bias_bt = _diff_bias_table(rb[:, wa // HEAD_DIM:], s)

    x2 = x.reshape(b * s, d)
    for layer in range(depth):
        x2 = _ffn(x2, ffn1_norm[layer], ffn1_w_gate[layer].astype(BF16),
                  ffn1_w_up[layer].astype(BF16), ffn1_w_down[layer].astype(BF16))

        qa, ka, va, kb, kc, qbt, vbt, qct, vct = _proj(
            x2, mix_norm[layer], w_in[layer], q_norm_a[layer], k_norm_a[layer],
            q_norm_b[layer], k_norm_b[layer], widths)
        seq3 = lambda t: t.reshape(b, s, t.shape[-1])

        out_a = _dilated_attention(seq3(qa), seq3(ka), seq3(va), bias_a)

        lam_init = 0.8 - 0.6 * math.exp(-0.3 * layer)
        lam = (jnp.exp(jnp.sum(lambda_q1[layer].astype(F32) * lambda_k1[layer].astype(F32)))
               - jnp.exp(jnp.sum(lambda_q2[layer].astype(F32) * lambda_k2[layer].astype(F32)))
               + lam_init)
        lam_row = jnp.full((1, ATT_TILE), lam, F32)
        post_gain = jnp.broadcast_to(
            (jnp.tile(diff_subln[layer].astype(F32), LANES // HEAD_DIM) * (1.0 - lam_init))[:, None],
            (LANES, ATT_TILE))
        out_b = _diff_attention(qbt, seq3(kb), vbt, bias_bt, lam_row, post_gain)

        out_c = _stick_attention(qct, seq3(kc), vct)

        x2 = _out_proj(x2, out_a.reshape(b * s, wa), out_b.reshape(b * s, wb),
                       out_c.reshape(b * s, wc), w_out[layer].astype(BF16), widths)

        x2 = _ffn(x2, ffn2_norm[layer], ffn2_w_gate[layer].astype(BF16),
                  ffn2_w_up[layer].astype(BF16), ffn2_w_down[layer].astype(BF16))
    return x2.reshape(b, s, d)
```

```python
import functools
import math

import jax
import jax.numpy as jnp
import numpy as np
from jax import lax
from jax.experimental import pallas as pl
from jax.experimental.pallas import tpu as pltpu

F32 = jnp.float32
BF16 = jnp.bfloat16

HEAD_DIM = 64
DIFF_HALF = HEAD_DIM // 2
N_BUCKETS = 32
MAX_DISTANCE = 2048
DILATED_BRANCHES = ((128, 1), (512, 4), (2048, 16))
Q_BLOCK = 128
EPS = 1e-6
NEG_INF = -1e30
SB_MASK = 1e4
LOG2E = math.log2(math.e)

LANES = 128
A_CHUNK = 2048
A_BLOCKS_PER_ITER = 4
ATT_TILE = 256
ROW_TILE = 512
ONES_ROWS = 16
QUERY_TILES = 2
VMEM_LIMIT = 56 * 1024 * 1024


def _cparams(*sem):
    return pltpu.CompilerParams(dimension_semantics=sem, vmem_limit_bytes=VMEM_LIMIT)


def _rms(x, gain_row):
    ms = jnp.mean(x * x, axis=-1, keepdims=True)
    return x * lax.rsqrt(ms + EPS) * gain_row


def _dot(a, b):
    return jnp.dot(a, b, preferred_element_type=F32)


def _dot_nt(a, b):
    return lax.dot_general(a, b, (((1,), (1,)), ((), ())), preferred_element_type=F32)


def _ffn_kernel(x_ref, g_ref, wg_ref, wu_ref, wd_ref, o_ref):
    x = x_ref[...]
    h = _rms(x, g_ref[...]).astype(BF16)
    gate = _dot(h, wg_ref[...])
    up = _dot(h, wu_ref[...])
    act = (gate * jax.nn.sigmoid(gate) * up).astype(BF16)
    o_ref[...] = x + 0.5 * _dot(act, wd_ref[...])


def _ffn(x2, gain, wg, wu, wd):
    t, d = x2.shape
    dff = wg.shape[1]
    row = pl.BlockSpec((ROW_TILE, d), lambda i: (i, 0))
    full = lambda shape: pl.BlockSpec(shape, lambda i: (0, 0))
    return pl.pallas_call(
        _ffn_kernel,
        out_shape=jax.ShapeDtypeStruct((t, d), F32),
        grid=(t // ROW_TILE,),
        in_specs=[row, full((1, d)), full((d, dff)), full((d, dff)), full((dff, d))],
        out_specs=row,
        compiler_params=_cparams("parallel"),
        name="ffn_half_step",
    )(x2, gain.reshape(1, d), wg, wu, wd)


def _proj_kernel(widths, x_ref, g_ref, wn_ref, wt_ref, gqa_ref, gka_ref, gkb_ref, gqb_ref,
                 g64_ref, g32_ref, qa_ref, ka_ref, va_ref, kb_ref, kc_ref,
                 qbt_ref, vbt_ref, qct_ref, vct_ref):
    wa, wb, wc = widths
    h = _rms(x_ref[...], g_ref[...]).astype(BF16)
    sub = 256

    def group_norm(t, ones_ref, gain_ref, group, scale):
        ss = _dot((t * t).astype(BF16), ones_ref[...])
        return t * lax.rsqrt(ss * (1.0 / group) + EPS) * (gain_ref[...] * scale)

    def group_norm_t(t, ones_ref, gain_ref, group, scale):
        ss = _dot(ones_ref[...], (t * t).astype(BF16))
        return t * lax.rsqrt(ss * (1.0 / group) + EPS) * (gain_ref[...] * scale)

    nat = _dot(h, wn_ref[...])
    col = 0
    for ref, width, post in (
            (qa_ref, wa, lambda t: group_norm(t, g64_ref, gqa_ref, HEAD_DIM,
                                              HEAD_DIM ** -0.5 * LOG2E)),
            (ka_ref, wa, lambda t: group_norm(t, g64_ref, gka_ref, HEAD_DIM, 1.0)),
            (va_ref, wa, lambda t: t),
            (kb_ref, wb, lambda t: group_norm(t, g32_ref, gkb_ref, DIFF_HALF, 1.0)),
            (kc_ref, wc, lambda t: t)):
        for c in range(0, width, sub):
            ref[:, c:c + sub] = post(nat[:, col + c:col + c + sub]).astype(ref.dtype)
        col += width

    tra = _dot_nt(wt_ref[...], h)
    row = 0
    for ref, width, post in (
            (qbt_ref, wb, lambda t: group_norm_t(t, g32_ref, gqb_ref, DIFF_HALF,
                                                 DIFF_HALF ** -0.5 * LOG2E)),
            (vbt_ref, wb, lambda t: t),
            (qct_ref, wc, lambda t: t * (HEAD_DIM ** -0.5 * LOG2E)),
            (vct_ref, wc, lambda t: t)):
        for r in range(0, width, sub):
            t = post(tra[row + r:row + r + sub]).astype(BF16)
            for j in range(ROW_TILE // ATT_TILE):
                ref[j, r:r + sub, :] = t[:, j * ATT_TILE:(j + 1) * ATT_TILE]
        row += width


def _block_diag_ones(group):
    idx = np.arange(256) // group
    return jnp.asarray(idx[:, None] == idx[None, :], dtype=BF16)


def _proj(x2, gain, w_in, gqa, gka, gqb, gkb, widths):
    t, d = x2.shape
    wa, wb, wc = widths
    assert wa % 256 == 0 and wb % 256 == 0 and wc % 256 == 0
    w = w_in.astype(BF16)
    o = np.cumsum([0, wa, wa, wa, wb, wb, wb, wc, wc, wc])
    sec = lambda i: w[:, o[i]:o[i + 1]]
    w_nat = jnp.concatenate([sec(0), sec(1), sec(2), sec(4), sec(7)], axis=1)
    w_tr = jnp.concatenate([sec(3), sec(5), sec(6), sec(8)], axis=1).T
    row = lambda wd: pl.BlockSpec((ROW_TILE, wd), lambda i: (i, 0))
    full = lambda shape: pl.BlockSpec(shape, lambda i: (0,) * len(shape))
    slab = lambda wd: pl.BlockSpec((ROW_TILE // ATT_TILE, wd, ATT_TILE), lambda i: (i, 0, 0))
    tile256 = lambda g: jnp.tile(g.astype(F32), 256 // g.shape[0])
    out_shape = ([jax.ShapeDtypeStruct((t, wa), F32)] * 3
                 + [jax.ShapeDtypeStruct((t, wb), BF16), jax.ShapeDtypeStruct((t, wc), BF16)]
                 + [jax.ShapeDtypeStruct((t // ATT_TILE, wd, ATT_TILE), BF16)
                    for wd in (wb, wb, wc, wc)])
    return pl.pallas_call(
        functools.partial(_proj_kernel, widths),
        out_shape=out_shape,
        grid=(t // ROW_TILE,),
        in_specs=[row(d), full((1, d)), full(w_nat.shape), full(w_tr.shape)]
                 + [full((1, 256))] * 3 + [full((256, 1))] + [full((256, 256))] * 2,
        out_specs=[row(wa)] * 3 + [row(wb), row(wc)] + [slab(wb), slab(wb), slab(wc), slab(wc)],
        compiler_params=_cparams("parallel"),
        name="norm_in_proj",
    )(x2, gain.reshape(1, d), w_nat, w_tr, tile256(gqa).reshape(1, 256), tile256(gka).reshape(1, 256),
      tile256(gkb).reshape(1, 256), tile256(gqb).reshape(256, 1),
      _block_diag_ones(HEAD_DIM), _block_diag_ones(DIFF_HALF))


def _dilated_kernel(q_ref, kp_ref, kc_ref, vp_ref, vc_ref, bias_ref, o_ref,
                    kk_ref, vv_ref, m_ref, l_ref, acc_ref):
    chunk = pl.program_id(1)
    kk_ref[0:A_CHUNK] = kp_ref[0]
    kk_ref[A_CHUNK:] = kc_ref[0]
    vv_ref[0:A_CHUNK] = vp_ref[0]
    vv_ref[A_CHUNK:] = vc_ref[0]

    lane = lax.broadcasted_iota(jnp.int32, (Q_BLOCK, LANES), 1)
    head0 = lane < HEAD_DIM

    order = sorted(range(len(DILATED_BRANCHES)), key=lambda b: -DILATED_BRANCHES[b][1])
    for bi in order:
        window, dil = DILATED_BRANCHES[bi]
        fresh = bi == order[0]
        nblk = A_CHUNK // (Q_BLOCK * dil)
        shift = int(math.log2(nblk))
        n_iter = dil * nblk
        ds = (lambda start, size, dil=dil:
              pl.ds(start, size, stride=dil) if dil > 1 else pl.ds(start, size))

        def load(idx, bi=bi, dil=dil, nblk=nblk, shift=shift, ds=ds, fresh=fresh):
            rho = idx >> shift
            t = idx & (nblk - 1)
            q_start = rho + t * (Q_BLOCK * dil)
            rows = ds(q_start, Q_BLOCK)
            krows = ds(A_CHUNK + q_start - Q_BLOCK * dil, 2 * Q_BLOCK)
            first = jnp.logical_and(chunk == 0, t == 0).astype(jnp.int32)
            blk = dict(rows=rows, q=q_ref[0, rows, :], k=kk_ref[krows, :], v=vv_ref[krows, :],
                       bias=bias_ref[bi, first, 0])
            if not fresh:
                blk.update(m=jnp.concatenate([m_ref[0, rows, :], m_ref[1, rows, :]], axis=0),
                           l=jnp.concatenate([l_ref[0, rows, :], l_ref[1, rows, :]], axis=0),
                           acc=acc_ref[rows, :])
            return blk

        def compute(b):
            q = b["q"]
            q2 = jnp.concatenate([jnp.where(head0, q, 0.0), jnp.where(head0, 0.0, q)],
                                 axis=0).astype(BF16)
            s = _dot_nt(q2, b["k"].astype(BF16)) + b["bias"]
            row_max = jnp.max(s, axis=-1, keepdims=True)
            if "m" not in b:
                m_new = jnp.broadcast_to(row_max, (2 * Q_BLOCK, LANES))
                p = jnp.exp2(s - row_max)
                l_new = jnp.broadcast_to(jnp.sum(p, axis=-1, keepdims=True), (2 * Q_BLOCK, LANES))
                pv = _dot(p.astype(BF16), b["v"].astype(BF16))
                return m_new, l_new, jnp.where(head0, pv[:Q_BLOCK], pv[Q_BLOCK:])
            m_new = jnp.maximum(b["m"], row_max)
            alpha = jnp.exp2(b["m"] - m_new)
            p = jnp.exp2(s - jnp.concatenate([m_new, m_new], axis=1))
            l_new = alpha * b["l"] + jnp.sum(p, axis=-1, keepdims=True)
            pv = _dot(p.astype(BF16), b["v"].astype(BF16))
            acc_new = jnp.where(head0, alpha[:Q_BLOCK] * b["acc"] + pv[:Q_BLOCK],
                                alpha[Q_BLOCK:] * b["acc"] + pv[Q_BLOCK:])
            return m_new, l_new, acc_new

        def store(b, res):
            m_new, l_new, acc_new = res
            rows = b["rows"]
            acc_ref[rows, :] = acc_new
            m_ref[0, rows, :] = m_new[:Q_BLOCK]
            m_ref[1, rows, :] = m_new[Q_BLOCK:]
            l_ref[0, rows, :] = l_new[:Q_BLOCK]
            l_ref[1, rows, :] = l_new[Q_BLOCK:]

        def some_blocks(i, carry, load=load, compute=compute, store=store,
                        part=n_iter // A_BLOCKS_PER_ITER):
            blocks = [load(i + c * part) for c in range(A_BLOCKS_PER_ITER)]
            results = [compute(b) for b in blocks]
            for b, res in zip(blocks, results):
                store(b, res)
            return carry

        lax.fori_loop(0, n_iter // A_BLOCKS_PER_ITER, some_blocks, 0)

    lane_c = lax.broadcasted_iota(jnp.int32, (A_CHUNK, LANES), 1)
    denom = jnp.where(lane_c < HEAD_DIM, l_ref[0], l_ref[1])
    o_ref[0] = (acc_ref[...] / denom).astype(BF16)


def _dilated_attention(qa, ka, va, bias_a):
    b, s, wa = qa.shape
    n_pairs = wa // LANES
    cur = pl.BlockSpec((1, A_CHUNK, LANES), lambda i, c, p: (i, c, p))
    prev = pl.BlockSpec((1, A_CHUNK, LANES), lambda i, c, p: (i, jnp.maximum(c - 1, 0), p))
    nb = len(DILATED_BRANCHES)
    bias = pl.BlockSpec((nb, 2, 1, 2 * Q_BLOCK, 2 * Q_BLOCK), lambda i, c, p: (0, 0, p, 0, 0))
    return pl.pallas_call(
        _dilated_kernel,
        out_shape=jax.ShapeDtypeStruct((b, s, wa), BF16),
        grid=(b, s // A_CHUNK, n_pairs),
        in_specs=[cur, prev, cur, prev, cur, bias],
        out_specs=cur,
        scratch_shapes=[pltpu.VMEM((2 * A_CHUNK, LANES), F32),
                        pltpu.VMEM((2 * A_CHUNK, LANES), F32),
                        pltpu.VMEM((2, A_CHUNK, LANES), F32),
                        pltpu.VMEM((2, A_CHUNK, LANES), F32),
                        pltpu.VMEM((A_CHUNK, LANES), F32)],
        compiler_params=_cparams("parallel", "parallel", "parallel"),
        name="dilated_attention",
    )(qa, ka, ka, va, va, bias_a)


def _diff_kernel(n_bias, qt_ref, k_ref, vt_ref, bias_ref, lam_ref, gain_ref, o_ref,
                 q4_ref, m_ref, mt_ref, acc_ref, s_ref, p_ref):
    t = ATT_TILE
    row = lax.broadcasted_iota(jnp.int32, (LANES, t), 0)
    ones = jnp.ones((ONES_ROWS, 2 * t), BF16)

    def pipeline(g):
        qi = pl.program_id(2) * QUERY_TILES + g

        def tile_of(step):
            return jnp.clip(qi - step, 0, qi)

        def init():
            qt = qt_ref[0, g]
            for c in range(4):
                sel = jnp.logical_and(row >= c * DIFF_HALF, row < (c + 1) * DIFF_HALF)
                q4_ref[g, :, c * t:(c + 1) * t] = jnp.where(sel, qt, jnp.zeros_like(qt))
            m_ref[g] = jnp.full(m_ref.shape[1:], NEG_INF, F32)
            acc_ref[g] = jnp.zeros(acc_ref.shape[1:], F32)
            p_ref[g] = jnp.zeros(p_ref.shape[1:], BF16)

        def values(i):
            vt = jnp.concatenate([vt_ref[0, tile_of(2 * i)], vt_ref[0, tile_of(2 * i + 1)]], axis=1)
            return _dot(jnp.concatenate([vt, ones], axis=0), p_ref[g])

        def logits(i):
            tile_max = None
            for half in range(2):
                step = 2 * i + half
                k = k_ref[0, pl.ds(pl.multiple_of(tile_of(step) * t, t), t), :]
                d = jnp.where(step > qi, n_bias, jnp.minimum(step, n_bias - 1))
                b0 = bias_ref[d, 0]
                b1 = bias_ref[d, 1]
                s = _dot(k, q4_ref[g]) + jnp.concatenate([b0, b0, b1, b1], axis=1)
                s_ref[g, half * t:(half + 1) * t] = s
                mx = jnp.max(s, axis=0, keepdims=True)
                tile_max = mx if tile_max is None else jnp.maximum(tile_max, mx)
            mt_ref[g] = tile_max

        def sweep(i):
            pv = values(i - 1)
            m_old = m_ref[g]
            m_new = jnp.maximum(m_old, mt_ref[g])
            alpha = jnp.exp2(m_old - m_new)
            m_ref[g] = m_new
            p_ref[g] = jnp.exp2(s_ref[g] - m_new).astype(BF16)
            acc_ref[g] = alpha * (acc_ref[g] + pv)
            logits(i + 1)

        def fill():
            init()
            logits(0)

        def drain(n_iter):
            acc = acc_ref[g] + values(n_iter - 1)
            pv = acc[0:LANES] / acc[LANES:LANES + 1]
            lam = lam_ref[...]
            outs = []
            for h in range(2):
                rows = slice(h * HEAD_DIM, (h + 1) * HEAD_DIM)
                diff = (pv[rows, (2 * h) * t:(2 * h + 1) * t]
                        - lam * pv[rows, (2 * h + 1) * t:(2 * h + 2) * t])
                ms = jnp.mean(diff * diff, axis=0, keepdims=True)
                outs.append(diff * lax.rsqrt(ms + EPS))
            out_t = jnp.concatenate(outs, axis=0) * gain_ref[...]
            o_ref[0, g * t:(g + 1) * t] = out_t.T.astype(BF16)

        return fill, sweep, drain

    stages = [pipeline(g) for g in range(QUERY_TILES)]
    for fill, _, _ in stages:
        fill()
    n_iter = (pl.program_id(2) * QUERY_TILES + QUERY_TILES + 1) // 2

    def body(i, c):
        for _, sweep, _ in stages:
            sweep(i)
        return c

    lax.fori_loop(0, n_iter, body, 0)
    for _, _, drain in stages:
        drain(n_iter)


def _diff_attention(qbt, kb, vbt, bias_bt, lam, post_gain):
    b, s, wb = kb.shape
    n_pairs = wb // LANES
    n_bias = bias_bt.shape[0] - 1
    t = ATT_TILE
    nt = s // t
    g = QUERY_TILES
    qspec = pl.BlockSpec((1, g, LANES, t), lambda i, p, j: (i, j, p, 0))
    kspec = pl.BlockSpec((1, s, LANES), lambda i, p, j: (i, 0, p))
    vspec = pl.BlockSpec((1, nt, LANES, t), lambda i, p, j: (i, 0, p, 0))
    bspec = pl.BlockSpec((n_bias + 1, 2, t, t), lambda i, p, j: (0, p, 0, 0))
    return pl.pallas_call(
        functools.partial(_diff_kernel, n_bias),
        out_shape=jax.ShapeDtypeStruct((b, s, wb), BF16),
        grid=(b, n_pairs, nt // g),
        in_specs=[qspec, kspec, vspec, bspec,
                  pl.BlockSpec((1, t), lambda i, p, j: (0, 0)),
                  pl.BlockSpec((LANES, t), lambda i, p, j: (0, 0))],
        out_specs=pl.BlockSpec((1, g * t, LANES), lambda i, p, j: (i, j, p)),
        scratch_shapes=[pltpu.VMEM((g, LANES, 4 * t), BF16),
                        pltpu.VMEM((g, 1, 4 * t), F32),
                        pltpu.VMEM((g, 1, 4 * t), F32),
                        pltpu.VMEM((g, LANES + ONES_ROWS, 4 * t), F32),
                        pltpu.VMEM((g, 2 * t, 4 * t), F32),
                        pltpu.VMEM((g, 2 * t, 4 * t), BF16)],
        compiler_params=_cparams("parallel", "parallel", "parallel"),
        name="diff_attention",
    )(qbt.reshape(b, nt, wb, t), kb, vbt.reshape(b, nt, wb, t), bias_bt, lam, post_gain)


def _stick_kernel(qt_ref, k_ref, vt_ref, tri_ref, o_ref, q2_ref, carry_ref, scale_ref, acc_ref,
                  z_ref, lw_ref, p_ref):
    t = ATT_TILE
    row = lax.broadcasted_iota(jnp.int32, (LANES, t), 0)

    def pipeline(g):
        qi = pl.program_id(2) * QUERY_TILES + g

        def tile_of(step):
            return jnp.clip(qi - step, 0, qi)

        def init():
            qt = qt_ref[0, g]
            zero = jnp.zeros_like(qt)
            q2_ref[g, :, 0:t] = jnp.where(row < HEAD_DIM, qt, zero)
            q2_ref[g, :, t:] = jnp.where(row < HEAD_DIM, zero, qt)
            carry_ref[g] = jnp.zeros(carry_ref.shape[1:], F32)
            scale_ref[g] = jnp.ones(scale_ref.shape[1:], F32)
            acc_ref[g] = jnp.zeros(acc_ref.shape[1:], F32)
            p_ref[g] = jnp.zeros(p_ref.shape[1:], BF16)

        def logits(i, diagonal=False):
            for half in range(2):
                k = k_ref[0, pl.ds(pl.multiple_of(tile_of(2 * i + half) * t, t), t), :]
                z = _dot(k, q2_ref[g])
                if diagonal and half == 0:
                    key = lax.broadcasted_iota(jnp.int32, (t, t), 0)
                    qry = lax.broadcasted_iota(jnp.int32, (t, t), 1)
                    strict = jnp.concatenate([key < qry, key < qry], axis=1)
                    z = jnp.where(strict, z, -SB_MASK * LOG2E)
                z_ref[g, half] = z

        def log_weights():
            for half in range(2):
                z = z_ref[g, half]
                neg_abs = pltpu.bitcast(pltpu.bitcast(z, jnp.uint32) | jnp.uint32(0x80000000), F32)
                sp = jnp.maximum(z, 0.0) + jnp.log(1.0 + jnp.exp2(neg_abs)) * LOG2E
                w = _dot(tri_ref[...], sp.astype(BF16))
                lw_ref[g, half, 0:t] = z + w[0:t]
                lw_ref[g, half, t:] = w[t:]

        def weights():
            carry = carry_ref[g]
            near_sum = lw_ref[g, 0, t:t + 1]
            p_ref[g, 0:t] = jnp.exp2(lw_ref[g, 0, 0:t]).astype(BF16)
            p_ref[g, t:] = jnp.exp2(lw_ref[g, 1, 0:t] + near_sum).astype(BF16)
            scale_ref[g] = jnp.exp2(carry)
            carry_ref[g] = carry + near_sum + lw_ref[g, 1, t:t + 1]

        def values(i):
            vts = []
            for half in range(2):
                step = 2 * i + half
                valid = jnp.logical_and(step >= 0, step <= qi)
                vt = vt_ref[0, tile_of(step)]
                vts.append(jnp.where(valid, vt, jnp.zeros_like(vt)))
            acc_ref[g] += _dot(jnp.concatenate(vts, axis=1), p_ref[g]) * scale_ref[g]

        def sweep(i):
            values(i - 1)
            weights()
            log_weights()
            logits(i + 2)

        def first_pair():
            init()
            logits(0, diagonal=True)
            log_weights()
            weights()

        def catch_up():
            logits(1)
            log_weights()
            logits(2)

        def drain(n_iter):
            values(n_iter - 1)
            out_t = jnp.concatenate([acc_ref[g, 0:HEAD_DIM, 0:t], acc_ref[g, HEAD_DIM:, t:]], axis=0)
            o_ref[0, g * t:(g + 1) * t] = out_t.T.astype(BF16)

        return first_pair, catch_up, sweep, drain

    stages = [pipeline(g) for g in range(QUERY_TILES)]
    for first_pair, _, _, _ in stages:
        first_pair()
    n_iter = (pl.program_id(2) * QUERY_TILES + QUERY_TILES + 1) // 2

    def live():
        return (jnp.max(jnp.exp2(carry_ref[...])) > 0.0).astype(jnp.int32)

    live_0 = live()

    @pl.when(live_0 > 0)
    def _():
        for _, catch_up, _, _ in stages:
            catch_up()

    def body(state):
        i, _ = state
        for _, _, sweep, _ in stages:
            sweep(i)
        return i + 1, live()

    n_done, _ = lax.while_loop(lambda st: jnp.logical_and(st[0] < n_iter, st[1] > 0), body,
                               (jnp.int32(1), live_0))
    for _, _, _, drain in stages:
        drain(n_done)


def _stick_attention(qct, kc, vct):
    b, s, wc = kc.shape
    n_pairs = wc // LANES
    t = ATT_TILE
    nt = s // t
    g = QUERY_TILES
    idx = np.arange(t)
    tri = np.concatenate([idx[None, :] >= idx[:, None], np.ones((ONES_ROWS, t), bool)], axis=0)
    tri = -jnp.asarray(tri, dtype=BF16)
    return pl.pallas_call(
        _stick_kernel,
        out_shape=jax.ShapeDtypeStruct((b, s, wc), BF16),
        grid=(b, n_pairs, nt // g),
        in_specs=[pl.BlockSpec((1, g, LANES, t), lambda i, p, j: (i, j, p, 0)),
                  pl.BlockSpec((1, s, LANES), lambda i, p, j: (i, 0, p)),
                  pl.BlockSpec((1, nt, LANES, t), lambda i, p, j: (i, 0, p, 0)),
                  pl.BlockSpec((t + ONES_ROWS, t), lambda i, p, j: (0, 0))],
        out_specs=pl.BlockSpec((1, g * t, LANES), lambda i, p, j: (i, j, p)),
        scratch_shapes=[pltpu.VMEM((g, LANES, 2 * t), BF16),
                        pltpu.VMEM((g, 1, 2 * t), F32),
                        pltpu.VMEM((g, 1, 2 * t), F32),
                        pltpu.VMEM((g, LANES, 2 * t), F32),
                        pltpu.VMEM((g, 2, t, 2 * t), F32),
                        pltpu.VMEM((g, 2, t + ONES_ROWS, 2 * t), F32),
                        pltpu.VMEM((g, 2 * t, 2 * t), BF16)],
        compiler_params=_cparams("parallel", "parallel", "parallel"),
        name="stick_breaking_attention",
    )(qct.reshape(b, nt, wc, t), kc, vct.reshape(b, nt, wc, t), tri)


def _out_proj_kernel(widths, n_ffn, x_ref, a_ref, b_ref, c_ref, w_ref, *rest):
    o_ref = rest[4 * n_ffn]
    wa, wb, wc = widths
    y = _dot(a_ref[...], w_ref[0:wa])
    y += _dot(b_ref[...], w_ref[wa:wa + wb])
    y += _dot(c_ref[...], w_ref[wa + wb:wa + wb + wc])
    x = x_ref[...] + y
    for f in range(n_ffn):
        g_ref, wg_ref, wu_ref, wd_ref = rest[4 * f:4 * f + 4]
        h = _rms(x, g_ref[...]).astype(BF16)
        gate = _dot(h, wg_ref[...])
        up = _dot(h, wu_ref[...])
        act = (gate * jax.nn.sigmoid(gate) * up).astype(BF16)
        x = x + 0.5 * _dot(act, wd_ref[...])
    o_ref[...] = x


def _out_proj(x2, oa, ob, oc, w_out, widths, ffns):
    t, d = x2.shape
    row = lambda w: pl.BlockSpec((ROW_TILE, w), lambda i: (i, 0))
    full = lambda a: pl.BlockSpec(a.shape, lambda i: (0, 0))
    ffn_args = []
    for gain, wg, wu, wd in ffns:
        ffn_args += [gain.reshape(1, d), wg, wu, wd]
    return pl.pallas_call(
        functools.partial(_out_proj_kernel, widths, len(ffns)),
        out_shape=jax.ShapeDtypeStruct((t, d), F32),
        grid=(t // ROW_TILE,),
        in_specs=[row(d), row(widths[0]), row(widths[1]), row(widths[2]), full(w_out)]
                 + [full(a) for a in ffn_args],
        out_specs=row(d),
        compiler_params=_cparams("parallel"),
        name="out_proj_residual_ffn",
    )(x2, oa, ob, oc, w_out, *ffn_args)


def _t5_bucket(dist):
    dist = jnp.maximum(dist, 0)
    max_exact = N_BUCKETS // 2
    d_f = jnp.maximum(dist, 1).astype(F32)
    large = max_exact + (jnp.log(d_f / max_exact) / math.log(MAX_DISTANCE / max_exact)
                         * (N_BUCKETS - max_exact)).astype(jnp.int32)
    large = jnp.minimum(large, N_BUCKETS - 1)
    return jnp.where(dist < max_exact, dist, large)


def _bias_of_distance(bias, dist):
    bucket = _t5_bucket(dist)[None]
    out = jnp.zeros((bias.shape[1],) + dist.shape, F32)
    for b in range(N_BUCKETS):
        out = jnp.where(bucket == b, bias[b].reshape((-1,) + (1,) * dist.ndim), out)
    return out


def _dilated_bias_table(bias_a):
    n_heads = bias_a.shape[1]
    i = jnp.arange(Q_BLOCK, dtype=jnp.int32)[:, None]
    j = jnp.arange(2 * Q_BLOCK, dtype=jnp.int32)[None, :]
    tables = []
    for window, dil in DILATED_BRANCHES:
        n = window // dil
        off = i + n - j
        band = (off >= 0) & (off <= n)
        bias = _bias_of_distance(bias_a, off * dil)
        variants = [jnp.where(valid[None], bias * LOG2E, NEG_INF)
                    for valid in (band, band & (j >= n))]
        tables.append(jnp.stack(variants))
    table = jnp.stack(tables)
    return table.reshape(len(DILATED_BRANCHES), 2, n_heads // 2, 2 * Q_BLOCK, 2 * Q_BLOCK)


def _diff_bias_table(bias_b, seq):
    t = ATT_TILE
    n_bias = min(seq // t, MAX_DISTANCE // t + 2)
    key = jnp.arange(t, dtype=jnp.int32)[None, :, None]
    qry = jnp.arange(t, dtype=jnp.int32)[None, None, :]
    dist = jnp.arange(n_bias + 1, dtype=jnp.int32)[:, None, None] * t + qry - key
    valid = (dist >= 0) & (jnp.arange(n_bias + 1)[:, None, None] < n_bias)
    tiles = jnp.where(valid[None], _bias_of_distance(bias_b, dist) * LOG2E, NEG_INF)
    return jnp.swapaxes(tiles, 0, 1)


def kernel(x, rel_bias, ffn1_norm, ffn1_w_gate, ffn1_w_up, ffn1_w_down, mix_norm, w_in,
           q_norm_a, k_norm_a, q_norm_b, k_norm_b, lambda_q1, lambda_k1, lambda_q2, lambda_k2,
           diff_subln, w_out, ffn2_norm, ffn2_w_gate, ffn2_w_up, ffn2_w_down):
    b, s, d = x.shape
    depth = w_in.shape[0]
    n_heads = d // HEAD_DIM
    wa = (n_heads // 2) * HEAD_DIM
    wb = (n_heads // 4) * HEAD_DIM
    wc = d - wa - wb
    widths = (wa, wb, wc)
    assert w_in.shape[2] == 3 * d and s % A_CHUNK == 0 and (b * s) % ROW_TILE == 0

    rb = rel_bias.astype(F32)
    bias_a = _dilated_bias_table(rb[:, :wa // HEAD_DIM])
    bias_bt = _diff_bias_table(rb[:, wa // HEAD_DIM:], s)

    def ffn1(layer):
        return (ffn1_norm[layer], ffn1_w_gate[layer].astype(BF16),
                ffn1_w_up[layer].astype(BF16), ffn1_w_down[layer].astype(BF16))

    def ffn2(layer):
        return (ffn2_norm[layer], ffn2_w_gate[layer].astype(BF16),
                ffn2_w_up[layer].astype(BF16), ffn2_w_down[layer].astype(BF16))

    x2 = _ffn(x.reshape(b * s, d), *ffn1(0))
    for layer in range(depth):
        qa, ka, va, kb, kc, qbt, vbt, qct, vct = _proj(
            x2, mix_norm[layer], w_in[layer], q_norm_a[layer], k_norm_a[layer],
            q_norm_b[layer], k_norm_b[layer], widths)
        seq3 = lambda t: t.reshape(b, s, t.shape[-1])

        out_a = _dilated_attention(seq3(qa), seq3(ka), seq3(va), bias_a)

        lam_init = 0.8 - 0.6 * math.exp(-0.3 * layer)
        lam = (jnp.exp(jnp.sum(lambda_q1[layer].astype(F32) * lambda_k1[layer].astype(F32)))
               - jnp.exp(jnp.sum(lambda_q2[layer].astype(F32) * lambda_k2[layer].astype(F32)))
               + lam_init)
        lam_row = jnp.full((1, ATT_TILE), lam, F32)
        post_gain = jnp.broadcast_to(
            (jnp.tile(diff_subln[layer].astype(F32), LANES // HEAD_DIM) * (1.0 - lam_init))[:, None],
            (LANES, ATT_TILE))
        out_b = _diff_attention(qbt, seq3(kb), vbt, bias_bt, lam_row, post_gain)

        out_c = _stick_attention(qct, seq3(kc), vct)

        ffns = [ffn2(layer)] + ([ffn1(layer + 1)] if layer + 1 < depth else [])
        x2 = _out_proj(x2, out_a.reshape(b * s, wa), out_b.reshape(b * s, wb),
                       out_c.reshape(b * s, wc), w_out[layer].astype(BF16), widths, ffns)
    return x2.reshape(b, s, d)
```

```python
import functools
import math

import jax
import jax.numpy as jnp
import numpy as np
from jax import lax
from jax.experimental import pallas as pl
from jax.experimental.pallas import tpu as pltpu

F32 = jnp.float32
BF16 = jnp.bfloat16

HEAD_DIM = 64
DIFF_HALF = HEAD_DIM // 2
N_BUCKETS = 32
MAX_DISTANCE = 2048
DILATED_BRANCHES = ((128, 1), (512, 4), (2048, 16))
Q_BLOCK = 128
EPS = 1e-6
NEG_INF = -1e30
SB_MASK = 1e4
LOG2E = math.log2(math.e)

LANES = 128
A_CHUNK = 2048
A_BLOCKS_PER_ITER = 4
ATT_TILE = 256
ROW_TILE = 512
ONES_ROWS = 16
QUERY_TILES = 2
VMEM_LIMIT = 56 * 1024 * 1024


def _cparams(*sem):
    return pltpu.CompilerParams(dimension_semantics=sem, vmem_limit_bytes=VMEM_LIMIT)


def _rms(x, gain_row):
    ms = jnp.mean(x * x, axis=-1, keepdims=True)
    return x * lax.rsqrt(ms + EPS) * gain_row


def _dot(a, b):
    return jnp.dot(a, b, preferred_element_type=F32)


def _dot_nt(a, b):
    return lax.dot_general(a, b, (((1,), (1,)), ((), ())), preferred_element_type=F32)


def _ffn_kernel(x_ref, g_ref, wg_ref, wu_ref, wd_ref, o_ref):
    x = x_ref[...]
    h = _rms(x, g_ref[...]).astype(BF16)
    gate = _dot(h, wg_ref[...])
    up = _dot(h, wu_ref[...])
    act = (gate * jax.nn.sigmoid(gate) * up).astype(BF16)
    o_ref[...] = x + 0.5 * _dot(act, wd_ref[...])


def _ffn(x2, gain, wg, wu, wd):
    t, d = x2.shape
    dff = wg.shape[1]
    row = pl.BlockSpec((ROW_TILE, d), lambda i: (i, 0))
    full = lambda shape: pl.BlockSpec(shape, lambda i: (0, 0))
    return pl.pallas_call(
        _ffn_kernel,
        out_shape=jax.ShapeDtypeStruct((t, d), F32),
        grid=(t // ROW_TILE,),
        in_specs=[row, full((1, d)), full((d, dff)), full((d, dff)), full((dff, d))],
        out_specs=row,
        compiler_params=_cparams("parallel"),
        name="ffn_half_step",
    )(x2, gain.reshape(1, d), wg, wu, wd)


def _proj_kernel(widths, x_ref, g_ref, wn_ref, wt_ref, gqa_ref, gka_ref, gkb_ref, gqb_ref,
                 g64_ref, g32_ref, qa_ref, ka_ref, va_ref, kb_ref, kc_ref,
                 qbt_ref, vbt_ref, qct_ref, vct_ref):
    wa, wb, wc = widths
    h = _rms(x_ref[...], g_ref[...]).astype(BF16)
    sub = 256

    def group_norm(t, ones_ref, gain_ref, group, scale):
        ss = _dot((t * t).astype(BF16), ones_ref[...])
        return t * lax.rsqrt(ss * (1.0 / group) + EPS) * (gain_ref[...] * scale)

    def group_norm_t(t, ones_ref, gain_ref, group, scale):
        ss = _dot(ones_ref[...], (t * t).astype(BF16))
        return t * lax.rsqrt(ss * (1.0 / group) + EPS) * (gain_ref[...] * scale)

    nat = _dot(h, wn_ref[...])
    col = 0
    for ref, width, post in (
            (qa_ref, wa, lambda t: group_norm(t, g64_ref, gqa_ref, HEAD_DIM,
                                              HEAD_DIM ** -0.5 * LOG2E)),
            (ka_ref, wa, lambda t: group_norm(t, g64_ref, gka_ref, HEAD_DIM, 1.0)),
            (va_ref, wa, lambda t: t),
            (kb_ref, wb, lambda t: group_norm(t, g32_ref, gkb_ref, DIFF_HALF, 1.0)),
            (kc_ref, wc, lambda t: t)):
        for c in range(0, width, sub):
            ref[:, c:c + sub] = post(nat[:, col + c:col + c + sub]).astype(ref.dtype)
        col += width

    tra = _dot_nt(wt_ref[...], h)
    row = 0
    for ref, width, post in (
            (qbt_ref, wb, lambda t: group_norm_t(t, g32_ref, gqb_ref, DIFF_HALF,
                                                 DIFF_HALF ** -0.5 * LOG2E)),
            (vbt_ref, wb, lambda t: t),
            (qct_ref, wc, lambda t: t * (HEAD_DIM ** -0.5 * LOG2E)),
            (vct_ref, wc, lambda t: t)):
        for r in range(0, width, sub):
            t = post(tra[row + r:row + r + sub]).astype(BF16)
            for j in range(ROW_TILE // ATT_TILE):
                ref[j, r:r + sub, :] = t[:, j * ATT_TILE:(j + 1) * ATT_TILE]
        row += width


def _block_diag_ones(group):
    idx = np.arange(256) // group
    return jnp.asarray(idx[:, None] == idx[None, :], dtype=BF16)


def _proj(x2, gain, w_in, gqa, gka, gqb, gkb, widths):
    t, d = x2.shape
    wa, wb, wc = widths
    assert wa % 256 == 0 and wb % 256 == 0 and wc % 256 == 0
    w = w_in.astype(BF16)
    o = np.cumsum([0, wa, wa, wa, wb, wb, wb, wc, wc, wc])
    sec = lambda i: w[:, o[i]:o[i + 1]]
    w_nat = jnp.concatenate([sec(0), sec(1), sec(2), sec(4), sec(7)], axis=1)
    w_tr = jnp.concatenate([sec(3), sec(5), sec(6), sec(8)], axis=1).T
    row = lambda wd: pl.BlockSpec((ROW_TILE, wd), lambda i: (i, 0))
    full = lambda shape: pl.BlockSpec(shape, lambda i: (0,) * len(shape))
    slab = lambda wd: pl.BlockSpec((ROW_TILE // ATT_TILE, wd, ATT_TILE), lambda i: (i, 0, 0))
    tile256 = lambda g: jnp.tile(g.astype(F32), 256 // g.shape[0])
    out_shape = ([jax.ShapeDtypeStruct((t, wa), F32)] * 3
                 + [jax.ShapeDtypeStruct((t, wb), BF16), jax.ShapeDtypeStruct((t, wc), BF16)]
                 + [jax.ShapeDtypeStruct((t // ATT_TILE, wd, ATT_TILE), BF16)
                    for wd in (wb, wb, wc, wc)])
    return pl.pallas_call(
        functools.partial(_proj_kernel, widths),
        out_shape=out_shape,
        grid=(t // ROW_TILE,),
        in_specs=[row(d), full((1, d)), full(w_nat.shape), full(w_tr.shape)]
                 + [full((1, 256))] * 3 + [full((256, 1))] + [full((256, 256))] * 2,
        out_specs=[row(wa)] * 3 + [row(wb), row(wc)] + [slab(wb), slab(wb), slab(wc), slab(wc)],
        compiler_params=_cparams("parallel"),
        name="norm_in_proj",
    )(x2, gain.reshape(1, d), w_nat, w_tr, tile256(gqa).reshape(1, 256), tile256(gka).reshape(1, 256),
      tile256(gkb).reshape(1, 256), tile256(gqb).reshape(256, 1),
      _block_diag_ones(HEAD_DIM), _block_diag_ones(DIFF_HALF))


def _dilated_kernel(q_ref, kp_ref, kc_ref, vp_ref, vc_ref, bias_ref, o_ref,
                    kk_ref, vv_ref, m_ref, l_ref, acc_ref):
    chunk = pl.program_id(1)
    kk_ref[0:A_CHUNK] = kp_ref[0]
    kk_ref[A_CHUNK:] = kc_ref[0]
    vv_ref[0:A_CHUNK] = vp_ref[0]
    vv_ref[A_CHUNK:] = vc_ref[0]

    lane = lax.broadcasted_iota(jnp.int32, (Q_BLOCK, LANES), 1)
    head0 = lane < HEAD_DIM

    order = sorted(range(len(DILATED_BRANCHES)), key=lambda b: -DILATED_BRANCHES[b][1])
    for bi in order:
        window, dil = DILATED_BRANCHES[bi]
        fresh = bi == order[0]
        nblk = A_CHUNK // (Q_BLOCK * dil)
        shift = int(math.log2(nblk))
        n_iter = dil * nblk
        ds = (lambda start, size, dil=dil:
              pl.ds(start, size, stride=dil) if dil > 1 else pl.ds(start, size))

        def load(idx, bi=bi, dil=dil, nblk=nblk, shift=shift, ds=ds, fresh=fresh):
            rho = idx >> shift
            t = idx & (nblk - 1)
            q_start = rho + t * (Q_BLOCK * dil)
            rows = ds(q_start, Q_BLOCK)
            krows = ds(A_CHUNK + q_start - Q_BLOCK * dil, 2 * Q_BLOCK)
            first = jnp.logical_and(chunk == 0, t == 0).astype(jnp.int32)
            blk = dict(rows=rows, q=q_ref[0, rows, :], k=kk_ref[krows, :], v=vv_ref[krows, :],
                       bias=bias_ref[bi, first, 0])
            if not fresh:
                blk.update(m=jnp.concatenate([m_ref[0, rows, :], m_ref[1, rows, :]], axis=0),
                           l=jnp.concatenate([l_ref[0, rows, :], l_ref[1, rows, :]], axis=0),
                           acc=acc_ref[rows, :])
            return blk

        def compute(b):
            q = b["q"]
            q2 = jnp.concatenate([jnp.where(head0, q, 0.0), jnp.where(head0, 0.0, q)],
                                 axis=0).astype(BF16)
            s = _dot_nt(q2, b["k"].astype(BF16)) + b["bias"]
            row_max = jnp.max(s, axis=-1, keepdims=True)
            if "m" not in b:
                m_new = jnp.broadcast_to(row_max, (2 * Q_BLOCK, LANES))
                p = jnp.exp2(s - row_max)
                l_new = jnp.broadcast_to(jnp.sum(p, axis=-1, keepdims=True), (2 * Q_BLOCK, LANES))
                pv = _dot(p.astype(BF16), b["v"].astype(BF16))
                return m_new, l_new, jnp.where(head0, pv[:Q_BLOCK], pv[Q_BLOCK:])
            m_new = jnp.maximum(b["m"], row_max)
            alpha = jnp.exp2(b["m"] - m_new)
            p = jnp.exp2(s - jnp.concatenate([m_new, m_new], axis=1))
            l_new = alpha * b["l"] + jnp.sum(p, axis=-1, keepdims=True)
            pv = _dot(p.astype(BF16), b["v"].astype(BF16))
            acc_new = jnp.where(head0, alpha[:Q_BLOCK] * b["acc"] + pv[:Q_BLOCK],
                                alpha[Q_BLOCK:] * b["acc"] + pv[Q_BLOCK:])
            return m_new, l_new, acc_new

        def store(b, res):
            m_new, l_new, acc_new = res
            rows = b["rows"]
            acc_ref[rows, :] = acc_new
            m_ref[0, rows, :] = m_new[:Q_BLOCK]
            m_ref[1, rows, :] = m_new[Q_BLOCK:]
            l_ref[0, rows, :] = l_new[:Q_BLOCK]
            l_ref[1, rows, :] = l_new[Q_BLOCK:]

        def some_blocks(i, carry, load=load, compute=compute, store=store,
                        part=n_iter // A_BLOCKS_PER_ITER):
            blocks = [load(i + c * part) for c in range(A_BLOCKS_PER_ITER)]
            results = [compute(b) for b in blocks]
            for b, res in zip(blocks, results):
                store(b, res)
            return carry

        lax.fori_loop(0, n_iter // A_BLOCKS_PER_ITER, some_blocks, 0)

    lane_c = lax.broadcasted_iota(jnp.int32, (A_CHUNK, LANES), 1)
    denom = jnp.where(lane_c < HEAD_DIM, l_ref[0], l_ref[1])
    o_ref[0] = (acc_ref[...] / denom).astype(BF16)


def _dilated_attention(qa, ka, va, bias_a):
    b, s, wa = qa.shape
    n_pairs = wa // LANES
    cur = pl.BlockSpec((1, A_CHUNK, LANES), lambda i, c, p: (i, c, p))
    prev = pl.BlockSpec((1, A_CHUNK, LANES), lambda i, c, p: (i, jnp.maximum(c - 1, 0), p))
    nb = len(DILATED_BRANCHES)
    bias = pl.BlockSpec((nb, 2, 1, 2 * Q_BLOCK, 2 * Q_BLOCK), lambda i, c, p: (0, 0, p, 0, 0))
    return pl.pallas_call(
        _dilated_kernel,
        out_shape=jax.ShapeDtypeStruct((b, s, wa), BF16),
        grid=(b, s // A_CHUNK, n_pairs),
        in_specs=[cur, prev, cur, prev, cur, bias],
        out_specs=cur,
        scratch_shapes=[pltpu.VMEM((2 * A_CHUNK, LANES), F32),
                        pltpu.VMEM((2 * A_CHUNK, LANES), F32),
                        pltpu.VMEM((2, A_CHUNK, LANES), F32),
                        pltpu.VMEM((2, A_CHUNK, LANES), F32),
                        pltpu.VMEM((A_CHUNK, LANES), F32)],
        compiler_params=_cparams("parallel", "parallel", "parallel"),
        name="dilated_attention",
    )(qa, ka, ka, va, va, bias_a)


def _diff_kernel(n_bias, qt_ref, k_ref, vt_ref, bias_ref, lam_ref, gain_ref, o_ref,
                 q4_ref, m_ref, mt_ref, acc_ref, s_ref, p_ref):
    t = ATT_TILE
    row = lax.broadcasted_iota(jnp.int32, (LANES, t), 0)
    ones = jnp.ones((ONES_ROWS, 2 * t), BF16)

    def pipeline(g):
        qi = pl.program_id(2) * QUERY_TILES + g

        def tile_of(step):
            return jnp.clip(qi - step, 0, qi)

        def init():
            qt = qt_ref[0, g]
            for c in range(4):
                sel = jnp.logical_and(row >= c * DIFF_HALF, row < (c + 1) * DIFF_HALF)
                q4_ref[g, :, c * t:(c + 1) * t] = jnp.where(sel, qt, jnp.zeros_like(qt))
            m_ref[g] = jnp.full(m_ref.shape[1:], NEG_INF, F32)
            acc_ref[g] = jnp.zeros(acc_ref.shape[1:], F32)
            p_ref[g] = jnp.zeros(p_ref.shape[1:], BF16)

        def values(i):
            vt = jnp.concatenate([vt_ref[0, tile_of(2 * i)], vt_ref[0, tile_of(2 * i + 1)]], axis=1)
            return [_dot(jnp.concatenate([vt[h * HEAD_DIM:(h + 1) * HEAD_DIM], ones], axis=0),
                         p_ref[g, :, h * 2 * t:(h + 1) * 2 * t]) for h in range(2)]

        def logits(i):
            tile_max = None
            for half in range(2):
                step = 2 * i + half
                k = k_ref[0, pl.ds(pl.multiple_of(tile_of(step) * t, t), t), :]
                d = jnp.where(step > qi, n_bias, jnp.minimum(step, n_bias - 1))
                b0 = bias_ref[d, 0]
                b1 = bias_ref[d, 1]
                s = _dot(k, q4_ref[g]) + jnp.concatenate([b0, b0, b1, b1], axis=1)
                s_ref[g, half * t:(half + 1) * t] = s
                mx = jnp.max(s, axis=0, keepdims=True)
                tile_max = mx if tile_max is None else jnp.maximum(tile_max, mx)
            mt_ref[g] = tile_max

        def sweep(i):
            pv = values(i - 1)
            m_old = m_ref[g]
            m_new = jnp.maximum(m_old, mt_ref[g])
            alpha = jnp.exp2(m_old - m_new)
            m_ref[g] = m_new
            p_ref[g] = jnp.exp2(s_ref[g] - m_new).astype(BF16)
            for h in range(2):
                acc_ref[g, h] = alpha[:, h * 2 * t:(h + 1) * 2 * t] * (acc_ref[g, h] + pv[h])
            logits(i + 1)

        def fill():
            init()
            logits(0)

        def drain(n_iter):
            last = values(n_iter - 1)
            lam = lam_ref[...]
            outs = []
            for h in range(2):
                acc = acc_ref[g, h] + last[h]
                pv = acc[0:HEAD_DIM] / acc[HEAD_DIM:HEAD_DIM + 1]
                diff = pv[:, 0:t] - lam * pv[:, t:2 * t]
                ms = jnp.mean(diff * diff, axis=0, keepdims=True)
                outs.append(diff * lax.rsqrt(ms + EPS))
            out_t = jnp.concatenate(outs, axis=0) * gain_ref[...]
            o_ref[0, g * t:(g + 1) * t] = out_t.T.astype(BF16)

        return fill, sweep, drain

    stages = [pipeline(g) for g in range(QUERY_TILES)]
    for fill, _, _ in stages:
        fill()
    n_iter = (pl.program_id(2) * QUERY_TILES + QUERY_TILES + 1) // 2

    def body(i, c):
        for _, sweep, _ in stages:
            sweep(i)
        return c

    lax.fori_loop(0, n_iter, body, 0)
    for _, _, drain in stages:
        drain(n_iter)


def _diff_attention(qbt, kb, vbt, bias_bt, lam, post_gain):
    b, s, wb = kb.shape
    n_pairs = wb // LANES
    n_bias = bias_bt.shape[0] - 1
    t = ATT_TILE
    nt = s // t
    g = QUERY_TILES
    qspec = pl.BlockSpec((1, g, LANES, t), lambda i, p, j: (i, j, p, 0))
    kspec = pl.BlockSpec((1, s, LANES), lambda i, p, j: (i, 0, p))
    vspec = pl.BlockSpec((1, nt, LANES, t), lambda i, p, j: (i, 0, p, 0))
    bspec = pl.BlockSpec((n_bias + 1, 2, t, t), lambda i, p, j: (0, p, 0, 0))
    return pl.pallas_call(
        functools.partial(_diff_kernel, n_bias),
        out_shape=jax.ShapeDtypeStruct((b, s, wb), BF16),
        grid=(b, n_pairs, nt // g),
        in_specs=[qspec, kspec, vspec, bspec,
                  pl.BlockSpec((1, t), lambda i, p, j: (0, 0)),
                  pl.BlockSpec((LANES, t), lambda i, p, j: (0, 0))],
        out_specs=pl.BlockSpec((1, g * t, LANES), lambda i, p, j: (i, j, p)),
        scratch_shapes=[pltpu.VMEM((g, LANES, 4 * t), BF16),
                        pltpu.VMEM((g, 1, 4 * t), F32),
                        pltpu.VMEM((g, 1, 4 * t), F32),
                        pltpu.VMEM((g, 2, HEAD_DIM + ONES_ROWS, 2 * t), F32),
                        pltpu.VMEM((g, 2 * t, 4 * t), F32),
                        pltpu.VMEM((g, 2 * t, 4 * t), BF16)],
        compiler_params=_cparams("parallel", "parallel", "parallel"),
        name="diff_attention",
    )(qbt.reshape(b, nt, wb, t), kb, vbt.reshape(b, nt, wb, t), bias_bt, lam, post_gain)


def _stick_kernel(qt_ref, k_ref, vt_ref, tri_ref, o_ref, q2_ref, carry_ref, scale_ref, acc_ref,
                  z_ref, lw_ref, p_ref):
    t = ATT_TILE
    row = lax.broadcasted_iota(jnp.int32, (LANES, t), 0)

    def pipeline(g):
        qi = pl.program_id(2) * QUERY_TILES + g

        def tile_of(step):
            return jnp.clip(qi - step, 0, qi)

        def init():
            qt = qt_ref[0, g]
            zero = jnp.zeros_like(qt)
            q2_ref[g, :, 0:t] = jnp.where(row < HEAD_DIM, qt, zero)
            q2_ref[g, :, t:] = jnp.where(row < HEAD_DIM, zero, qt)
            carry_ref[g] = jnp.zeros(carry_ref.shape[1:], F32)
            scale_ref[g] = jnp.ones(scale_ref.shape[1:], F32)
            acc_ref[g] = jnp.zeros(acc_ref.shape[1:], F32)
            p_ref[g] = jnp.zeros(p_ref.shape[1:], BF16)

        def logits(i, diagonal=False):
            for half in range(2):
                k = k_ref[0, pl.ds(pl.multiple_of(tile_of(2 * i + half) * t, t), t), :]
                z = _dot(k, q2_ref[g])
                if diagonal and half == 0:
                    key = lax.broadcasted_iota(jnp.int32, (t, t), 0)
                    qry = lax.broadcasted_iota(jnp.int32, (t, t), 1)
                    strict = jnp.concatenate([key < qry, key < qry], axis=1)
                    z = jnp.where(strict, z, -SB_MASK * LOG2E)
                z_ref[g, half] = z

        def log_weights():
            for half in range(2):
                z = z_ref[g, half]
                neg_abs = pltpu.bitcast(pltpu.bitcast(z, jnp.uint32) | jnp.uint32(0x80000000), F32)
                sp = jnp.maximum(z, 0.0) + jnp.log(1.0 + jnp.exp2(neg_abs)) * LOG2E
                w = _dot(tri_ref[...], sp.astype(BF16))
                lw_ref[g, half, 0:t] = z + w[0:t]
                lw_ref[g, half, t:] = w[t:]

        def weights():
            carry = carry_ref[g]
            near_sum = lw_ref[g, 0, t:t + 1]
            p_ref[g, 0:t] = jnp.exp2(lw_ref[g, 0, 0:t]).astype(BF16)
            p_ref[g, t:] = jnp.exp2(lw_ref[g, 1, 0:t] + near_sum).astype(BF16)
            scale_ref[g] = jnp.exp2(carry)
            carry_ref[g] = carry + near_sum + lw_ref[g, 1, t:t + 1]

        def values(i):
            vts = []
            for half in range(2):
                step = 2 * i + half
                valid = jnp.logical_and(step >= 0, step <= qi)
                vt = vt_ref[0, tile_of(step)]
                vts.append(jnp.where(valid, vt, jnp.zeros_like(vt)))
            acc_ref[g] += _dot(jnp.concatenate(vts, axis=1), p_ref[g]) * scale_ref[g]

        def sweep(i):
            values(i - 1)
            weights()
            log_weights()
            logits(i + 2)

        def first_pair():
            init()
            logits(0, diagonal=True)
            log_weights()
            weights()

        def catch_up():
            logits(1)
            log_weights()
            logits(2)

        def drain(n_iter):
            values(n_iter - 1)
            out_t = jnp.concatenate([acc_ref[g, 0:HEAD_DIM, 0:t], acc_ref[g, HEAD_DIM:, t:]], axis=0)
            o_ref[0, g * t:(g + 1) * t] = out_t.T.astype(BF16)

        return first_pair, catch_up, sweep, drain

    stages = [pipeline(g) for g in range(QUERY_TILES)]
    for first_pair, _, _, _ in stages:
        first_pair()
    n_iter = (pl.program_id(2) * QUERY_TILES + QUERY_TILES + 1) // 2

    def live():
        return (jnp.max(jnp.exp2(carry_ref[...])) > 0.0).astype(jnp.int32)

    live_0 = live()

    @pl.when(live_0 > 0)
    def _():
        for _, catch_up, _, _ in stages:
            catch_up()

    def body(state):
        i, _ = state
        for _, _, sweep, _ in stages:
            sweep(i)
        return i + 1, live()

    n_done, _ = lax.while_loop(lambda st: jnp.logical_and(st[0] < n_iter, st[1] > 0), body,
                               (jnp.int32(1), live_0))
    for _, _, _, drain in stages:
        drain(n_done)


def _stick_attention(qct, kc, vct):
    b, s, wc = kc.shape
    n_pairs = wc // LANES
    t = ATT_TILE
    nt = s // t
    g = QUERY_TILES
    idx = np.arange(t)
    tri = np.concatenate([idx[None, :] >= idx[:, None], np.ones((ONES_ROWS, t), bool)], axis=0)
    tri = -jnp.asarray(tri, dtype=BF16)
    return pl.pallas_call(
        _stick_kernel,
        out_shape=jax.ShapeDtypeStruct((b, s, wc), BF16),
        grid=(b, n_pairs, nt // g),
        in_specs=[pl.BlockSpec((1, g, LANES, t), lambda i, p, j: (i, j, p, 0)),
                  pl.BlockSpec((1, s, LANES), lambda i, p, j: (i, 0, p)),
                  pl.BlockSpec((1, nt, LANES, t), lambda i, p, j: (i, 0, p, 0)),
                  pl.BlockSpec((t + ONES_ROWS, t), lambda i, p, j: (0, 0))],
        out_specs=pl.BlockSpec((1, g * t, LANES), lambda i, p, j: (i, j, p)),
        scratch_shapes=[pltpu.VMEM((g, LANES, 2 * t), BF16),
                        pltpu.VMEM((g, 1, 2 * t), F32),
                        pltpu.VMEM((g, 1, 2 * t), F32),
                        pltpu.VMEM((g, LANES, 2 * t), F32),
                        pltpu.VMEM((g, 2, t, 2 * t), F32),
                        pltpu.VMEM((g, 2, t + ONES_ROWS, 2 * t), F32),
                        pltpu.VMEM((g, 2 * t, 2 * t), BF16)],
        compiler_params=_cparams("parallel", "parallel", "parallel"),
        name="stick_breaking_attention",
    )(qct.reshape(b, nt, wc, t), kc, vct.reshape(b, nt, wc, t), tri)


def _out_proj_kernel(widths, n_ffn, x_ref, a_ref, b_ref, c_ref, w_ref, *rest):
    o_ref = rest[4 * n_ffn]
    wa, wb, wc = widths
    y = _dot(a_ref[...], w_ref[0:wa])
    y += _dot(b_ref[...], w_ref[wa:wa + wb])
    y += _dot(c_ref[...], w_ref[wa + wb:wa + wb + wc])
    x = x_ref[...] + y
    for f in range(n_ffn):
        g_ref, wg_ref, wu_ref, wd_ref = rest[4 * f:4 * f + 4]
        h = _rms(x, g_ref[...]).astype(BF16)
        gate = _dot(h, wg_ref[...])
        up = _dot(h, wu_ref[...])
        act = (gate * jax.nn.sigmoid(gate) * up).astype(BF16)
        x = x + 0.5 * _dot(act, wd_ref[...])
    o_ref[...] = x


def _out_proj(x2, oa, ob, oc, w_out, widths, ffns):
    t, d = x2.shape
    row = lambda w: pl.BlockSpec((ROW_TILE, w), lambda i: (i, 0))
    full = lambda a: pl.BlockSpec(a.shape, lambda i: (0, 0))
    ffn_args = []
    for gain, wg, wu, wd in ffns:
        ffn_args += [gain.reshape(1, d), wg, wu, wd]
    return pl.pallas_call(
        functools.partial(_out_proj_kernel, widths, len(ffns)),
        out_shape=jax.ShapeDtypeStruct((t, d), F32),
        grid=(t // ROW_TILE,),
        in_specs=[row(d), row(widths[0]), row(widths[1]), row(widths[2]), full(w_out)]
                 + [full(a) for a in ffn_args],
        out_specs=row(d),
        compiler_params=_cparams("parallel"),
        name="out_proj_residual_ffn",
    )(x2, oa, ob, oc, w_out, *ffn_args)


def _t5_bucket(dist):
    dist = jnp.maximum(dist, 0)
    max_exact = N_BUCKETS // 2
    d_f = jnp.maximum(dist, 1).astype(F32)
    large = max_exact + (jnp.log(d_f / max_exact) / math.log(MAX_DISTANCE / max_exact)
                         * (N_BUCKETS - max_exact)).astype(jnp.int32)
    large = jnp.minimum(large, N_BUCKETS - 1)
    return jnp.where(dist < max_exact, dist, large)


def _bias_of_distance(bias, dist):
    bucket = _t5_bucket(dist)[None]
    out = jnp.zeros((bias.shape[1],) + dist.shape, F32)
    for b in range(N_BUCKETS):
        out = jnp.where(bucket == b, bias[b].reshape((-1,) + (1,) * dist.ndim), out)
    return out


def _dilated_bias_table(bias_a):
    n_heads = bias_a.shape[1]
    i = jnp.arange(Q_BLOCK, dtype=jnp.int32)[:, None]
    j = jnp.arange(2 * Q_BLOCK, dtype=jnp.int32)[None, :]
    tables = []
    for window, dil in DILATED_BRANCHES:
        n = window // dil
        off = i + n - j
        band = (off >= 0) & (off <= n)
        bias = _bias_of_distance(bias_a, off * dil)
        variants = [jnp.where(valid[None], bias * LOG2E, NEG_INF)
                    for valid in (band, band & (j >= n))]
        tables.append(jnp.stack(variants))
    table = jnp.stack(tables)
    return table.reshape(len(DILATED_BRANCHES), 2, n_heads // 2, 2 * Q_BLOCK, 2 * Q_BLOCK)


def _diff_bias_table(bias_b, seq):
    t = ATT_TILE
    n_bias = min(seq // t, MAX_DISTANCE // t + 2)
    key = jnp.arange(t, dtype=jnp.int32)[None, :, None]
    qry = jnp.arange(t, dtype=jnp.int32)[None, None, :]
    dist = jnp.arange(n_bias + 1, dtype=jnp.int32)[:, None, None] * t + qry - key
    valid = (dist >= 0) & (jnp.arange(n_bias + 1)[:, None, None] < n_bias)
    tiles = jnp.where(valid[None], _bias_of_distance(bias_b, dist) * LOG2E, NEG_INF)
    return jnp.swapaxes(tiles, 0, 1)


def kernel(x, rel_bias, ffn1_norm, ffn1_w_gate, ffn1_w_up, ffn1_w_down, mix_norm, w_in,
           q_norm_a, k_norm_a, q_norm_b, k_norm_b, lambda_q1, lambda_k1, lambda_q2, lambda_k2,
           diff_subln, w_out, ffn2_norm, ffn2_w_gate, ffn2_w_up, ffn2_w_down):
    b, s, d = x.shape
    depth = w_in.shape[0]
    n_heads = d // HEAD_DIM
    wa = (n_heads // 2) * HEAD_DIM
    wb = (n_heads // 4) * HEAD_DIM
    wc = d - wa - wb
    widths = (wa, wb, wc)
    assert w_in.shape[2] == 3 * d and s % A_CHUNK == 0 and (b * s) % ROW_TILE == 0

    rb = rel_bias.astype(F32)
    bias_a = _dilated_bias_table(rb[:, :wa // HEAD_DIM])
    bias_bt = _diff_bias_table(rb[:, wa // HEAD_DIM:], s)

    def ffn1(layer):
        return (ffn1_norm[layer], ffn1_w_gate[layer].astype(BF16),
                ffn1_w_up[layer].astype(BF16), ffn1_w_down[layer].astype(BF16))

    def ffn2(layer):
        return (ffn2_norm[layer], ffn2_w_gate[layer].astype(BF16),
                ffn2_w_up[layer].astype(BF16), ffn2_w_down[layer].astype(BF16))

    x2 = _ffn(x.reshape(b * s, d), *ffn1(0))
    for layer in range(depth):
        qa, ka, va, kb, kc, qbt, vbt, qct, vct = _proj(
            x2, mix_norm[layer], w_in[layer], q_norm_a[layer], k_norm_a[layer],
            q_norm_b[layer], k_norm_b[layer], widths)
        seq3 = lambda t: t.reshape(b, s, t.shape[-1])

        out_a = _dilated_attention(seq3(qa), seq3(ka), seq3(va), bias_a)

        lam_init = 0.8 - 0.6 * math.exp(-0.3 * layer)
        lam = (jnp.exp(jnp.sum(lambda_q1[layer].astype(F32) * lambda_k1[layer].astype(F32)))
               - jnp.exp(jnp.sum(lambda_q2[layer].astype(F32) * lambda_k2[layer].astype(F32)))
               + lam_init)
        lam_row = jnp.full((1, ATT_TILE), lam, F32)
        post_gain = jnp.broadcast_to(
            (jnp.tile(diff_subln[layer].astype(F32), LANES // HEAD_DIM) * (1.0 - lam_init))[:, None],
            (LANES, ATT_TILE))
        out_b = _diff_attention(qbt, seq3(kb), vbt, bias_bt, lam_row, post_gain)

        out_c = _stick_attention(qct, seq3(kc), vct)

        ffns = [ffn2(layer)] + ([ffn1(layer + 1)] if layer + 1 < depth else [])
        x2 = _out_proj(x2, out_a.reshape(b * s, wa), out_b.reshape(b * s, wb),
                       out_c.reshape(b * s, wc), w_out[layer].astype(BF16), widths, ffns)
    return x2.reshape(b, s, d)
```

```python
import functools
import math

import jax
import jax.numpy as jnp
import numpy as np
from jax import lax
from jax.experimental import pallas as pl
from jax.experimental.pallas import tpu as pltpu

F32 = jnp.float32
BF16 = jnp.bfloat16

HEAD_DIM = 64
DIFF_HALF = HEAD_DIM // 2
N_BUCKETS = 32
MAX_DISTANCE = 2048
DILATED_BRANCHES = ((128, 1), (512, 4), (2048, 16))
Q_BLOCK = 128
EPS = 1e-6
NEG_INF = -1e30
SB_MASK = 1e4
LOG2E = math.log2(math.e)

LANES = 128
A_CHUNK = 2048
A_BLOCKS_PER_ITER = 16
ATT_TILE = 256
ROW_TILE = 512
ONES_ROWS = 16
QUERY_TILES = 2
VMEM_LIMIT = 56 * 1024 * 1024


def _cparams(*sem):
    return pltpu.CompilerParams(dimension_semantics=sem, vmem_limit_bytes=VMEM_LIMIT)


def _rms(x, gain_row):
    ms = jnp.mean(x * x, axis=-1, keepdims=True)
    return x * lax.rsqrt(ms + EPS) * gain_row


def _dot(a, b):
    return jnp.dot(a, b, preferred_element_type=F32)


def _dot_nt(a, b):
    return lax.dot_general(a, b, (((1,), (1,)), ((), ())), preferred_element_type=F32)


def _ffn_kernel(x_ref, g_ref, wg_ref, wu_ref, wd_ref, o_ref):
    x = x_ref[...]
    h = _rms(x, g_ref[...]).astype(BF16)
    gate = _dot(h, wg_ref[...])
    up = _dot(h, wu_ref[...])
    act = (gate * jax.nn.sigmoid(gate) * up).astype(BF16)
    o_ref[...] = x + 0.5 * _dot(act, wd_ref[...])


def _ffn(x2, gain, wg, wu, wd):
    t, d = x2.shape
    dff = wg.shape[1]
    row = pl.BlockSpec((ROW_TILE, d), lambda i: (i, 0))
    full = lambda shape: pl.BlockSpec(shape, lambda i: (0, 0))
    return pl.pallas_call(
        _ffn_kernel,
        out_shape=jax.ShapeDtypeStruct((t, d), F32),
        grid=(t // ROW_TILE,),
        in_specs=[row, full((1, d)), full((d, dff)), full((d, dff)), full((dff, d))],
        out_specs=row,
        compiler_params=_cparams("parallel"),
        name="ffn_half_step",
    )(x2, gain.reshape(1, d), wg, wu, wd)


def _proj_kernel(widths, x_ref, g_ref, wn_ref, wt_ref, gqa_ref, gka_ref, gkb_ref, gqb_ref,
                 g64_ref, g32_ref, qa_ref, ka_ref, va_ref, kb_ref, kc_ref,
                 qbt_ref, vbt_ref, qct_ref, vct_ref):
    wa, wb, wc = widths
    h = _rms(x_ref[...], g_ref[...]).astype(BF16)
    sub = 256

    def group_norm(t, ones_ref, gain_ref, group, scale):
        ss = _dot((t * t).astype(BF16), ones_ref[...])
        return t * lax.rsqrt(ss * (1.0 / group) + EPS) * (gain_ref[...] * scale)

    def group_norm_t(t, ones_ref, gain_ref, group, scale):
        ss = _dot(ones_ref[...], (t * t).astype(BF16))
        return t * lax.rsqrt(ss * (1.0 / group) + EPS) * (gain_ref[...] * scale)

    nat = _dot(h, wn_ref[...])
    col = 0
    for ref, width, post in (
            (qa_ref, wa, lambda t: group_norm(t, g64_ref, gqa_ref, HEAD_DIM,
                                              HEAD_DIM ** -0.5 * LOG2E)),
            (ka_ref, wa, lambda t: group_norm(t, g64_ref, gka_ref, HEAD_DIM, 1.0)),
            (va_ref, wa, lambda t: t),
            (kb_ref, wb, lambda t: group_norm(t, g32_ref, gkb_ref, DIFF_HALF, 1.0)),
            (kc_ref, wc, lambda t: t)):
        for c in range(0, width, sub):
            ref[:, c:c + sub] = post(nat[:, col + c:col + c + sub]).astype(ref.dtype)
        col += width

    tra = _dot_nt(wt_ref[...], h)
    row = 0
    for ref, width, post in (
            (qbt_ref, wb, lambda t: group_norm_t(t, g32_ref, gqb_ref, DIFF_HALF,
                                                 DIFF_HALF ** -0.5 * LOG2E)),
            (vbt_ref, wb, lambda t: t),
            (qct_ref, wc, lambda t: t * (HEAD_DIM ** -0.5 * LOG2E)),
            (vct_ref, wc, lambda t: t)):
        for r in range(0, width, sub):
            t = post(tra[row + r:row + r + sub]).astype(BF16)
            for j in range(ROW_TILE // ATT_TILE):
                ref[j, r:r + sub, :] = t[:, j * ATT_TILE:(j + 1) * ATT_TILE]
        row += width


def _block_diag_ones(group):
    idx = np.arange(256) // group
    return jnp.asarray(idx[:, None] == idx[None, :], dtype=BF16)


def _proj(x2, gain, w_in, gqa, gka, gqb, gkb, widths):
    t, d = x2.shape
    wa, wb, wc = widths
    assert wa % 256 == 0 and wb % 256 == 0 and wc % 256 == 0
    w = w_in.astype(BF16)
    o = np.cumsum([0, wa, wa, wa, wb, wb, wb, wc, wc, wc])
    sec = lambda i: w[:, o[i]:o[i + 1]]
    w_nat = jnp.concatenate([sec(0), sec(1), sec(2), sec(4), sec(7)], axis=1)
    w_tr = jnp.concatenate([sec(3), sec(5), sec(6), sec(8)], axis=1).T
    row = lambda wd: pl.BlockSpec((ROW_TILE, wd), lambda i: (i, 0))
    full = lambda shape: pl.BlockSpec(shape, lambda i: (0,) * len(shape))
    slab = lambda wd: pl.BlockSpec((ROW_TILE // ATT_TILE, wd, ATT_TILE), lambda i: (i, 0, 0))
    tile256 = lambda g: jnp.tile(g.astype(F32), 256 // g.shape[0])
    out_shape = ([jax.ShapeDtypeStruct((t, wa), F32)] * 3
                 + [jax.ShapeDtypeStruct((t, wb), BF16), jax.ShapeDtypeStruct((t, wc), BF16)]
                 + [jax.ShapeDtypeStruct((t // ATT_TILE, wd, ATT_TILE), BF16)
                    for wd in (wb, wb, wc, wc)])
    return pl.pallas_call(
        functools.partial(_proj_kernel, widths),
        out_shape=out_shape,
        grid=(t // ROW_TILE,),
        in_specs=[row(d), full((1, d)), full(w_nat.shape), full(w_tr.shape)]
                 + [full((1, 256))] * 3 + [full((256, 1))] + [full((256, 256))] * 2,
        out_specs=[row(wa)] * 3 + [row(wb), row(wc)] + [slab(wb), slab(wb), slab(wc), slab(wc)],
        compiler_params=_cparams("parallel"),
        name="norm_in_proj",
    )(x2, gain.reshape(1, d), w_nat, w_tr, tile256(gqa).reshape(1, 256), tile256(gka).reshape(1, 256),
      tile256(gkb).reshape(1, 256), tile256(gqb).reshape(256, 1),
      _block_diag_ones(HEAD_DIM), _block_diag_ones(DIFF_HALF))


def _dilated_kernel(q_ref, kp_ref, kc_ref, vp_ref, vc_ref, bias_ref, o_ref,
                    kk_ref, vv_ref, m_ref, l_ref, acc_ref):
    chunk = pl.program_id(1)
    kk_ref[0:A_CHUNK] = kp_ref[0]
    kk_ref[A_CHUNK:] = kc_ref[0]
    vv_ref[0:A_CHUNK] = vp_ref[0]
    vv_ref[A_CHUNK:] = vc_ref[0]

    lane = lax.broadcasted_iota(jnp.int32, (Q_BLOCK, LANES), 1)
    head0 = lane < HEAD_DIM

    order = sorted(range(len(DILATED_BRANCHES)), key=lambda b: -DILATED_BRANCHES[b][1])
    for bi in order:
        window, dil = DILATED_BRANCHES[bi]
        fresh = bi == order[0]
        nblk = A_CHUNK // (Q_BLOCK * dil)
        shift = int(math.log2(nblk))
        n_iter = dil * nblk
        ds = (lambda start, size, dil=dil:
              pl.ds(start, size, stride=dil) if dil > 1 else pl.ds(start, size))

        def load(idx, bi=bi, dil=dil, nblk=nblk, shift=shift, ds=ds, fresh=fresh):
            rho = idx >> shift
            t = idx & (nblk - 1)
            q_start = rho + t * (Q_BLOCK * dil)
            rows = ds(q_start, Q_BLOCK)
            krows = ds(A_CHUNK + q_start - Q_BLOCK * dil, 2 * Q_BLOCK)
            first = jnp.logical_and(chunk == 0, t == 0).astype(jnp.int32)
            blk = dict(rows=rows, q=q_ref[0, rows, :], k=kk_ref[krows, :], v=vv_ref[krows, :],
                       bias=bias_ref[bi, first, 0])
            if not fresh:
                blk.update(m=jnp.concatenate([m_ref[0, rows, :], m_ref[1, rows, :]], axis=0),
                           l=jnp.concatenate([l_ref[0, rows, :], l_ref[1, rows, :]], axis=0),
                           acc=acc_ref[rows, :])
            return blk

        def compute(b):
            q = b["q"]
            q2 = jnp.concatenate([jnp.where(head0, q, 0.0), jnp.where(head0, 0.0, q)],
                                 axis=0).astype(BF16)
            s = _dot_nt(q2, b["k"].astype(BF16)) + b["bias"]
            row_max = jnp.max(s, axis=-1, keepdims=True)
            if "m" not in b:
                m_new = jnp.broadcast_to(row_max, (2 * Q_BLOCK, LANES))
                p = jnp.exp2(s - row_max)
                l_new = jnp.broadcast_to(jnp.sum(p, axis=-1, keepdims=True), (2 * Q_BLOCK, LANES))
                pv = _dot(p.astype(BF16), b["v"].astype(BF16))
                return m_new, l_new, jnp.where(head0, pv[:Q_BLOCK], pv[Q_BLOCK:])
            m_new = jnp.maximum(b["m"], row_max)
            alpha = jnp.exp2(b["m"] - m_new)
            p = jnp.exp2(s - jnp.concatenate([m_new, m_new], axis=1))
            l_new = alpha * b["l"] + jnp.sum(p, axis=-1, keepdims=True)
            pv = _dot(p.astype(BF16), b["v"].astype(BF16))
            acc_new = jnp.where(head0, alpha[:Q_BLOCK] * b["acc"] + pv[:Q_BLOCK],
                                alpha[Q_BLOCK:] * b["acc"] + pv[Q_BLOCK:])
            return m_new, l_new, acc_new

        def store(b, res):
            m_new, l_new, acc_new = res
            rows = b["rows"]
            acc_ref[rows, :] = acc_new
            m_ref[0, rows, :] = m_new[:Q_BLOCK]
            m_ref[1, rows, :] = m_new[Q_BLOCK:]
            l_ref[0, rows, :] = l_new[:Q_BLOCK]
            l_ref[1, rows, :] = l_new[Q_BLOCK:]

        def some_blocks(i, carry, load=load, compute=compute, store=store,
                        part=n_iter // A_BLOCKS_PER_ITER):
            blocks = [load(i + c * part) for c in range(A_BLOCKS_PER_ITER)]
            results = [compute(b) for b in blocks]
            for b, res in zip(blocks, results):
                store(b, res)
            return carry

        lax.fori_loop(0, n_iter // A_BLOCKS_PER_ITER, some_blocks, 0)

    lane_c = lax.broadcasted_iota(jnp.int32, (A_CHUNK, LANES), 1)
    denom = jnp.where(lane_c < HEAD_DIM, l_ref[0], l_ref[1])
    o_ref[0] = (acc_ref[...] / denom).astype(BF16)


def _dilated_attention(qa, ka, va, bias_a):
    b, s, wa = qa.shape
    n_pairs = wa // LANES
    cur = pl.BlockSpec((1, A_CHUNK, LANES), lambda i, c, p: (i, c, p))
    prev = pl.BlockSpec((1, A_CHUNK, LANES), lambda i, c, p: (i, jnp.maximum(c - 1, 0), p))
    nb = len(DILATED_BRANCHES)
    bias = pl.BlockSpec((nb, 2, 1, 2 * Q_BLOCK, 2 * Q_BLOCK), lambda i, c, p: (0, 0, p, 0, 0))
    return pl.pallas_call(
        _dilated_kernel,
        out_shape=jax.ShapeDtypeStruct((b, s, wa), BF16),
        grid=(b, s // A_CHUNK, n_pairs),
        in_specs=[cur, prev, cur, prev, cur, bias],
        out_specs=cur,
        scratch_shapes=[pltpu.VMEM((2 * A_CHUNK, LANES), F32),
                        pltpu.VMEM((2 * A_CHUNK, LANES), F32),
                        pltpu.VMEM((2, A_CHUNK, LANES), F32),
                        pltpu.VMEM((2, A_CHUNK, LANES), F32),
                        pltpu.VMEM((A_CHUNK, LANES), F32)],
        compiler_params=_cparams("parallel", "parallel", "parallel"),
        name="dilated_attention",
    )(qa, ka, ka, va, va, bias_a)


def _diff_kernel(n_bias, qt_ref, k_ref, vt_ref, bias_ref, lam_ref, gain_ref, o_ref,
                 q4_ref, m_ref, mt_ref, acc_ref, s_ref, p_ref):
    t = ATT_TILE
    row = lax.broadcasted_iota(jnp.int32, (LANES, t), 0)
    ones = jnp.ones((ONES_ROWS, 2 * t), BF16)

    def pipeline(g):
        qi = pl.program_id(2) * QUERY_TILES + g

        def tile_of(step):
            return jnp.clip(qi - step, 0, qi)

        def init():
            qt = qt_ref[0, g]
            for c in range(4):
                sel = jnp.logical_and(row >= c * DIFF_HALF, row < (c + 1) * DIFF_HALF)
                q4_ref[g, :, c * t:(c + 1) * t] = jnp.where(sel, qt, jnp.zeros_like(qt))
            m_ref[g] = jnp.full(m_ref.shape[1:], NEG_INF, F32)
            acc_ref[g] = jnp.zeros(acc_ref.shape[1:], F32)
            p_ref[g] = jnp.zeros(p_ref.shape[1:], BF16)

        def values(i):
            vt = jnp.concatenate([vt_ref[0, tile_of(2 * i)], vt_ref[0, tile_of(2 * i + 1)]], axis=1)
            return [_dot(jnp.concatenate([vt[h * HEAD_DIM:(h + 1) * HEAD_DIM], ones], axis=0),
                         p_ref[g, :, h * 2 * t:(h + 1) * 2 * t]) for h in range(2)]

        def logits(i):
            tile_max = None
            for half in range(2):
                step = 2 * i + half
                k = k_ref[0, pl.ds(pl.multiple_of(tile_of(step) * t, t), t), :]
                d = jnp.where(step > qi, n_bias, jnp.minimum(step, n_bias - 1))
                b0 = bias_ref[d, 0]
                b1 = bias_ref[d, 1]
                s = _dot(k, q4_ref[g]) + jnp.concatenate([b0, b0, b1, b1], axis=1)
                s_ref[g, half * t:(half + 1) * t] = s
                mx = jnp.max(s, axis=0, keepdims=True)
                tile_max = mx if tile_max is None else jnp.maximum(tile_max, mx)
            mt_ref[g] = tile_max

        def sweep(i):
            pv = values(i - 1)
            m_old = m_ref[g]
            m_new = jnp.maximum(m_old, mt_ref[g])
            alpha = jnp.exp2(m_old - m_new)
            m_ref[g] = m_new
            p_ref[g] = jnp.exp2(s_ref[g] - m_new).astype(BF16)
            for h in range(2):
                acc_ref[g, h] = alpha[:, h * 2 * t:(h + 1) * 2 * t] * (acc_ref[g, h] + pv[h])
            logits(i + 1)

        def fill():
            init()
            logits(0)

        def drain(n_iter):
            last = values(n_iter - 1)
            lam = lam_ref[...]
            outs = []
            for h in range(2):
                acc = acc_ref[g, h] + last[h]
                pv = acc[0:HEAD_DIM] / acc[HEAD_DIM:HEAD_DIM + 1]
                diff = pv[:, 0:t] - lam * pv[:, t:2 * t]
                ms = jnp.mean(diff * diff, axis=0, keepdims=True)
                outs.append(diff * lax.rsqrt(ms + EPS))
            out_t = jnp.concatenate(outs, axis=0) * gain_ref[...]
            o_ref[0, g * t:(g + 1) * t] = out_t.T.astype(BF16)

        return fill, sweep, drain

    stages = [pipeline(g) for g in range(QUERY_TILES)]
    for fill, _, _ in stages:
        fill()
    n_iter = (pl.program_id(2) * QUERY_TILES + QUERY_TILES + 1) // 2

    def body(i, c):
        for _, sweep, _ in stages:
            sweep(i)
        return c

    lax.fori_loop(0, n_iter, body, 0)
    for _, _, drain in stages:
        drain(n_iter)


def _diff_attention(qbt, kb, vbt, bias_bt, lam, post_gain):
    b, s, wb = kb.shape
    n_pairs = wb // LANES
    n_bias = bias_bt.shape[0] - 1
    t = ATT_TILE
    nt = s // t
    g = QUERY_TILES
    qspec = pl.BlockSpec((1, g, LANES, t), lambda i, p, j: (i, j, p, 0))
    kspec = pl.BlockSpec((1, s, LANES), lambda i, p, j: (i, 0, p))
    vspec = pl.BlockSpec((1, nt, LANES, t), lambda i, p, j: (i, 0, p, 0))
    bspec = pl.BlockSpec((n_bias + 1, 2, t, t), lambda i, p, j: (0, p, 0, 0))
    return pl.pallas_call(
        functools.partial(_diff_kernel, n_bias),
        out_shape=jax.ShapeDtypeStruct((b, s, wb), BF16),
        grid=(b, n_pairs, nt // g),
        in_specs=[qspec, kspec, vspec, bspec,
                  pl.BlockSpec((1, t), lambda i, p, j: (0, 0)),
                  pl.BlockSpec((LANES, t), lambda i, p, j: (0, 0))],
        out_specs=pl.BlockSpec((1, g * t, LANES), lambda i, p, j: (i, j, p)),
        scratch_shapes=[pltpu.VMEM((g, LANES, 4 * t), BF16),
                        pltpu.VMEM((g, 1, 4 * t), F32),
                        pltpu.VMEM((g, 1, 4 * t), F32),
                        pltpu.VMEM((g, 2, HEAD_DIM + ONES_ROWS, 2 * t), F32),
                        pltpu.VMEM((g, 2 * t, 4 * t), F32),
                        pltpu.VMEM((g, 2 * t, 4 * t), BF16)],
        compiler_params=_cparams("parallel", "parallel", "parallel"),
        name="diff_attention",
    )(qbt.reshape(b, nt, wb, t), kb, vbt.reshape(b, nt, wb, t), bias_bt, lam, post_gain)


def _stick_kernel(qt_ref, k_ref, vt_ref, tri_ref, o_ref, q2_ref, carry_ref, scale_ref, acc_ref,
                  z_ref, lw_ref, p_ref):
    t = ATT_TILE
    row = lax.broadcasted_iota(jnp.int32, (LANES, t), 0)

    def pipeline(g):
        qi = pl.program_id(2) * QUERY_TILES + g

        def tile_of(step):
            return jnp.clip(qi - step, 0, qi)

        def init():
            qt = qt_ref[0, g]
            zero = jnp.zeros_like(qt)
            q2_ref[g, :, 0:t] = jnp.where(row < HEAD_DIM, qt, zero)
            q2_ref[g, :, t:] = jnp.where(row < HEAD_DIM, zero, qt)
            carry_ref[g] = jnp.zeros(carry_ref.shape[1:], F32)
            scale_ref[g] = jnp.ones(scale_ref.shape[1:], F32)
            acc_ref[g] = jnp.zeros(acc_ref.shape[1:], F32)
            p_ref[g] = jnp.zeros(p_ref.shape[1:], BF16)

        def logits(i, diagonal=False):
            for half in range(2):
                k = k_ref[0, pl.ds(pl.multiple_of(tile_of(2 * i + half) * t, t), t), :]
                z = _dot(k, q2_ref[g])
                if diagonal and half == 0:
                    key = lax.broadcasted_iota(jnp.int32, (t, t), 0)
                    qry = lax.broadcasted_iota(jnp.int32, (t, t), 1)
                    strict = jnp.concatenate([key < qry, key < qry], axis=1)
                    z = jnp.where(strict, z, -SB_MASK * LOG2E)
                z_ref[g, half] = z

        def log_weights():
            for half in range(2):
                z = z_ref[g, half]
                neg_abs = pltpu.bitcast(pltpu.bitcast(z, jnp.uint32) | jnp.uint32(0x80000000), F32)
                sp = jnp.maximum(z, 0.0) + jnp.log(1.0 + jnp.exp2(neg_abs)) * LOG2E
                w = _dot(tri_ref[...], sp.astype(BF16))
                lw_ref[g, half, 0:t] = z + w[0:t]
                lw_ref[g, half, t:] = w[t:]

        def weights():
            carry = carry_ref[g]
            near_sum = lw_ref[g, 0, t:t + 1]
            p_ref[g, 0:t] = jnp.exp2(lw_ref[g, 0, 0:t]).astype(BF16)
            p_ref[g, t:] = jnp.exp2(lw_ref[g, 1, 0:t] + near_sum).astype(BF16)
            scale_ref[g] = jnp.exp2(carry)
            carry_ref[g] = carry + near_sum + lw_ref[g, 1, t:t + 1]

        def values(i):
            vts = []
            for half in range(2):
                step = 2 * i + half
                valid = jnp.logical_and(step >= 0, step <= qi)
                vt = vt_ref[0, tile_of(step)]
                vts.append(jnp.where(valid, vt, jnp.zeros_like(vt)))
            acc_ref[g] += _dot(jnp.concatenate(vts, axis=1), p_ref[g]) * scale_ref[g]

        def sweep(i):
            values(i - 1)
            weights()
            log_weights()
            logits(i + 2)

        def first_pair():
            init()
            logits(0, diagonal=True)
            log_weights()
            weights()

        def catch_up():
            logits(1)
            log_weights()
            logits(2)

        def drain(n_iter):
            values(n_iter - 1)
            out_t = jnp.concatenate([acc_ref[g, 0:HEAD_DIM, 0:t], acc_ref[g, HEAD_DIM:, t:]], axis=0)
            o_ref[0, g * t:(g + 1) * t] = out_t.T.astype(BF16)

        return first_pair, catch_up, sweep, drain

    stages = [pipeline(g) for g in range(QUERY_TILES)]
    for first_pair, _, _, _ in stages:
        first_pair()
    n_iter = (pl.program_id(2) * QUERY_TILES + QUERY_TILES + 1) // 2

    def live():
        return (jnp.max(jnp.exp2(carry_ref[...])) > 0.0).astype(jnp.int32)

    live_0 = live()

    @pl.when(live_0 > 0)
    def _():
        for _, catch_up, _, _ in stages:
            catch_up()

    def body(state):
        i, _ = state
        for _, _, sweep, _ in stages:
            sweep(i)
        return i + 1, live()

    n_done, _ = lax.while_loop(lambda st: jnp.logical_and(st[0] < n_iter, st[1] > 0), body,
                               (jnp.int32(1), live_0))
    for _, _, _, drain in stages:
        drain(n_done)


def _stick_attention(qct, kc, vct):
    b, s, wc = kc.shape
    n_pairs = wc // LANES
    t = ATT_TILE
    nt = s // t
    g = QUERY_TILES
    idx = np.arange(t)
    tri = np.concatenate([idx[None, :] >= idx[:, None], np.ones((ONES_ROWS, t), bool)], axis=0)
    tri = -jnp.asarray(tri, dtype=BF16)
    return pl.pallas_call(
        _stick_kernel,
        out_shape=jax.ShapeDtypeStruct((b, s, wc), BF16),
        grid=(b, n_pairs, nt // g),
        in_specs=[pl.BlockSpec((1, g, LANES, t), lambda i, p, j: (i, j, p, 0)),
                  pl.BlockSpec((1, s, LANES), lambda i, p, j: (i, 0, p)),
                  pl.BlockSpec((1, nt, LANES, t), lambda i, p, j: (i, 0, p, 0)),
                  pl.BlockSpec((t + ONES_ROWS, t), lambda i, p, j: (0, 0))],
        out_specs=pl.BlockSpec((1, g * t, LANES), lambda i, p, j: (i, j, p)),
        scratch_shapes=[pltpu.VMEM((g, LANES, 2 * t), BF16),
                        pltpu.VMEM((g, 1, 2 * t), F32),
                        pltpu.VMEM((g, 1, 2 * t), F32),
                        pltpu.VMEM((g, LANES, 2 * t), F32),
                        pltpu.VMEM((g, 2, t, 2 * t), F32),
                        pltpu.VMEM((g, 2, t + ONES_ROWS, 2 * t), F32),
                        pltpu.VMEM((g, 2 * t, 2 * t), BF16)],
        compiler_params=_cparams("parallel", "parallel", "parallel"),
        name="stick_breaking_attention",
    )(qct.reshape(b, nt, wc, t), kc, vct.reshape(b, nt, wc, t), tri)


def _out_proj_kernel(widths, n_ffn, x_ref, a_ref, b_ref, c_ref, w_ref, *rest):
    o_ref = rest[4 * n_ffn]
    wa, wb, wc = widths
    y = _dot(a_ref[...], w_ref[0:wa])
    y += _dot(b_ref[...], w_ref[wa:wa + wb])
    y += _dot(c_ref[...], w_ref[wa + wb:wa + wb + wc])
    x = x_ref[...] + y
    for f in range(n_ffn):
        g_ref, wg_ref, wu_ref, wd_ref = rest[4 * f:4 * f + 4]
        h = _rms(x, g_ref[...]).astype(BF16)
        gate = _dot(h, wg_ref[...])
        up = _dot(h, wu_ref[...])
        act = (gate * jax.nn.sigmoid(gate) * up).astype(BF16)
        x = x + 0.5 * _dot(act, wd_ref[...])
    o_ref[...] = x


def _out_proj(x2, oa, ob, oc, w_out, widths, ffns):
    t, d = x2.shape
    row = lambda w: pl.BlockSpec((ROW_TILE, w), lambda i: (i, 0))
    full = lambda a: pl.BlockSpec(a.shape, lambda i: (0, 0))
    ffn_args = []
    for gain, wg, wu, wd in ffns:
        ffn_args += [gain.reshape(1, d), wg, wu, wd]
    return pl.pallas_call(
        functools.partial(_out_proj_kernel, widths, len(ffns)),
        out_shape=jax.ShapeDtypeStruct((t, d), F32),
        grid=(t // ROW_TILE,),
        in_specs=[row(d), row(widths[0]), row(widths[1]), row(widths[2]), full(w_out)]
                 + [full(a) for a in ffn_args],
        out_specs=row(d),
        compiler_params=_cparams("parallel"),
        name="out_proj_residual_ffn",
    )(x2, oa, ob, oc, w_out, *ffn_args)


def _t5_bucket(dist):
    dist = jnp.maximum(dist, 0)
    max_exact = N_BUCKETS // 2
    d_f = jnp.maximum(dist, 1).astype(F32)
    large = max_exact + (jnp.log(d_f / max_exact) / math.log(MAX_DISTANCE / max_exact)
                         * (N_BUCKETS - max_exact)).astype(jnp.int32)
    large = jnp.minimum(large, N_BUCKETS - 1)
    return jnp.where(dist < max_exact, dist, large)


def _bias_of_distance(bias, dist):
    bucket = _t5_bucket(dist)[None]
    out = jnp.zeros((bias.shape[1],) + dist.shape, F32)
    for b in range(N_BUCKETS):
        out = jnp.where(bucket == b, bias[b].reshape((-1,) + (1,) * dist.ndim), out)
    return out


def _dilated_bias_table(bias_a):
    n_heads = bias_a.shape[1]
    i = jnp.arange(Q_BLOCK, dtype=jnp.int32)[:, None]
    j = jnp.arange(2 * Q_BLOCK, dtype=jnp.int32)[None, :]
    tables = []
    for window, dil in DILATED_BRANCHES:
        n = window // dil
        off = i + n - j
        band = (off >= 0) & (off <= n)
        bias = _bias_of_distance(bias_a, off * dil)
        variants = [jnp.where(valid[None], bias * LOG2E, NEG_INF)
                    for valid in (band, band & (j >= n))]
        tables.append(jnp.stack(variants))
    table = jnp.stack(tables)
    return table.reshape(len(DILATED_BRANCHES), 2, n_heads // 2, 2 * Q_BLOCK, 2 * Q_BLOCK)


def _diff_bias_table(bias_b, seq):
    t = ATT_TILE
    n_bias = min(seq // t, MAX_DISTANCE // t + 2)
    key = jnp.arange(t, dtype=jnp.int32)[None, :, None]
    qry = jnp.arange(t, dtype=jnp.int32)[None, None, :]
    dist = jnp.arange(n_bias + 1, dtype=jnp.int32)[:, None, None] * t + qry - key
    valid = (dist >= 0) & (jnp.arange(n_bias + 1)[:, None, None] < n_bias)
    tiles = jnp.where(valid[None], _bias_of_distance(bias_b, dist) * LOG2E, NEG_INF)
    return jnp.swapaxes(tiles, 0, 1)


def kernel(x, rel_bias, ffn1_norm, ffn1_w_gate, ffn1_w_up, ffn1_w_down, mix_norm, w_in,
           q_norm_a, k_norm_a, q_norm_b, k_norm_b, lambda_q1, lambda_k1, lambda_q2, lambda_k2,
           diff_subln, w_out, ffn2_norm, ffn2_w_gate, ffn2_w_up, ffn2_w_down):
    b, s, d = x.shape
    depth = w_in.shape[0]
    n_heads = d // HEAD_DIM
    wa = (n_heads // 2) * HEAD_DIM
    wb = (n_heads // 4) * HEAD_DIM
    wc = d - wa - wb
    widths = (wa, wb, wc)
    assert w_in.shape[2] == 3 * d and s % A_CHUNK == 0 and (b * s) % ROW_TILE == 0

    rb = rel_bias.astype(F32)
    bias_a = _dilated_bias_table(rb[:, :wa // HEAD_DIM])
    bias_bt = _diff_bias_table(rb[:, wa // HEAD_DIM:], s)

    def ffn1(layer):
        return (ffn1_norm[layer], ffn1_w_gate[layer].astype(BF16),
                ffn1_w_up[layer].astype(BF16), ffn1_w_down[layer].astype(BF16))

    def ffn2(layer):
        return (ffn2_norm[layer], ffn2_w_gate[layer].astype(BF16),
                ffn2_w_up[layer].astype(BF16), ffn2_w_down[layer].astype(BF16))

    x2 = _ffn(x.reshape(b * s, d), *ffn1(0))
    for layer in range(depth):
        qa, ka, va, kb, kc, qbt, vbt, qct, vct = _proj(
            x2, mix_norm[layer], w_in[layer], q_norm_a[layer], k_norm_a[layer],
            q_norm_b[layer], k_norm_b[layer], widths)
        seq3 = lambda t: t.reshape(b, s, t.shape[-1])

        out_a = _dilated_attention(seq3(qa), seq3(ka), seq3(va), bias_a)

        lam_init = 0.8 - 0.6 * math.exp(-0.3 * layer)
        lam = (jnp.exp(jnp.sum(lambda_q1[layer].astype(F32) * lambda_k1[layer].astype(F32)))
               - jnp.exp(jnp.sum(lambda_q2[layer].astype(F32) * lambda_k2[layer].astype(F32)))
               + lam_init)
        lam_row = jnp.full((1, ATT_TILE), lam, F32)
        post_gain = jnp.broadcast_to(
            (jnp.tile(diff_subln[layer].astype(F32), LANES // HEAD_DIM) * (1.0 - lam_init))[:, None],
            (LANES, ATT_TILE))
        out_b = _diff_attention(qbt, seq3(kb), vbt, bias_bt, lam_row, post_gain)

        out_c = _stick_attention(qct, seq3(kc), vct)

        ffns = [ffn2(layer)] + ([ffn1(layer + 1)] if layer + 1 < depth else [])
        x2 = _out_proj(x2, out_a.reshape(b * s, wa), out_b.reshape(b * s, wb),
                       out_c.reshape(b * s, wc), w_out[layer].astype(BF16), widths, ffns)
    return x2.reshape(b, s, d)
```

```python
import functools
import math

import jax
import jax.numpy as jnp
import numpy as np
from jax import lax
from jax.experimental import pallas as pl
from jax.experimental.pallas import tpu as pltpu

F32 = jnp.float32
BF16 = jnp.bfloat16

HEAD_DIM = 64
DIFF_HALF = HEAD_DIM // 2
N_BUCKETS = 32
MAX_DISTANCE = 2048
DILATED_BRANCHES = ((128, 1), (512, 4), (2048, 16))
Q_BLOCK = 128
EPS = 1e-6
NEG_INF = -1e30
SB_MASK = 1e4
LOG2E = math.log2(math.e)

LANES = 128
A_CHUNK = 2048
A_BLOCKS_PER_ITER = 16
ATT_TILE = 256
ROW_TILE = 512
ONES_ROWS = 16
QUERY_TILES = 2
VMEM_LIMIT = 56 * 1024 * 1024


def _cparams(*sem):
    return pltpu.CompilerParams(dimension_semantics=sem, vmem_limit_bytes=VMEM_LIMIT)


def _rms(x, gain_row):
    ms = jnp.mean(x * x, axis=-1, keepdims=True)
    return x * lax.rsqrt(ms + EPS) * gain_row


def _dot(a, b):
    return jnp.dot(a, b, preferred_element_type=F32)


def _dot_nt(a, b):
    return lax.dot_general(a, b, (((1,), (1,)), ((), ())), preferred_element_type=F32)


def _ffn_kernel(x_ref, g_ref, wg_ref, wu_ref, wd_ref, o_ref):
    x = x_ref[...]
    h = _rms(x, g_ref[...]).astype(BF16)
    gate = _dot(h, wg_ref[...])
    up = _dot(h, wu_ref[...])
    act = (gate * jax.nn.sigmoid(gate) * up).astype(BF16)
    o_ref[...] = x + 0.5 * _dot(act, wd_ref[...])


def _ffn(x2, gain, wg, wu, wd):
    t, d = x2.shape
    dff = wg.shape[1]
    row = pl.BlockSpec((ROW_TILE, d), lambda i: (i, 0))
    full = lambda shape: pl.BlockSpec(shape, lambda i: (0, 0))
    return pl.pallas_call(
        _ffn_kernel,
        out_shape=jax.ShapeDtypeStruct((t, d), F32),
        grid=(t // ROW_TILE,),
        in_specs=[row, full((1, d)), full((d, dff)), full((d, dff)), full((dff, d))],
        out_specs=row,
        compiler_params=_cparams("parallel"),
        name="ffn_half_step",
    )(x2, gain.reshape(1, d), wg, wu, wd)


def _proj_kernel(widths, x_ref, g_ref, wn_ref, wt_ref, gqa_ref, gka_ref, gkb_ref, gqb_ref,
                 g64_ref, g32_ref, qa_ref, ka_ref, va_ref, kb_ref, kc_ref,
                 qbt_ref, vbt_ref, qct_ref, vct_ref):
    wa, wb, wc = widths
    h = _rms(x_ref[...], g_ref[...]).astype(BF16)
    sub = 256

    def group_norm(t, ones_ref, gain_ref, group, scale):
        ss = _dot((t * t).astype(BF16), ones_ref[...])
        return t * lax.rsqrt(ss * (1.0 / group) + EPS) * (gain_ref[...] * scale)

    def group_norm_t(t, ones_ref, gain_ref, group, scale):
        ss = _dot(ones_ref[...], (t * t).astype(BF16))
        return t * lax.rsqrt(ss * (1.0 / group) + EPS) * (gain_ref[...] * scale)

    nat = _dot(h, wn_ref[...])
    col = 0
    for ref, width, post in (
            (qa_ref, wa, lambda t: group_norm(t, g64_ref, gqa_ref, HEAD_DIM,
                                              HEAD_DIM ** -0.5 * LOG2E)),
            (ka_ref, wa, lambda t: group_norm(t, g64_ref, gka_ref, HEAD_DIM, 1.0)),
            (va_ref, wa, lambda t: t),
            (kb_ref, wb, lambda t: group_norm(t, g32_ref, gkb_ref, DIFF_HALF, 1.0)),
            (kc_ref, wc, lambda t: t)):
        for c in range(0, width, sub):
            ref[:, c:c + sub] = post(nat[:, col + c:col + c + sub]).astype(ref.dtype)
        col += width

    tra = _dot_nt(wt_ref[...], h)
    row = 0
    for ref, width, post in (
            (qbt_ref, wb, lambda t: group_norm_t(t, g32_ref, gqb_ref, DIFF_HALF,
                                                 DIFF_HALF ** -0.5 * LOG2E)),
            (vbt_ref, wb, lambda t: t),
            (qct_ref, wc, lambda t: t * (HEAD_DIM ** -0.5 * LOG2E)),
            (vct_ref, wc, lambda t: t)):
        for r in range(0, width, sub):
            t = post(tra[row + r:row + r + sub]).astype(BF16)
            for j in range(ROW_TILE // ATT_TILE):
                ref[j, r:r + sub, :] = t[:, j * ATT_TILE:(j + 1) * ATT_TILE]
        row += width


def _block_diag_ones(group):
    idx = np.arange(256) // group
    return jnp.asarray(idx[:, None] == idx[None, :], dtype=BF16)


def _proj(x2, gain, w_in, gqa, gka, gqb, gkb, widths):
    t, d = x2.shape
    wa, wb, wc = widths
    assert wa % 256 == 0 and wb % 256 == 0 and wc % 256 == 0
    w = w_in.astype(BF16)
    o = np.cumsum([0, wa, wa, wa, wb, wb, wb, wc, wc, wc])
    sec = lambda i: w[:, o[i]:o[i + 1]]
    w_nat = jnp.concatenate([sec(0), sec(1), sec(2), sec(4), sec(7)], axis=1)
    w_tr = jnp.concatenate([sec(3), sec(5), sec(6), sec(8)], axis=1).T
    row = lambda wd: pl.BlockSpec((ROW_TILE, wd), lambda i: (i, 0))
    full = lambda shape: pl.BlockSpec(shape, lambda i: (0,) * len(shape))
    slab = lambda wd: pl.BlockSpec((ROW_TILE // ATT_TILE, wd, ATT_TILE), lambda i: (i, 0, 0))
    tile256 = lambda g: jnp.tile(g.astype(F32), 256 // g.shape[0])
    out_shape = ([jax.ShapeDtypeStruct((t, wa), F32)] * 3
                 + [jax.ShapeDtypeStruct((t, wb), BF16), jax.ShapeDtypeStruct((t, wc), BF16)]
                 + [jax.ShapeDtypeStruct((t // ATT_TILE, wd, ATT_TILE), BF16)
                    for wd in (wb, wb, wc, wc)])
    return pl.pallas_call(
        functools.partial(_proj_kernel, widths),
        out_shape=out_shape,
        grid=(t // ROW_TILE,),
        in_specs=[row(d), full((1, d)), full(w_nat.shape), full(w_tr.shape)]
                 + [full((1, 256))] * 3 + [full((256, 1))] + [full((256, 256))] * 2,
        out_specs=[row(wa)] * 3 + [row(wb), row(wc)] + [slab(wb), slab(wb), slab(wc), slab(wc)],
        compiler_params=_cparams("parallel"),
        name="norm_in_proj",
    )(x2, gain.reshape(1, d), w_nat, w_tr, tile256(gqa).reshape(1, 256), tile256(gka).reshape(1, 256),
      tile256(gkb).reshape(1, 256), tile256(gqb).reshape(256, 1),
      _block_diag_ones(HEAD_DIM), _block_diag_ones(DIFF_HALF))


def _dilated_kernel(q_ref, kp_ref, kc_ref, vp_ref, vc_ref, bias_ref, o_ref,
                    kk_ref, vv_ref, m_ref, l_ref, acc_ref):
    chunk = pl.program_id(1)
    kk_ref[0:A_CHUNK] = kp_ref[0]
    kk_ref[A_CHUNK:] = kc_ref[0]
    vv_ref[0:A_CHUNK] = vp_ref[0]
    vv_ref[A_CHUNK:] = vc_ref[0]

    lane = lax.broadcasted_iota(jnp.int32, (Q_BLOCK, LANES), 1)
    head0 = lane < HEAD_DIM

    order = sorted(range(len(DILATED_BRANCHES)), key=lambda b: -DILATED_BRANCHES[b][1])
    for bi in order:
        window, dil = DILATED_BRANCHES[bi]
        fresh = bi == order[0]
        nblk = A_CHUNK // (Q_BLOCK * dil)
        shift = int(math.log2(nblk))
        n_iter = dil * nblk
        ds = (lambda start, size, dil=dil:
              pl.ds(start, size, stride=dil) if dil > 1 else pl.ds(start, size))

        def load(idx, bi=bi, dil=dil, nblk=nblk, shift=shift, ds=ds, fresh=fresh):
            rho = idx >> shift
            t = idx & (nblk - 1)
            q_start = rho + t * (Q_BLOCK * dil)
            rows = ds(q_start, Q_BLOCK)
            krows = ds(A_CHUNK + q_start - Q_BLOCK * dil, 2 * Q_BLOCK)
            first = jnp.logical_and(chunk == 0, t == 0).astype(jnp.int32)
            blk = dict(rows=rows, q=q_ref[0, rows, :], k=kk_ref[krows, :], v=vv_ref[krows, :],
                       bias=bias_ref[bi, first, 0])
            if not fresh:
                blk.update(m=jnp.concatenate([m_ref[0, rows, :], m_ref[1, rows, :]], axis=0),
                           l=jnp.concatenate([l_ref[0, rows, :], l_ref[1, rows, :]], axis=0),
                           acc=acc_ref[rows, :])
            return blk

        def compute(b):
            q = b["q"]
            q2 = jnp.concatenate([jnp.where(head0, q, 0.0), jnp.where(head0, 0.0, q)],
                                 axis=0).astype(BF16)
            s = _dot_nt(q2, b["k"].astype(BF16)) + b["bias"]
            row_max = jnp.max(s, axis=-1, keepdims=True)
            if "m" not in b:
                m_new = jnp.broadcast_to(row_max, (2 * Q_BLOCK, LANES))
                p = jnp.exp2(s - row_max)
                l_new = jnp.broadcast_to(jnp.sum(p, axis=-1, keepdims=True), (2 * Q_BLOCK, LANES))
                pv = _dot(p.astype(BF16), b["v"].astype(BF16))
                return m_new, l_new, jnp.where(head0, pv[:Q_BLOCK], pv[Q_BLOCK:])
            m_new = jnp.maximum(b["m"], row_max)
            alpha = jnp.exp2(b["m"] - m_new)
            p = jnp.exp2(s - jnp.concatenate([m_new, m_new], axis=1))
            l_new = alpha * b["l"] + jnp.sum(p, axis=-1, keepdims=True)
            pv = _dot(p.astype(BF16), b["v"].astype(BF16))
            acc_new = jnp.where(head0, alpha[:Q_BLOCK] * b["acc"] + pv[:Q_BLOCK],
                                alpha[Q_BLOCK:] * b["acc"] + pv[Q_BLOCK:])
            return m_new, l_new, acc_new

        def store(b, res):
            m_new, l_new, acc_new = res
            rows = b["rows"]
            acc_ref[rows, :] = acc_new
            m_ref[0, rows, :] = m_new[:Q_BLOCK]
            m_ref[1, rows, :] = m_new[Q_BLOCK:]
            l_ref[0, rows, :] = l_new[:Q_BLOCK]
            l_ref[1, rows, :] = l_new[Q_BLOCK:]

        def some_blocks(i, carry, load=load, compute=compute, store=store,
                        part=n_iter // A_BLOCKS_PER_ITER):
            blocks = [load(i + c * part) for c in range(A_BLOCKS_PER_ITER)]
            results = [compute(b) for b in blocks]
            for b, res in zip(blocks, results):
                store(b, res)
            return carry

        lax.fori_loop(0, n_iter // A_BLOCKS_PER_ITER, some_blocks, 0)

    lane_c = lax.broadcasted_iota(jnp.int32, (A_CHUNK, LANES), 1)
    denom = jnp.where(lane_c < HEAD_DIM, l_ref[0], l_ref[1])
    o_ref[0] = (acc_ref[...] / denom).astype(BF16)


def _dilated_attention(qa, ka, va, bias_a):
    b, s, wa = qa.shape
    n_pairs = wa // LANES
    cur = pl.BlockSpec((1, A_CHUNK, LANES), lambda i, c, p: (i, c, p))
    prev = pl.BlockSpec((1, A_CHUNK, LANES), lambda i, c, p: (i, jnp.maximum(c - 1, 0), p))
    nb = len(DILATED_BRANCHES)
    bias = pl.BlockSpec((nb, 2, 1, 2 * Q_BLOCK, 2 * Q_BLOCK), lambda i, c, p: (0, 0, p, 0, 0))
    return pl.pallas_call(
        _dilated_kernel,
        out_shape=jax.ShapeDtypeStruct((b, s, wa), BF16),
        grid=(b, s // A_CHUNK, n_pairs),
        in_specs=[cur, prev, cur, prev, cur, bias],
        out_specs=cur,
        scratch_shapes=[pltpu.VMEM((2 * A_CHUNK, LANES), F32),
                        pltpu.VMEM((2 * A_CHUNK, LANES), F32),
                        pltpu.VMEM((2, A_CHUNK, LANES), F32),
                        pltpu.VMEM((2, A_CHUNK, LANES), F32),
                        pltpu.VMEM((A_CHUNK, LANES), F32)],
        compiler_params=_cparams("parallel", "parallel", "parallel"),
        name="dilated_attention",
    )(qa, ka, ka, va, va, bias_a)


def _diff_kernel(n_bias, qt_ref, k_ref, vt_ref, bias_ref, lam_ref, gain_ref, o_ref,
                 q4_ref, m_ref, mt_ref, acc_ref, s_ref, p_ref):
    t = ATT_TILE
    row = lax.broadcasted_iota(jnp.int32, (LANES, t), 0)
    ones = jnp.ones((ONES_ROWS, 2 * t), BF16)

    def pipeline(g):
        qi = pl.program_id(2) * QUERY_TILES + g

        def tile_of(step):
            return jnp.clip(qi - step, 0, qi)

        def init():
            qt = qt_ref[0, g]
            for c in range(4):
                sel = jnp.logical_and(row >= c * DIFF_HALF, row < (c + 1) * DIFF_HALF)
                q4_ref[g, :, c * t:(c + 1) * t] = jnp.where(sel, qt, jnp.zeros_like(qt))
            m_ref[g] = jnp.full(m_ref.shape[1:], NEG_INF, F32)
            acc_ref[g] = jnp.zeros(acc_ref.shape[1:], F32)
            p_ref[g] = jnp.zeros(p_ref.shape[1:], BF16)

        def values(i):
            vt = jnp.concatenate([vt_ref[0, tile_of(2 * i)], vt_ref[0, tile_of(2 * i + 1)]], axis=1)
            return [_dot(jnp.concatenate([vt[h * HEAD_DIM:(h + 1) * HEAD_DIM], ones], axis=0),
                         p_ref[g, :, h * 2 * t:(h + 1) * 2 * t]) for h in range(2)]

        def logits(i):
            tile_max = None
            for half in range(2):
                step = 2 * i + half
                k = k_ref[0, pl.ds(pl.multiple_of(tile_of(step) * t, t), t), :]
                d = jnp.where(step > qi, n_bias, jnp.minimum(step, n_bias - 1))
                b0 = bias_ref[d, 0]
                b1 = bias_ref[d, 1]
                s = _dot(k, q4_ref[g]) + jnp.concatenate([b0, b0, b1, b1], axis=1)
                s_ref[g, half * t:(half + 1) * t] = s
                mx = jnp.max(s, axis=0, keepdims=True)
                tile_max = mx if tile_max is None else jnp.maximum(tile_max, mx)
            mt_ref[g] = tile_max

        def sweep(i):
            pv = values(i - 1)
            m_old = m_ref[g]
            m_new = jnp.maximum(m_old, mt_ref[g])
            alpha = jnp.exp2(m_old - m_new)
            m_ref[g] = m_new
            p_ref[g] = jnp.exp2(s_ref[g] - m_new).astype(BF16)
            for h in range(2):
                acc_ref[g, h] = alpha[:, h * 2 * t:(h + 1) * 2 * t] * (acc_ref[g, h] + pv[h])
            logits(i + 1)

        def fill():
            init()
            logits(0)

        def drain(n_iter):
            last = values(n_iter - 1)
            lam = lam_ref[...]
            outs = []
            for h in range(2):
                acc = acc_ref[g, h] + last[h]
                pv = acc[0:HEAD_DIM] / acc[HEAD_DIM:HEAD_DIM + 1]
                diff = pv[:, 0:t] - lam * pv[:, t:2 * t]
                ms = jnp.mean(diff * diff, axis=0, keepdims=True)
                outs.append(diff * lax.rsqrt(ms + EPS))
            out_t = jnp.concatenate(outs, axis=0) * gain_ref[...]
            o_ref[0, g * t:(g + 1) * t] = out_t.T.astype(BF16)

        return fill, sweep, drain

    stages = [pipeline(g) for g in range(QUERY_TILES)]
    for fill, _, _ in stages:
        fill()
    n_iter = (pl.program_id(2) * QUERY_TILES + QUERY_TILES + 1) // 2

    def body(i, c):
        for _, sweep, _ in stages:
            sweep(i)
        return c

    lax.fori_loop(0, n_iter, body, 0)
    for _, _, drain in stages:
        drain(n_iter)


def _diff_attention(qbt, kb, vbt, bias_bt, lam, post_gain):
    b, s, wb = kb.shape
    n_pairs = wb // LANES
    n_bias = bias_bt.shape[0] - 1
    t = ATT_TILE
    nt = s // t
    g = QUERY_TILES
    qspec = pl.BlockSpec((1, g, LANES, t), lambda i, p, j: (i, j, p, 0))
    kspec = pl.BlockSpec((1, s, LANES), lambda i, p, j: (i, 0, p))
    vspec = pl.BlockSpec((1, nt, LANES, t), lambda i, p, j: (i, 0, p, 0))
    bspec = pl.BlockSpec((n_bias + 1, 2, t, t), lambda i, p, j: (0, p, 0, 0))
    return pl.pallas_call(
        functools.partial(_diff_kernel, n_bias),
        out_shape=jax.ShapeDtypeStruct((b, s, wb), BF16),
        grid=(b, n_pairs, nt // g),
        in_specs=[qspec, kspec, vspec, bspec,
                  pl.BlockSpec((1, t), lambda i, p, j: (0, 0)),
                  pl.BlockSpec((LANES, t), lambda i, p, j: (0, 0))],
        out_specs=pl.BlockSpec((1, g * t, LANES), lambda i, p, j: (i, j, p)),
        scratch_shapes=[pltpu.VMEM((g, LANES, 4 * t), BF16),
                        pltpu.VMEM((g, 1, 4 * t), F32),
                        pltpu.VMEM((g, 1, 4 * t), F32),
                        pltpu.VMEM((g, 2, HEAD_DIM + ONES_ROWS, 2 * t), F32),
                        pltpu.VMEM((g, 2 * t, 4 * t), F32),
                        pltpu.VMEM((g, 2 * t, 4 * t), BF16)],
        compiler_params=_cparams("parallel", "parallel", "parallel"),
        name="diff_attention",
    )(qbt.reshape(b, nt, wb, t), kb, vbt.reshape(b, nt, wb, t), bias_bt, lam, post_gain)


def _stick_kernel(qt_ref, k_ref, vt_ref, tri_ref, o_ref, q2_ref, carry_ref, scale_ref, acc_ref,
                  z_ref, lw_ref, p_ref):
    t = ATT_TILE
    row = lax.broadcasted_iota(jnp.int32, (LANES, t), 0)

    def pipeline(g):
        qi = pl.program_id(2) * QUERY_TILES + g

        def tile_of(step):
            return jnp.clip(qi - step, 0, qi)

        def init():
            qt = qt_ref[0, g]
            zero = jnp.zeros_like(qt)
            q2_ref[g, :, 0:t] = jnp.where(row < HEAD_DIM, qt, zero)
            q2_ref[g, :, t:] = jnp.where(row < HEAD_DIM, zero, qt)
            carry_ref[g] = jnp.zeros(carry_ref.shape[1:], F32)
            acc_ref[g] = jnp.zeros(acc_ref.shape[1:], F32)

        def logits(i, diagonal=False):
            for half in range(2):
                k = k_ref[0, pl.ds(pl.multiple_of(tile_of(2 * i + half) * t, t), t), :]
                z = _dot(k, q2_ref[g])
                if diagonal and half == 0:
                    key = lax.broadcasted_iota(jnp.int32, (t, t), 0)
                    qry = lax.broadcasted_iota(jnp.int32, (t, t), 1)
                    strict = jnp.concatenate([key < qry, key < qry], axis=1)
                    z = jnp.where(strict, z, -SB_MASK * LOG2E)
                z_ref[g, half] = z

        def log_weights():
            for half in range(2):
                z = z_ref[g, half]
                neg_abs = pltpu.bitcast(pltpu.bitcast(z, jnp.uint32) | jnp.uint32(0x80000000), F32)
                sp = jnp.maximum(z, 0.0) + jnp.log(1.0 + jnp.exp2(neg_abs)) * LOG2E
                w = _dot(tri_ref[...], sp.astype(BF16))
                lw_ref[g, half, 0:t] = z + w[0:t]
                lw_ref[g, half, t:] = w[t:]

        def weights():
            carry = carry_ref[g]
            near_sum = lw_ref[g, 0, t:t + 1]
            p_ref[g, 0:t] = jnp.exp2(lw_ref[g, 0, 0:t]).astype(BF16)
            p_ref[g, t:] = jnp.exp2(lw_ref[g, 1, 0:t] + near_sum).astype(BF16)
            scale_ref[g] = jnp.exp2(carry)
            carry_ref[g] = carry + near_sum + lw_ref[g, 1, t:t + 1]

        def values(i):
            vts = []
            for half in range(2):
                step = 2 * i + half
                valid = jnp.logical_and(step >= 0, step <= qi)
                vt = vt_ref[0, tile_of(step)]
                vts.append(jnp.where(valid, vt, jnp.zeros_like(vt)))
            acc_ref[g] += _dot(jnp.concatenate(vts, axis=1), p_ref[g]) * scale_ref[g]

        def sweep(i):
            values(i - 1)
            weights()
            log_weights()
            logits(i + 2)

        def first_pair():
            init()
            logits(0, diagonal=True)
            log_weights()
            weights()

        def catch_up():
            logits(1)
            log_weights()
            logits(2)

        def drain(n_iter):
            values(n_iter - 1)
            out_t = jnp.concatenate([acc_ref[g, 0:HEAD_DIM, 0:t], acc_ref[g, HEAD_DIM:, t:]], axis=0)
            o_ref[0, g * t:(g + 1) * t] = out_t.T.astype(BF16)

        return first_pair, catch_up, sweep, drain

    stages = [pipeline(g) for g in range(QUERY_TILES)]
    for first_pair, _, _, _ in stages:
        first_pair()
    n_iter = (pl.program_id(2) * QUERY_TILES + QUERY_TILES + 1) // 2

    def live():
        return (jnp.max(jnp.exp2(carry_ref[...])) > 0.0).astype(jnp.int32)

    live_0 = live()

    @pl.when(live_0 > 0)
    def _():
        for _, catch_up, _, _ in stages:
            catch_up()

    def body(state):
        i, _ = state
        for _, _, sweep, _ in stages:
            sweep(i)
        return i + 1, live()

    n_done, _ = lax.while_loop(lambda st: jnp.logical_and(st[0] < n_iter, st[1] > 0), body,
                               (jnp.int32(1), live_0))
    for _, _, _, drain in stages:
        drain(n_done)


def _stick_attention(qct, kc, vct):
    b, s, wc = kc.shape
    n_pairs = wc // LANES
    t = ATT_TILE
    nt = s // t
    g = QUERY_TILES
    idx = np.arange(t)
    tri = np.concatenate([idx[None, :] >= idx[:, None], np.ones((ONES_ROWS, t), bool)], axis=0)
    tri = -jnp.asarray(tri, dtype=BF16)
    return pl.pallas_call(
        _stick_kernel,
        out_shape=jax.ShapeDtypeStruct((b, s, wc), BF16),
        grid=(b, n_pairs, nt // g),
        in_specs=[pl.BlockSpec((1, g, LANES, t), lambda i, p, j: (i, j, p, 0)),
                  pl.BlockSpec((1, s, LANES), lambda i, p, j: (i, 0, p)),
                  pl.BlockSpec((1, nt, LANES, t), lambda i, p, j: (i, 0, p, 0)),
                  pl.BlockSpec((t + ONES_ROWS, t), lambda i, p, j: (0, 0))],
        out_specs=pl.BlockSpec((1, g * t, LANES), lambda i, p, j: (i, j, p)),
        scratch_shapes=[pltpu.VMEM((g, LANES, 2 * t), BF16),
                        pltpu.VMEM((g, 1, 2 * t), F32),
                        pltpu.VMEM((g, 1, 2 * t), F32),
                        pltpu.VMEM((g, LANES, 2 * t), F32),
                        pltpu.VMEM((g, 2, t, 2 * t), F32),
                        pltpu.VMEM((g, 2, t + ONES_ROWS, 2 * t), F32),
                        pltpu.VMEM((g, 2 * t, 2 * t), BF16)],
        compiler_params=_cparams("parallel", "parallel", "parallel"),
        name="stick_breaking_attention",
    )(qct.reshape(b, nt, wc, t), kc, vct.reshape(b, nt, wc, t), tri)


def _out_proj_kernel(widths, n_ffn, x_ref, a_ref, b_ref, c_ref, w_ref, *rest):
    o_ref = rest[4 * n_ffn]
    wa, wb, wc = widths
    y = _dot(a_ref[...], w_ref[0:wa])
    y += _dot(b_ref[...], w_ref[wa:wa + wb])
    y += _dot(c_ref[...], w_ref[wa + wb:wa + wb + wc])
    x = x_ref[...] + y
    for f in range(n_ffn):
        g_ref, wg_ref, wu_ref, wd_ref = rest[4 * f:4 * f + 4]
        h = _rms(x, g_ref[...]).astype(BF16)
        gate = _dot(h, wg_ref[...])
        up = _dot(h, wu_ref[...])
        act = (gate * jax.nn.sigmoid(gate) * up).astype(BF16)
        x = x + 0.5 * _dot(act, wd_ref[...])
    o_ref[...] = x


def _out_proj(x2, oa, ob, oc, w_out, widths, ffns):
    t, d = x2.shape
    row = lambda w: pl.BlockSpec((ROW_TILE, w), lambda i: (i, 0))
    full = lambda a: pl.BlockSpec(a.shape, lambda i: (0, 0))
    ffn_args = []
    for gain, wg, wu, wd in ffns:
        ffn_args += [gain.reshape(1, d), wg, wu, wd]
    return pl.pallas_call(
        functools.partial(_out_proj_kernel, widths, len(ffns)),
        out_shape=jax.ShapeDtypeStruct((t, d), F32),
        grid=(t // ROW_TILE,),
        in_specs=[row(d), row(widths[0]), row(widths[1]), row(widths[2]), full(w_out)]
                 + [full(a) for a in ffn_args],
        out_specs=row(d),
        compiler_params=_cparams("parallel"),
        name="out_proj_residual_ffn",
    )(x2, oa, ob, oc, w_out, *ffn_args)


def _t5_bucket(dist):
    dist = jnp.maximum(dist, 0)
    max_exact = N_BUCKETS // 2
    d_f = jnp.maximum(dist, 1).astype(F32)
    large = max_exact + (jnp.log(d_f / max_exact) / math.log(MAX_DISTANCE / max_exact)
                         * (N_BUCKETS - max_exact)).astype(jnp.int32)
    large = jnp.minimum(large, N_BUCKETS - 1)
    return jnp.where(dist < max_exact, dist, large)


def _bias_of_distance(bias, dist):
    bucket = _t5_bucket(dist)[None]
    out = jnp.zeros((bias.shape[1],) + dist.shape, F32)
    for b in range(N_BUCKETS):
        out = jnp.where(bucket == b, bias[b].reshape((-1,) + (1,) * dist.ndim), out)
    return out


def _dilated_bias_table(bias_a):
    n_heads = bias_a.shape[1]
    i = jnp.arange(Q_BLOCK, dtype=jnp.int32)[:, None]
    j = jnp.arange(2 * Q_BLOCK, dtype=jnp.int32)[None, :]
    tables = []
    for window, dil in DILATED_BRANCHES:
        n = window // dil
        off = i + n - j
        band = (off >= 0) & (off <= n)
        bias = _bias_of_distance(bias_a, off * dil)
        variants = [jnp.where(valid[None], bias * LOG2E, NEG_INF)
                    for valid in (band, band & (j >= n))]
        tables.append(jnp.stack(variants))
    table = jnp.stack(tables)
    return table.reshape(len(DILATED_BRANCHES), 2, n_heads // 2, 2 * Q_BLOCK, 2 * Q_BLOCK)


def _diff_bias_table(bias_b, seq):
    t = ATT_TILE
    n_bias = min(seq // t, MAX_DISTANCE // t + 2)
    key = jnp.arange(t, dtype=jnp.int32)[None, :, None]
    qry = jnp.arange(t, dtype=jnp.int32)[None, None, :]
    dist = jnp.arange(n_bias + 1, dtype=jnp.int32)[:, None, None] * t + qry - key
    valid = (dist >= 0) & (jnp.arange(n_bias + 1)[:, None, None] < n_bias)
    tiles = jnp.where(valid[None], _bias_of_distance(bias_b, dist) * LOG2E, NEG_INF)
    return jnp.swapaxes(tiles, 0, 1)


def kernel(x, rel_bias, ffn1_norm, ffn1_w_gate, ffn1_w_up, ffn1_w_down, mix_norm, w_in,
           q_norm_a, k_norm_a, q_norm_b, k_norm_b, lambda_q1, lambda_k1, lambda_q2, lambda_k2,
           diff_subln, w_out, ffn2_norm, ffn2_w_gate, ffn2_w_up, ffn2_w_down):
    b, s, d = x.shape
    depth = w_in.shape[0]
    n_heads = d // HEAD_DIM
    wa = (n_heads // 2) * HEAD_DIM
    wb = (n_heads // 4) * HEAD_DIM
    wc = d - wa - wb
    widths = (wa, wb, wc)
    assert w_in.shape[2] == 3 * d and s % A_CHUNK == 0 and (b * s) % ROW_TILE == 0

    rb = rel_bias.astype(F32)
    bias_a = _dilated_bias_table(rb[:, :wa // HEAD_DIM])
    bias_bt = _diff_bias_table(rb[:, wa // HEAD_DIM:], s)

    def ffn1(layer):
        return (ffn1_norm[layer], ffn1_w_gate[layer].astype(BF16),
                ffn1_w_up[layer].astype(BF16), ffn1_w_down[layer].astype(BF16))

    def ffn2(layer):
        return (ffn2_norm[layer], ffn2_w_gate[layer].astype(BF16),
                ffn2_w_up[layer].astype(BF16), ffn2_w_down[layer].astype(BF16))

    x2 = _ffn(x.reshape(b * s, d), *ffn1(0))
    for layer in range(depth):
        qa, ka, va, kb, kc, qbt, vbt, qct, vct = _proj(
            x2, mix_norm[layer], w_in[layer], q_norm_a[layer], k_norm_a[layer],
            q_norm_b[layer], k_norm_b[layer], widths)
        seq3 = lambda t: t.reshape(b, s, t.shape[-1])

        out_a = _dilated_attention(seq3(qa), seq3(ka), seq3(va), bias_a)

        lam_init = 0.8 - 0.6 * math.exp(-0.3 * layer)
        lam = (jnp.exp(jnp.sum(lambda_q1[layer].astype(F32) * lambda_k1[layer].astype(F32)))
               - jnp.exp(jnp.sum(lambda_q2[layer].astype(F32) * lambda_k2[layer].astype(F32)))
               + lam_init)
        lam_row = jnp.full((1, ATT_TILE), lam, F32)
        post_gain = jnp.broadcast_to(
            (jnp.tile(diff_subln[layer].astype(F32), LANES // HEAD_DIM) * (1.0 - lam_init))[:, None],
            (LANES, ATT_TILE))
        out_b = _diff_attention(qbt, seq3(kb), vbt, bias_bt, lam_row, post_gain)

        out_c = _stick_attention(qct, seq3(kc), vct)

        ffns = [ffn2(layer)] + ([ffn1(layer + 1)] if layer + 1 < depth else [])
        x2 = _out_proj(x2, out_a.reshape(b * s, wa), out_b.reshape(b * s, wb),
                       out_c.reshape(b * s, wc), w_out[layer].astype(BF16), widths, ffns)
    return x2.reshape(b, s, d)
```

```python
import functools
import math

import jax
import jax.numpy as jnp
import numpy as np
from jax import lax
from jax.experimental import pallas as pl
from jax.experimental.pallas import tpu as pltpu

F32 = jnp.float32
BF16 = jnp.bfloat16

HEAD_DIM = 64
DIFF_HALF = HEAD_DIM // 2
N_BUCKETS = 32
MAX_DISTANCE = 2048
DILATED_BRANCHES = ((128, 1), (512, 4), (2048, 16))
Q_BLOCK = 128
EPS = 1e-6
NEG_INF = -1e30
SB_MASK = 1e4
LOG2E = math.log2(math.e)

LANES = 128
A_CHUNK = 2048
A_BLOCKS_PER_ITER = 16
ATT_TILE = 256
ROW_TILE = 512
ONES_ROWS = 16
QUERY_TILES = 2
VMEM_LIMIT = 56 * 1024 * 1024


def _cparams(*sem):
    return pltpu.CompilerParams(dimension_semantics=sem, vmem_limit_bytes=VMEM_LIMIT)


def _rms(x, gain_row):
    ms = jnp.mean(x * x, axis=-1, keepdims=True)
    return x * lax.rsqrt(ms + EPS) * gain_row


def _dot(a, b):
    return jnp.dot(a, b, preferred_element_type=F32)


def _dot_nt(a, b):
    return lax.dot_general(a, b, (((1,), (1,)), ((), ())), preferred_element_type=F32)


def _ffn_kernel(x_ref, g_ref, wg_ref, wu_ref, wd_ref, o_ref):
    x = x_ref[...]
    h = _rms(x, g_ref[...]).astype(BF16)
    gate = _dot(h, wg_ref[...])
    up = _dot(h, wu_ref[...])
    act = (gate * jax.nn.sigmoid(gate) * up).astype(BF16)
    o_ref[...] = x + 0.5 * _dot(act, wd_ref[...])


def _ffn(x2, gain, wg, wu, wd):
    t, d = x2.shape
    dff = wg.shape[1]
    row = pl.BlockSpec((ROW_TILE, d), lambda i: (i, 0))
    full = lambda shape: pl.BlockSpec(shape, lambda i: (0, 0))
    return pl.pallas_call(
        _ffn_kernel,
        out_shape=jax.ShapeDtypeStruct((t, d), F32),
        grid=(t // ROW_TILE,),
        in_specs=[row, full((1, d)), full((d, dff)), full((d, dff)), full((dff, d))],
        out_specs=row,
        compiler_params=_cparams("parallel"),
        name="ffn_half_step",
    )(x2, gain.reshape(1, d), wg, wu, wd)


def _proj_kernel(widths, x_ref, g_ref, wn_ref, wt_ref, gqa_ref, gka_ref, gkb_ref, gqb_ref,
                 g64_ref, g32_ref, qa_ref, ka_ref, va_ref, kb_ref, kc_ref,
                 qbt_ref, vbt_ref, qct_ref, vct_ref):
    wa, wb, wc = widths
    h = _rms(x_ref[...], g_ref[...]).astype(BF16)
    sub = 256

    def group_norm(t, ones_ref, gain_ref, group, scale):
        ss = _dot((t * t).astype(BF16), ones_ref[...])
        return t * lax.rsqrt(ss * (1.0 / group) + EPS) * (gain_ref[...] * scale)

    def group_norm_t(t, ones_ref, gain_ref, group, scale):
        ss = _dot(ones_ref[...], (t * t).astype(BF16))
        return t * lax.rsqrt(ss * (1.0 / group) + EPS) * (gain_ref[...] * scale)

    nat = _dot(h, wn_ref[...])
    col = 0
    for ref, width, post in (
            (qa_ref, wa, lambda t: group_norm(t, g64_ref, gqa_ref, HEAD_DIM,
                                              HEAD_DIM ** -0.5 * LOG2E)),
            (ka_ref, wa, lambda t: group_norm(t, g64_ref, gka_ref, HEAD_DIM, 1.0)),
            (va_ref, wa, lambda t: t),
            (kb_ref, wb, lambda t: group_norm(t, g32_ref, gkb_ref, DIFF_HALF, 1.0)),
            (kc_ref, wc, lambda t: t)):
        for c in range(0, width, sub):
            ref[:, c:c + sub] = post(nat[:, col + c:col + c + sub]).astype(ref.dtype)
        col += width

    tra = _dot_nt(wt_ref[...], h)
    row = 0
    for ref, width, post in (
            (qbt_ref, wb, lambda t: group_norm_t(t, g32_ref, gqb_ref, DIFF_HALF,
                                                 DIFF_HALF ** -0.5 * LOG2E)),
            (vbt_ref, wb, lambda t: t),
            (qct_ref, wc, lambda t: t * (HEAD_DIM ** -0.5 * LOG2E)),
            (vct_ref, wc, lambda t: t)):
        for r in range(0, width, sub):
            t = post(tra[row + r:row + r + sub]).astype(BF16)
            for j in range(ROW_TILE // ATT_TILE):
                ref[j, r:r + sub, :] = t[:, j * ATT_TILE:(j + 1) * ATT_TILE]
        row += width


def _block_diag_ones(group):
    idx = np.arange(256) // group
    return jnp.asarray(idx[:, None] == idx[None, :], dtype=BF16)


def _proj(x2, gain, w_in, gqa, gka, gqb, gkb, widths):
    t, d = x2.shape
    wa, wb, wc = widths
    assert wa % 256 == 0 and wb % 256 == 0 and wc % 256 == 0
    w = w_in.astype(BF16)
    o = np.cumsum([0, wa, wa, wa, wb, wb, wb, wc, wc, wc])
    sec = lambda i: w[:, o[i]:o[i + 1]]
    w_nat = jnp.concatenate([sec(0), sec(1), sec(2), sec(4), sec(7)], axis=1)
    w_tr = jnp.concatenate([sec(3), sec(5), sec(6), sec(8)], axis=1).T
    row = lambda wd: pl.BlockSpec((ROW_TILE, wd), lambda i: (i, 0))
    full = lambda shape: pl.BlockSpec(shape, lambda i: (0,) * len(shape))
    slab = lambda wd: pl.BlockSpec((ROW_TILE // ATT_TILE, wd, ATT_TILE), lambda i: (i, 0, 0))
    tile256 = lambda g: jnp.tile(g.astype(F32), 256 // g.shape[0])
    out_shape = ([jax.ShapeDtypeStruct((t, wa), F32)] * 3
                 + [jax.ShapeDtypeStruct((t, wb), BF16), jax.ShapeDtypeStruct((t, wc), BF16)]
                 + [jax.ShapeDtypeStruct((t // ATT_TILE, wd, ATT_TILE), BF16)
                    for wd in (wb, wb, wc, wc)])
    return pl.pallas_call(
        functools.partial(_proj_kernel, widths),
        out_shape=out_shape,
        grid=(t // ROW_TILE,),
        in_specs=[row(d), full((1, d)), full(w_nat.shape), full(w_tr.shape)]
                 + [full((1, 256))] * 3 + [full((256, 1))] + [full((256, 256))] * 2,
        out_specs=[row(wa)] * 3 + [row(wb), row(wc)] + [slab(wb), slab(wb), slab(wc), slab(wc)],
        compiler_params=_cparams("parallel"),
        name="norm_in_proj",
    )(x2, gain.reshape(1, d), w_nat, w_tr, tile256(gqa).reshape(1, 256), tile256(gka).reshape(1, 256),
      tile256(gkb).reshape(1, 256), tile256(gqb).reshape(256, 1),
      _block_diag_ones(HEAD_DIM), _block_diag_ones(DIFF_HALF))


def _dilated_kernel(q_ref, kp_ref, kc_ref, vp_ref, vc_ref, bias_ref, o_ref,
                    kk_ref, vv_ref, m_ref, l_ref, acc_ref):
    chunk = pl.program_id(1)
    kk_ref[0:A_CHUNK] = kp_ref[0]
    kk_ref[A_CHUNK:] = kc_ref[0]
    vv_ref[0:A_CHUNK] = vp_ref[0]
    vv_ref[A_CHUNK:] = vc_ref[0]

    lane = lax.broadcasted_iota(jnp.int32, (Q_BLOCK, LANES), 1)
    head0 = lane < HEAD_DIM

    order = sorted(range(len(DILATED_BRANCHES)), key=lambda b: -DILATED_BRANCHES[b][1])
    for bi in order:
        window, dil = DILATED_BRANCHES[bi]
        fresh = bi == order[0]
        nblk = A_CHUNK // (Q_BLOCK * dil)
        shift = int(math.log2(nblk))
        n_iter = dil * nblk
        ds = (lambda start, size, dil=dil:
              pl.ds(start, size, stride=dil) if dil > 1 else pl.ds(start, size))

        def load(idx, bi=bi, dil=dil, nblk=nblk, shift=shift, ds=ds, fresh=fresh):
            rho = idx >> shift
            t = idx & (nblk - 1)
            q_start = rho + t * (Q_BLOCK * dil)
            rows = ds(q_start, Q_BLOCK)
            krows = ds(A_CHUNK + q_start - Q_BLOCK * dil, 2 * Q_BLOCK)
            first = jnp.logical_and(chunk == 0, t == 0).astype(jnp.int32)
            blk = dict(rows=rows, q=q_ref[0, rows, :], k=kk_ref[krows, :], v=vv_ref[krows, :],
                       bias=bias_ref[bi, first, 0])
            if not fresh:
                blk.update(m=jnp.concatenate([m_ref[0, rows, :], m_ref[1, rows, :]], axis=0),
                           l=jnp.concatenate([l_ref[0, rows, :], l_ref[1, rows, :]], axis=0),
                           acc=acc_ref[rows, :])
            return blk

        def compute(b):
            q = b["q"]
            q2 = jnp.concatenate([jnp.where(head0, q, 0.0), jnp.where(head0, 0.0, q)],
                                 axis=0).astype(BF16)
            s = _dot_nt(q2, b["k"].astype(BF16)) + b["bias"]
            row_max = jnp.max(s, axis=-1, keepdims=True)
            if "m" not in b:
                m_new = jnp.broadcast_to(row_max, (2 * Q_BLOCK, LANES))
                p = jnp.exp2(s - row_max)
                l_new = jnp.broadcast_to(jnp.sum(p, axis=-1, keepdims=True), (2 * Q_BLOCK, LANES))
                pv = _dot(p.astype(BF16), b["v"].astype(BF16))
                return m_new, l_new, jnp.where(head0, pv[:Q_BLOCK], pv[Q_BLOCK:])
            m_new = jnp.maximum(b["m"], row_max)
            alpha = jnp.exp2(b["m"] - m_new)
            p = jnp.exp2(s - jnp.concatenate([m_new, m_new], axis=1))
            l_new = alpha * b["l"] + jnp.sum(p, axis=-1, keepdims=True)
            pv = _dot(p.astype(BF16), b["v"].astype(BF16))
            acc_new = jnp.where(head0, alpha[:Q_BLOCK] * b["acc"] + pv[:Q_BLOCK],
                                alpha[Q_BLOCK:] * b["acc"] + pv[Q_BLOCK:])
            return m_new, l_new, acc_new

        def store(b, res):
            m_new, l_new, acc_new = res
            rows = b["rows"]
            acc_ref[rows, :] = acc_new
            m_ref[0, rows, :] = m_new[:Q_BLOCK]
            m_ref[1, rows, :] = m_new[Q_BLOCK:]
            l_ref[0, rows, :] = l_new[:Q_BLOCK]
            l_ref[1, rows, :] = l_new[Q_BLOCK:]

        def some_blocks(i, carry, load=load, compute=compute, store=store,
                        part=n_iter // A_BLOCKS_PER_ITER):
            blocks = [load(i + c * part) for c in range(A_BLOCKS_PER_ITER)]
            results = [compute(b) for b in blocks]
            for b, res in zip(blocks, results):
                store(b, res)
            return carry

        lax.fori_loop(0, n_iter // A_BLOCKS_PER_ITER, some_blocks, 0)

    lane_c = lax.broadcasted_iota(jnp.int32, (A_CHUNK, LANES), 1)
    denom = jnp.where(lane_c < HEAD_DIM, l_ref[0], l_ref[1])
    o_ref[0] = (acc_ref[...] / denom).astype(BF16)


def _dilated_attention(qa, ka, va, bias_a):
    b, s, wa = qa.shape
    n_pairs = wa // LANES
    cur = pl.BlockSpec((1, A_CHUNK, LANES), lambda i, c, p: (i, c, p))
    prev = pl.BlockSpec((1, A_CHUNK, LANES), lambda i, c, p: (i, jnp.maximum(c - 1, 0), p))
    nb = len(DILATED_BRANCHES)
    bias = pl.BlockSpec((nb, 2, 1, 2 * Q_BLOCK, 2 * Q_BLOCK), lambda i, c, p: (0, 0, p, 0, 0))
    return pl.pallas_call(
        _dilated_kernel,
        out_shape=jax.ShapeDtypeStruct((b, s, wa), BF16),
        grid=(b, s // A_CHUNK, n_pairs),
        in_specs=[cur, prev, cur, prev, cur, bias],
        out_specs=cur,
        scratch_shapes=[pltpu.VMEM((2 * A_CHUNK, LANES), F32),
                        pltpu.VMEM((2 * A_CHUNK, LANES), F32),
                        pltpu.VMEM((2, A_CHUNK, LANES), F32),
                        pltpu.VMEM((2, A_CHUNK, LANES), F32),
                        pltpu.VMEM((A_CHUNK, LANES), F32)],
        compiler_params=_cparams("parallel", "parallel", "parallel"),
        name="dilated_attention",
    )(qa, ka, ka, va, va, bias_a)


def _diff_kernel(n_bias, qt_ref, k_ref, vt_ref, bias_ref, lam_ref, gain_ref, o_ref,
                 q4_ref, m_ref, mt_ref, acc_ref, s_ref, p_ref):
    t = ATT_TILE
    row = lax.broadcasted_iota(jnp.int32, (LANES, t), 0)
    ones = jnp.ones((ONES_ROWS, 2 * t), BF16)

    def pipeline(g):
        qi = pl.program_id(2) * QUERY_TILES + g

        def tile_of(step):
            return jnp.clip(qi - step, 0, qi)

        def init():
            qt = qt_ref[0, g]
            for c in range(4):
                sel = jnp.logical_and(row >= c * DIFF_HALF, row < (c + 1) * DIFF_HALF)
                q4_ref[g, :, c * t:(c + 1) * t] = jnp.where(sel, qt, jnp.zeros_like(qt))
            m_ref[g] = jnp.full(m_ref.shape[1:], NEG_INF, F32)
            acc_ref[g] = jnp.zeros(acc_ref.shape[1:], F32)
            p_ref[g] = jnp.zeros(p_ref.shape[1:], BF16)

        def values(i):
            vt = jnp.concatenate([vt_ref[0, tile_of(2 * i)], vt_ref[0, tile_of(2 * i + 1)]], axis=1)
            return [_dot(jnp.concatenate([vt[h * HEAD_DIM:(h + 1) * HEAD_DIM], ones], axis=0),
                         p_ref[g, :, h * 2 * t:(h + 1) * 2 * t]) for h in range(2)]

        def logits(i):
            tile_max = None
            for half in range(2):
                step = 2 * i + half
                k = k_ref[0, pl.ds(pl.multiple_of(tile_of(step) * t, t), t), :]
                d = jnp.where(step > qi, n_bias, jnp.minimum(step, n_bias - 1))
                b0 = bias_ref[d, 0]
                b1 = bias_ref[d, 1]
                s = _dot(k, q4_ref[g]) + jnp.concatenate([b0, b0, b1, b1], axis=1)
                s_ref[g, half * t:(half + 1) * t] = s
                mx = jnp.max(s, axis=0, keepdims=True)
                tile_max = mx if tile_max is None else jnp.maximum(tile_max, mx)
            mt_ref[g] = tile_max

        def sweep(i):
            pv = values(i - 1)
            m_old = m_ref[g]
            m_new = jnp.maximum(m_old, mt_ref[g])
            alpha = jnp.exp2(m_old - m_new)
            m_ref[g] = m_new
            p_ref[g] = jnp.exp2(s_ref[g] - m_new).astype(BF16)
            for h in range(2):
                acc_ref[g, h] = alpha[:, h * 2 * t:(h + 1) * 2 * t] * (acc_ref[g, h] + pv[h])
            logits(i + 1)

        def fill():
            init()
            logits(0)

        def drain(n_iter):
            last = values(n_iter - 1)
            lam = lam_ref[...]
            outs = []
            for h in range(2):
                acc = acc_ref[g, h] + last[h]
                pv = acc[0:HEAD_DIM] / acc[HEAD_DIM:HEAD_DIM + 1]
                diff = pv[:, 0:t] - lam * pv[:, t:2 * t]
                ms = jnp.mean(diff * diff, axis=0, keepdims=True)
                outs.append(diff * lax.rsqrt(ms + EPS))
            out_t = jnp.concatenate(outs, axis=0) * gain_ref[...]
            o_ref[0, g * t:(g + 1) * t] = out_t.T.astype(BF16)

        return fill, sweep, drain

    stages = [pipeline(g) for g in range(QUERY_TILES)]
    for fill, _, _ in stages:
        fill()
    n_iter = (pl.program_id(2) * QUERY_TILES + QUERY_TILES + 1) // 2

    def body(i, c):
        for _, sweep, _ in stages:
            sweep(i)
        return c

    lax.fori_loop(0, n_iter, body, 0)
    for _, _, drain in stages:
        drain(n_iter)


def _diff_attention(qbt, kb, vbt, bias_bt, lam, post_gain):
    b, s, wb = kb.shape
    n_pairs = wb // LANES
    n_bias = bias_bt.shape[0] - 1
    t = ATT_TILE
    nt = s // t
    g = QUERY_TILES
    qspec = pl.BlockSpec((1, g, LANES, t), lambda i, p, j: (i, j, p, 0))
    kspec = pl.BlockSpec((1, s, LANES), lambda i, p, j: (i, 0, p))
    vspec = pl.BlockSpec((1, nt, LANES, t), lambda i, p, j: (i, 0, p, 0))
    bspec = pl.BlockSpec((n_bias + 1, 2, t, t), lambda i, p, j: (0, p, 0, 0))
    return pl.pallas_call(
        functools.partial(_diff_kernel, n_bias),
        out_shape=jax.ShapeDtypeStruct((b, s, wb), BF16),
        grid=(b, n_pairs, nt // g),
        in_specs=[qspec, kspec, vspec, bspec,
                  pl.BlockSpec((1, t), lambda i, p, j: (0, 0)),
                  pl.BlockSpec((LANES, t), lambda i, p, j: (0, 0))],
        out_specs=pl.BlockSpec((1, g * t, LANES), lambda i, p, j: (i, j, p)),
        scratch_shapes=[pltpu.VMEM((g, LANES, 4 * t), BF16),
                        pltpu.VMEM((g, 1, 4 * t), F32),
                        pltpu.VMEM((g, 1, 4 * t), F32),
                        pltpu.VMEM((g, 2, HEAD_DIM + ONES_ROWS, 2 * t), F32),
                        pltpu.VMEM((g, 2 * t, 4 * t), F32),
                        pltpu.VMEM((g, 2 * t, 4 * t), BF16)],
        compiler_params=_cparams("parallel", "parallel", "parallel"),
        name="diff_attention",
    )(qbt.reshape(b, nt, wb, t), kb, vbt.reshape(b, nt, wb, t), bias_bt, lam, post_gain)


def _stick_kernel(qt_ref, k_ref, vt_ref, tri_ref, o_ref, q2_ref, carry_ref, scale_ref, acc_ref,
                  z_ref, lw_ref, p_ref):
    t = ATT_TILE
    row = lax.broadcasted_iota(jnp.int32, (LANES, t), 0)

    def pipeline(g):
        qi = pl.program_id(2) * QUERY_TILES + g

        def tile_of(step):
            return jnp.clip(qi - step, 0, qi)

        def init():
            qt = qt_ref[0, g]
            zero = jnp.zeros_like(qt)
            q2_ref[g, :, 0:t] = jnp.where(row < HEAD_DIM, qt, zero)
            q2_ref[g, :, t:] = jnp.where(row < HEAD_DIM, zero, qt)
            carry_ref[g] = jnp.zeros(carry_ref.shape[1:], F32)
            acc_ref[g] = jnp.zeros(acc_ref.shape[1:], F32)

        def logits(i, diagonal=False):
            for half in range(2):
                k = k_ref[0, pl.ds(pl.multiple_of(tile_of(2 * i + half) * t, t), t), :]
                z = _dot(k, q2_ref[g])
                if diagonal and half == 0:
                    key = lax.broadcasted_iota(jnp.int32, (t, t), 0)
                    qry = lax.broadcasted_iota(jnp.int32, (t, t), 1)
                    strict = jnp.concatenate([key < qry, key < qry], axis=1)
                    z = jnp.where(strict, z, -SB_MASK * LOG2E)
                z_ref[g, half] = z

        def log_weights():
            for half in range(2):
                z = z_ref[g, half]
                neg_abs = pltpu.bitcast(pltpu.bitcast(z, jnp.uint32) | jnp.uint32(0x80000000), F32)
                sp = jnp.maximum(z, 0.0) + jnp.log(1.0 + jnp.exp2(neg_abs)) * LOG2E
                w = _dot(tri_ref[...], sp.astype(BF16))
                lw_ref[g, half, 0:t] = z + w[0:t]
                lw_ref[g, half, t:] = w[t:]

        def weights():
            carry = carry_ref[g]
            near_sum = lw_ref[g, 0, t:t + 1]
            p_ref[g, 0:t] = jnp.exp2(lw_ref[g, 0, 0:t]).astype(BF16)
            p_ref[g, t:] = jnp.exp2(lw_ref[g, 1, 0:t] + near_sum).astype(BF16)
            scale_ref[g] = jnp.exp2(carry)
            carry_ref[g] = carry + near_sum + lw_ref[g, 1, t:t + 1]

        def values(i):
            vts = []
            for half in range(2):
                step = 2 * i + half
                valid = jnp.logical_and(step >= 0, step <= qi)
                vt = vt_ref[0, tile_of(step)]
                vts.append(jnp.where(valid, vt, jnp.zeros_like(vt)))
            acc_ref[g] += _dot(jnp.concatenate(vts, axis=1), p_ref[g]) * scale_ref[g]

        def sweep(i):
            values(i - 1)
            weights()
            log_weights()
            logits(i + 2)

        def first_pair():
            init()
            logits(0, diagonal=True)
            log_weights()
            weights()

        def catch_up():
            logits(1)
            log_weights()
            logits(2)

        def drain(n_iter):
            values(n_iter - 1)
            out_t = jnp.concatenate([acc_ref[g, 0:HEAD_DIM, 0:t], acc_ref[g, HEAD_DIM:, t:]], axis=0)
            o_ref[0, g * t:(g + 1) * t] = out_t.T.astype(BF16)

        return first_pair, catch_up, sweep, drain

    stages = [pipeline(g) for g in range(QUERY_TILES)]
    for first_pair, _, _, _ in stages:
        first_pair()
    n_iter = (pl.program_id(2) * QUERY_TILES + QUERY_TILES + 1) // 2

    for _, catch_up, _, _ in stages:
        catch_up()

    def body(i, c):
        for _, _, sweep, _ in stages:
            sweep(i)
        return c

    lax.fori_loop(1, n_iter, body, 0)
    for _, _, _, drain in stages:
        drain(n_iter)


def _stick_attention(qct, kc, vct):
    b, s, wc = kc.shape
    n_pairs = wc // LANES
    t = ATT_TILE
    nt = s // t
    g = QUERY_TILES
    idx = np.arange(t)
    tri = np.concatenate([idx[None, :] >= idx[:, None], np.ones((ONES_ROWS, t), bool)], axis=0)
    tri = -jnp.asarray(tri, dtype=BF16)
    return pl.pallas_call(
        _stick_kernel,
        out_shape=jax.ShapeDtypeStruct((b, s, wc), BF16),
        grid=(b, n_pairs, nt // g),
        in_specs=[pl.BlockSpec((1, g, LANES, t), lambda i, p, j: (i, j, p, 0)),
                  pl.BlockSpec((1, s, LANES), lambda i, p, j: (i, 0, p)),
                  pl.BlockSpec((1, nt, LANES, t), lambda i, p, j: (i, 0, p, 0)),
                  pl.BlockSpec((t + ONES_ROWS, t), lambda i, p, j: (0, 0))],
        out_specs=pl.BlockSpec((1, g * t, LANES), lambda i, p, j: (i, j, p)),
        scratch_shapes=[pltpu.VMEM((g, LANES, 2 * t), BF16),
                        pltpu.VMEM((g, 1, 2 * t), F32),
                        pltpu.VMEM((g, 1, 2 * t), F32),
                        pltpu.VMEM((g, LANES, 2 * t), F32),
                        pltpu.VMEM((g, 2, t, 2 * t), F32),
                        pltpu.VMEM((g, 2, t + ONES_ROWS, 2 * t), F32),
                        pltpu.VMEM((g, 2 * t, 2 * t), BF16)],
        compiler_params=_cparams("parallel", "parallel", "parallel"),
        name="stick_breaking_attention",
    )(qct.reshape(b, nt, wc, t), kc, vct.reshape(b, nt, wc, t), tri)


def _out_proj_kernel(widths, n_ffn, x_ref, a_ref, b_ref, c_ref, w_ref, *rest):
    o_ref = rest[4 * n_ffn]
    wa, wb, wc = widths
    y = _dot(a_ref[...], w_ref[0:wa])
    y += _dot(b_ref[...], w_ref[wa:wa + wb])
    y += _dot(c_ref[...], w_ref[wa + wb:wa + wb + wc])
    x = x_ref[...] + y
    for f in range(n_ffn):
        g_ref, wg_ref, wu_ref, wd_ref = rest[4 * f:4 * f + 4]
        h = _rms(x, g_ref[...]).astype(BF16)
        gate = _dot(h, wg_ref[...])
        up = _dot(h, wu_ref[...])
        act = (gate * jax.nn.sigmoid(gate) * up).astype(BF16)
        x = x + 0.5 * _dot(act, wd_ref[...])
    o_ref[...] = x


def _out_proj(x2, oa, ob, oc, w_out, widths, ffns):
    t, d = x2.shape
    row = lambda w: pl.BlockSpec((ROW_TILE, w), lambda i: (i, 0))
    full = lambda a: pl.BlockSpec(a.shape, lambda i: (0, 0))
    ffn_args = []
    for gain, wg, wu, wd in ffns:
        ffn_args += [gain.reshape(1, d), wg, wu, wd]
    return pl.pallas_call(
        functools.partial(_out_proj_kernel, widths, len(ffns)),
        out_shape=jax.ShapeDtypeStruct((t, d), F32),
        grid=(t // ROW_TILE,),
        in_specs=[row(d), row(widths[0]), row(widths[1]), row(widths[2]), full(w_out)]
                 + [full(a) for a in ffn_args],
        out_specs=row(d),
        compiler_params=_cparams("parallel"),
        name="out_proj_residual_ffn",
    )(x2, oa, ob, oc, w_out, *ffn_args)


def _t5_bucket(dist):
    dist = jnp.maximum(dist, 0)
    max_exact = N_BUCKETS // 2
    d_f = jnp.maximum(dist, 1).astype(F32)
    large = max_exact + (jnp.log(d_f / max_exact) / math.log(MAX_DISTANCE / max_exact)
                         * (N_BUCKETS - max_exact)).astype(jnp.int32)
    large = jnp.minimum(large, N_BUCKETS - 1)
    return jnp.where(dist < max_exact, dist, large)


def _bias_of_distance(bias, dist):
    bucket = _t5_bucket(dist)[None]
    out = jnp.zeros((bias.shape[1],) + dist.shape, F32)
    for b in range(N_BUCKETS):
        out = jnp.where(bucket == b, bias[b].reshape((-1,) + (1,) * dist.ndim), out)
    return out


def _dilated_bias_table(bias_a):
    n_heads = bias_a.shape[1]
    i = jnp.arange(Q_BLOCK, dtype=jnp.int32)[:, None]
    j = jnp.arange(2 * Q_BLOCK, dtype=jnp.int32)[None, :]
    tables = []
    for window, dil in DILATED_BRANCHES:
        n = window // dil
        off = i + n - j
        band = (off >= 0) & (off <= n)
        bias = _bias_of_distance(bias_a, off * dil)
        variants = [jnp.where(valid[None], bias * LOG2E, NEG_INF)
                    for valid in (band, band & (j >= n))]
        tables.append(jnp.stack(variants))
    table = jnp.stack(tables)
    return table.reshape(len(DILATED_BRANCHES), 2, n_heads // 2, 2 * Q_BLOCK, 2 * Q_BLOCK)


def _diff_bias_table(bias_b, seq):
    t = ATT_TILE
    n_bias = min(seq // t, MAX_DISTANCE // t + 2)
    key = jnp.arange(t, dtype=jnp.int32)[None, :, None]
    qry = jnp.arange(t, dtype=jnp.int32)[None, None, :]
    dist = jnp.arange(n_bias + 1, dtype=jnp.int32)[:, None, None] * t + qry - key
    valid = (dist >= 0) & (jnp.arange(n_bias + 1)[:, None, None] < n_bias)
    tiles = jnp.where(valid[None], _bias_of_distance(bias_b, dist) * LOG2E, NEG_INF)
    return jnp.swapaxes(tiles, 0, 1)


def kernel(x, rel_bias, ffn1_norm, ffn1_w_gate, ffn1_w_up, ffn1_w_down, mix_norm, w_in,
           q_norm_a, k_norm_a, q_norm_b, k_norm_b, lambda_q1, lambda_k1, lambda_q2, lambda_k2,
           diff_subln, w_out, ffn2_norm, ffn2_w_gate, ffn2_w_up, ffn2_w_down):
    b, s, d = x.shape
    depth = w_in.shape[0]
    n_heads = d // HEAD_DIM
    wa = (n_heads // 2) * HEAD_DIM
    wb = (n_heads // 4) * HEAD_DIM
    wc = d - wa - wb
    widths = (wa, wb, wc)
    assert w_in.shape[2] == 3 * d and s % A_CHUNK == 0 and (b * s) % ROW_TILE == 0

    rb = rel_bias.astype(F32)
    bias_a = _dilated_bias_table(rb[:, :wa // HEAD_DIM])
    bias_bt = _diff_bias_table(rb[:, wa // HEAD_DIM:], s)

    def ffn1(layer):
        return (ffn1_norm[layer], ffn1_w_gate[layer].astype(BF16),
                ffn1_w_up[layer].astype(BF16), ffn1_w_down[layer].astype(BF16))

    def ffn2(layer):
        return (ffn2_norm[layer], ffn2_w_gate[layer].astype(BF16),
                ffn2_w_up[layer].astype(BF16), ffn2_w_down[layer].astype(BF16))

    x2 = _ffn(x.reshape(b * s, d), *ffn1(0))
    for layer in range(depth):
        qa, ka, va, kb, kc, qbt, vbt, qct, vct = _proj(
            x2, mix_norm[layer], w_in[layer], q_norm_a[layer], k_norm_a[layer],
            q_norm_b[layer], k_norm_b[layer], widths)
        seq3 = lambda t: t.reshape(b, s, t.shape[-1])

        out_a = _dilated_attention(seq3(qa), seq3(ka), seq3(va), bias_a)

        lam_init = 0.8 - 0.6 * math.exp(-0.3 * layer)
        lam = (jnp.exp(jnp.sum(lambda_q1[layer].astype(F32) * lambda_k1[layer].astype(F32)))
               - jnp.exp(jnp.sum(lambda_q2[layer].astype(F32) * lambda_k2[layer].astype(F32)))
               + lam_init)
        lam_row = jnp.full((1, ATT_TILE), lam, F32)
        post_gain = jnp.broadcast_to(
            (jnp.tile(diff_subln[layer].astype(F32), LANES // HEAD_DIM) * (1.0 - lam_init))[:, None],
            (LANES, ATT_TILE))
        out_b = _diff_attention(qbt, seq3(kb), vbt, bias_bt, lam_row, post_gain)

        out_c = _stick_attention(qct, seq3(kc), vct)

        ffns = [ffn2(layer)] + ([ffn1(layer + 1)] if layer + 1 < depth else [])
        x2 = _out_proj(x2, out_a.reshape(b * s, wa), out_b.reshape(b * s, wb),
                       out_c.reshape(b * s, wc), w_out[layer].astype(BF16), widths, ffns)
    return x2.reshape(b, s, d)
```

```python
import functools
import math

import jax
import jax.numpy as jnp
import numpy as np
from jax import lax
from jax.experimental import pallas as pl
from jax.experimental.pallas import tpu as pltpu

F32 = jnp.float32
BF16 = jnp.bfloat16

HEAD_DIM = 64
DIFF_HALF = HEAD_DIM // 2
N_BUCKETS = 32
MAX_DISTANCE = 2048
DILATED_BRANCHES = ((128, 1), (512, 4), (2048, 16))
Q_BLOCK = 128
EPS = 1e-6
NEG_INF = -1e30
SB_MASK = 1e4
LOG2E = math.log2(math.e)

LANES = 128
A_CHUNK = 2048
A_BLOCKS_PER_ITER = 16
ATT_TILE = 256
ROW_TILE = 512
ONES_ROWS = 16
QUERY_TILES = 2
VMEM_LIMIT = 56 * 1024 * 1024


def _cparams(*sem):
    return pltpu.CompilerParams(dimension_semantics=sem, vmem_limit_bytes=VMEM_LIMIT)


def _rms(x, gain_row):
    ms = jnp.mean(x * x, axis=-1, keepdims=True)
    return x * lax.rsqrt(ms + EPS) * gain_row


def _dot(a, b):
    return jnp.dot(a, b, preferred_element_type=F32)


def _dot_nt(a, b):
    return lax.dot_general(a, b, (((1,), (1,)), ((), ())), preferred_element_type=F32)


def _ffn_kernel(x_ref, g_ref, wg_ref, wu_ref, wd_ref, o_ref):
    x = x_ref[...]
    h = _rms(x, g_ref[...]).astype(BF16)
    gate = _dot(h, wg_ref[...])
    up = _dot(h, wu_ref[...])
    act = (gate * jax.nn.sigmoid(gate) * up).astype(BF16)
    o_ref[...] = x + 0.5 * _dot(act, wd_ref[...])


def _ffn(x2, gain, wg, wu, wd):
    t, d = x2.shape
    dff = wg.shape[1]
    row = pl.BlockSpec((ROW_TILE, d), lambda i: (i, 0))
    full = lambda shape: pl.BlockSpec(shape, lambda i: (0, 0))
    return pl.pallas_call(
        _ffn_kernel,
        out_shape=jax.ShapeDtypeStruct((t, d), F32),
        grid=(t // ROW_TILE,),
        in_specs=[row, full((1, d)), full((d, dff)), full((d, dff)), full((dff, d))],
        out_specs=row,
        compiler_params=_cparams("parallel"),
        name="ffn_half_step",
    )(x2, gain.reshape(1, d), wg, wu, wd)


def _proj_kernel(widths, x_ref, g_ref, wn_ref, wt_ref, gqa_ref, gka_ref, gkb_ref, gqb_ref,
                 g64_ref, g32_ref, qa_ref, ka_ref, va_ref, kb_ref, kc_ref,
                 qbt_ref, vbt_ref, qct_ref, vct_ref):
    wa, wb, wc = widths
    h = _rms(x_ref[...], g_ref[...]).astype(BF16)
    sub = 256

    def group_norm(t, ones_ref, gain_ref, group, scale):
        ss = _dot((t * t).astype(BF16), ones_ref[...])
        return t * lax.rsqrt(ss * (1.0 / group) + EPS) * (gain_ref[...] * scale)

    def group_norm_t(t, ones_ref, gain_ref, group, scale):
        ss = _dot(ones_ref[...], (t * t).astype(BF16))
        return t * lax.rsqrt(ss * (1.0 / group) + EPS) * (gain_ref[...] * scale)

    nat = _dot(h, wn_ref[...])
    col = 0
    for ref, width, post in (
            (qa_ref, wa, lambda t: group_norm(t, g64_ref, gqa_ref, HEAD_DIM,
                                              HEAD_DIM ** -0.5 * LOG2E)),
            (ka_ref, wa, lambda t: group_norm(t, g64_ref, gka_ref, HEAD_DIM, 1.0)),
            (va_ref, wa, lambda t: t),
            (kb_ref, wb, lambda t: group_norm(t, g32_ref, gkb_ref, DIFF_HALF, 1.0)),
            (kc_ref, wc, lambda t: t)):
        for c in range(0, width, sub):
            ref[:, c:c + sub] = post(nat[:, col + c:col + c + sub]).astype(ref.dtype)
        col += width

    tra = _dot_nt(wt_ref[...], h)
    row = 0
    for ref, width, post in (
            (qbt_ref, wb, lambda t: group_norm_t(t, g32_ref, gqb_ref, DIFF_HALF,
                                                 DIFF_HALF ** -0.5 * LOG2E)),
            (vbt_ref, wb, lambda t: t),
            (qct_ref, wc, lambda t: t * (HEAD_DIM ** -0.5 * LOG2E)),
            (vct_ref, wc, lambda t: t)):
        for r in range(0, width, sub):
            t = post(tra[row + r:row + r + sub]).astype(BF16)
            for j in range(ROW_TILE // ATT_TILE):
                ref[j, r:r + sub, :] = t[:, j * ATT_TILE:(j + 1) * ATT_TILE]
        row += width


def _block_diag_ones(group):
    idx = np.arange(256) // group
    return jnp.asarray(idx[:, None] == idx[None, :], dtype=BF16)


def _proj(x2, gain, w_in, gqa, gka, gqb, gkb, widths):
    t, d = x2.shape
    wa, wb, wc = widths
    assert wa % 256 == 0 and wb % 256 == 0 and wc % 256 == 0
    w = w_in.astype(BF16)
    o = np.cumsum([0, wa, wa, wa, wb, wb, wb, wc, wc, wc])
    sec = lambda i: w[:, o[i]:o[i + 1]]
    w_nat = jnp.concatenate([sec(0), sec(1), sec(2), sec(4), sec(7)], axis=1)
    w_tr = jnp.concatenate([sec(3), sec(5), sec(6), sec(8)], axis=1).T
    row = lambda wd: pl.BlockSpec((ROW_TILE, wd), lambda i: (i, 0))
    full = lambda shape: pl.BlockSpec(shape, lambda i: (0,) * len(shape))
    slab = lambda wd: pl.BlockSpec((ROW_TILE // ATT_TILE, wd, ATT_TILE), lambda i: (i, 0, 0))
    tile256 = lambda g: jnp.tile(g.astype(F32), 256 // g.shape[0])
    out_shape = ([jax.ShapeDtypeStruct((t, wa), F32)] * 3
                 + [jax.ShapeDtypeStruct((t, wb), BF16), jax.ShapeDtypeStruct((t, wc), BF16)]
                 + [jax.ShapeDtypeStruct((t // ATT_TILE, wd, ATT_TILE), BF16)
                    for wd in (wb, wb, wc, wc)])
    return pl.pallas_call(
        functools.partial(_proj_kernel, widths),
        out_shape=out_shape,
        grid=(t // ROW_TILE,),
        in_specs=[row(d), full((1, d)), full(w_nat.shape), full(w_tr.shape)]
                 + [full((1, 256))] * 3 + [full((256, 1))] + [full((256, 256))] * 2,
        out_specs=[row(wa)] * 3 + [row(wb), row(wc)] + [slab(wb), slab(wb), slab(wc), slab(wc)],
        compiler_params=_cparams("parallel"),
        name="norm_in_proj",
    )(x2, gain.reshape(1, d), w_nat, w_tr, tile256(gqa).reshape(1, 256), tile256(gka).reshape(1, 256),
      tile256(gkb).reshape(1, 256), tile256(gqb).reshape(256, 1),
      _block_diag_ones(HEAD_DIM), _block_diag_ones(DIFF_HALF))


def _dilated_kernel(q_ref, kp_ref, kc_ref, vp_ref, vc_ref, bias_ref, o_ref,
                    kk_ref, vv_ref, m_ref, l_ref, acc_ref):
    chunk = pl.program_id(1)
    kk_ref[0:A_CHUNK] = kp_ref[0]
    kk_ref[A_CHUNK:] = kc_ref[0]
    vv_ref[0:A_CHUNK] = vp_ref[0]
    vv_ref[A_CHUNK:] = vc_ref[0]

    lane = lax.broadcasted_iota(jnp.int32, (Q_BLOCK, LANES), 1)
    head0 = lane < HEAD_DIM

    order = sorted(range(len(DILATED_BRANCHES)), key=lambda b: -DILATED_BRANCHES[b][1])
    for bi in order:
        window, dil = DILATED_BRANCHES[bi]
        fresh = bi == order[0]
        nblk = A_CHUNK // (Q_BLOCK * dil)
        shift = int(math.log2(nblk))
        n_iter = dil * nblk
        ds = (lambda start, size, dil=dil:
              pl.ds(start, size, stride=dil) if dil > 1 else pl.ds(start, size))

        def load(idx, bi=bi, dil=dil, nblk=nblk, shift=shift, ds=ds, fresh=fresh):
            rho = idx >> shift
            t = idx & (nblk - 1)
            q_start = rho + t * (Q_BLOCK * dil)
            rows = ds(q_start, Q_BLOCK)
            krows = ds(A_CHUNK + q_start - Q_BLOCK * dil, 2 * Q_BLOCK)
            first = jnp.logical_and(chunk == 0, t == 0).astype(jnp.int32)
            blk = dict(rows=rows, q=q_ref[0, rows, :], k=kk_ref[krows, :], v=vv_ref[krows, :],
                       bias=bias_ref[bi, first, 0])
            if not fresh:
                blk.update(m=jnp.concatenate([m_ref[0, rows, :], m_ref[1, rows, :]], axis=0),
                           l=jnp.concatenate([l_ref[0, rows, :], l_ref[1, rows, :]], axis=0),
                           acc=acc_ref[rows, :])
            return blk

        def compute(b):
            q = b["q"]
            q2 = jnp.concatenate([jnp.where(head0, q, 0.0), jnp.where(head0, 0.0, q)],
                                 axis=0).astype(BF16)
            s = _dot_nt(q2, b["k"].astype(BF16)) + b["bias"]
            row_max = jnp.max(s, axis=-1, keepdims=True)
            if "m" not in b:
                m_new = jnp.broadcast_to(row_max, (2 * Q_BLOCK, LANES))
                p = jnp.exp2(s - row_max)
                l_new = jnp.broadcast_to(jnp.sum(p, axis=-1, keepdims=True), (2 * Q_BLOCK, LANES))
                pv = _dot(p.astype(BF16), b["v"].astype(BF16))
                return m_new, l_new, jnp.where(head0, pv[:Q_BLOCK], pv[Q_BLOCK:])
            m_new = jnp.maximum(b["m"], row_max)
            alpha = jnp.exp2(b["m"] - m_new)
            p = jnp.exp2(s - jnp.concatenate([m_new, m_new], axis=1))
            l_new = alpha * b["l"] + jnp.sum(p, axis=-1, keepdims=True)
            pv = _dot(p.astype(BF16), b["v"].astype(BF16))
            acc_new = jnp.where(head0, alpha[:Q_BLOCK] * b["acc"] + pv[:Q_BLOCK],
                                alpha[Q_BLOCK:] * b["acc"] + pv[Q_BLOCK:])
            return m_new, l_new, acc_new

        def store(b, res):
            m_new, l_new, acc_new = res
            rows = b["rows"]
            acc_ref[rows, :] = acc_new
            m_ref[0, rows, :] = m_new[:Q_BLOCK]
            m_ref[1, rows, :] = m_new[Q_BLOCK:]
            l_ref[0, rows, :] = l_new[:Q_BLOCK]
            l_ref[1, rows, :] = l_new[Q_BLOCK:]

        def some_blocks(i, carry, load=load, compute=compute, store=store,
                        part=n_iter // A_BLOCKS_PER_ITER):
            blocks = [load(i + c * part) for c in range(A_BLOCKS_PER_ITER)]
            results = [compute(b) for b in blocks]
            for b, res in zip(blocks, results):
                store(b, res)
            return carry

        lax.fori_loop(0, n_iter // A_BLOCKS_PER_ITER, some_blocks, 0)

    lane_c = lax.broadcasted_iota(jnp.int32, (A_CHUNK, LANES), 1)
    denom = jnp.where(lane_c < HEAD_DIM, l_ref[0], l_ref[1])
    o_ref[0] = (acc_ref[...] / denom).astype(BF16)


def _dilated_attention(qa, ka, va, bias_a):
    b, s, wa = qa.shape
    n_pairs = wa // LANES
    cur = pl.BlockSpec((1, A_CHUNK, LANES), lambda i, c, p: (i, c, p))
    prev = pl.BlockSpec((1, A_CHUNK, LANES), lambda i, c, p: (i, jnp.maximum(c - 1, 0), p))
    nb = len(DILATED_BRANCHES)
    bias = pl.BlockSpec((nb, 2, 1, 2 * Q_BLOCK, 2 * Q_BLOCK), lambda i, c, p: (0, 0, p, 0, 0))
    return pl.pallas_call(
        _dilated_kernel,
        out_shape=jax.ShapeDtypeStruct((b, s, wa), BF16),
        grid=(b, s // A_CHUNK, n_pairs),
        in_specs=[cur, prev, cur, prev, cur, bias],
        out_specs=cur,
        scratch_shapes=[pltpu.VMEM((2 * A_CHUNK, LANES), F32),
                        pltpu.VMEM((2 * A_CHUNK, LANES), F32),
                        pltpu.VMEM((2, A_CHUNK, LANES), F32),
                        pltpu.VMEM((2, A_CHUNK, LANES), F32),
                        pltpu.VMEM((A_CHUNK, LANES), F32)],
        compiler_params=_cparams("parallel", "parallel", "parallel"),
        name="dilated_attention",
    )(qa, ka, ka, va, va, bias_a)


def _diff_kernel(n_bias, qt_ref, k_ref, vt_ref, bias_ref, lam_ref, gain_ref, o_ref,
                 q4_ref, m_ref, mt_ref, acc_ref, s_ref, p_ref):
    t = ATT_TILE
    row = lax.broadcasted_iota(jnp.int32, (LANES, t), 0)
    ones = jnp.ones((ONES_ROWS, 2 * t), BF16)

    def pipeline(g):
        qi = pl.program_id(2) * QUERY_TILES + g

        def tile_of(step):
            return jnp.clip(qi - step, 0, qi)

        def init():
            qt = qt_ref[0, g]
            for c in range(4):
                sel = jnp.logical_and(row >= c * DIFF_HALF, row < (c + 1) * DIFF_HALF)
                q4_ref[g, :, c * t:(c + 1) * t] = jnp.where(sel, qt, jnp.zeros_like(qt))
            acc_ref[g] = jnp.zeros(acc_ref.shape[1:], F32)

        def values(i):
            vt = jnp.concatenate([vt_ref[0, tile_of(2 * i)], vt_ref[0, tile_of(2 * i + 1)]], axis=1)
            return [_dot(jnp.concatenate([vt[h * HEAD_DIM:(h + 1) * HEAD_DIM], ones], axis=0),
                         p_ref[g, :, h * 2 * t:(h + 1) * 2 * t]) for h in range(2)]

        def logits(i):
            tile_max = None
            for half in range(2):
                step = 2 * i + half
                k = k_ref[0, pl.ds(pl.multiple_of(tile_of(step) * t, t), t), :]
                d = jnp.where(step > qi, n_bias, jnp.minimum(step, n_bias - 1))
                b0 = bias_ref[d, 0]
                b1 = bias_ref[d, 1]
                s = _dot(k, q4_ref[g]) + jnp.concatenate([b0, b0, b1, b1], axis=1)
                s_ref[g, half * t:(half + 1) * t] = s
                mx = jnp.max(s, axis=0, keepdims=True)
                tile_max = mx if tile_max is None else jnp.maximum(tile_max, mx)
            mt_ref[g] = tile_max

        def sweep(i):
            pv = values(i - 1)
            m_old = m_ref[g]
            m_new = jnp.maximum(m_old, mt_ref[g])
            alpha = jnp.exp2(m_old - m_new)
            m_ref[g] = m_new
            p_ref[g] = jnp.exp2(s_ref[g] - m_new).astype(BF16)
            for h in range(2):
                acc_ref[g, h] = alpha[:, h * 2 * t:(h + 1) * 2 * t] * (acc_ref[g, h] + pv[h])
            logits(i + 1)

        def fill():
            init()
            logits(0)
            m_new = mt_ref[g]
            m_ref[g] = m_new
            p_ref[g] = jnp.exp2(s_ref[g] - m_new).astype(BF16)
            logits(1)

        def drain(n_iter):
            last = values(n_iter - 1)
            lam = lam_ref[...]
            outs = []
            for h in range(2):
                acc = acc_ref[g, h] + last[h]
                pv = acc[0:HEAD_DIM] / acc[HEAD_DIM:HEAD_DIM + 1]
                diff = pv[:, 0:t] - lam * pv[:, t:2 * t]
                ms = jnp.mean(diff * diff, axis=0, keepdims=True)
                outs.append(diff * lax.rsqrt(ms + EPS))
            out_t = jnp.concatenate(outs, axis=0) * gain_ref[...]
            o_ref[0, g * t:(g + 1) * t] = out_t.T.astype(BF16)

        return fill, sweep, drain

    stages = [pipeline(g) for g in range(QUERY_TILES)]
    for fill, _, _ in stages:
        fill()
    n_iter = (pl.program_id(2) * QUERY_TILES + QUERY_TILES + 1) // 2

    def body(i, c):
        for _, sweep, _ in stages:
            sweep(i)
        return c

    lax.fori_loop(1, n_iter, body, 0)
    for _, _, drain in stages:
        drain(n_iter)


def _diff_attention(qbt, kb, vbt, bias_bt, lam, post_gain):
    b, s, wb = kb.shape
    n_pairs = wb // LANES
    n_bias = bias_bt.shape[0] - 1
    t = ATT_TILE
    nt = s // t
    g = QUERY_TILES
    qspec = pl.BlockSpec((1, g, LANES, t), lambda i, p, j: (i, j, p, 0))
    kspec = pl.BlockSpec((1, s, LANES), lambda i, p, j: (i, 0, p))
    vspec = pl.BlockSpec((1, nt, LANES, t), lambda i, p, j: (i, 0, p, 0))
    bspec = pl.BlockSpec((n_bias + 1, 2, t, t), lambda i, p, j: (0, p, 0, 0))
    return pl.pallas_call(
        functools.partial(_diff_kernel, n_bias),
        out_shape=jax.ShapeDtypeStruct((b, s, wb), BF16),
        grid=(b, n_pairs, nt // g),
        in_specs=[qspec, kspec, vspec, bspec,
                  pl.BlockSpec((1, t), lambda i, p, j: (0, 0)),
                  pl.BlockSpec((LANES, t), lambda i, p, j: (0, 0))],
        out_specs=pl.BlockSpec((1, g * t, LANES), lambda i, p, j: (i, j, p)),
        scratch_shapes=[pltpu.VMEM((g, LANES, 4 * t), BF16),
                        pltpu.VMEM((g, 1, 4 * t), F32),
                        pltpu.VMEM((g, 1, 4 * t), F32),
                        pltpu.VMEM((g, 2, HEAD_DIM + ONES_ROWS, 2 * t), F32),
                        pltpu.VMEM((g, 2 * t, 4 * t), F32),
                        pltpu.VMEM((g, 2 * t, 4 * t), BF16)],
        compiler_params=_cparams("parallel", "parallel", "parallel"),
        name="diff_attention",
    )(qbt.reshape(b, nt, wb, t), kb, vbt.reshape(b, nt, wb, t), bias_bt, lam, post_gain)


def _stick_kernel(qt_ref, k_ref, vt_ref, tri_ref, o_ref, q2_ref, carry_ref, scale_ref, acc_ref,
                  z_ref, lw_ref, p_ref):
    t = ATT_TILE
    row = lax.broadcasted_iota(jnp.int32, (LANES, t), 0)

    def pipeline(g):
        qi = pl.program_id(2) * QUERY_TILES + g

        def tile_of(step):
            return jnp.clip(qi - step, 0, qi)

        def init():
            qt = qt_ref[0, g]
            zero = jnp.zeros_like(qt)
            q2_ref[g, :, 0:t] = jnp.where(row < HEAD_DIM, qt, zero)
            q2_ref[g, :, t:] = jnp.where(row < HEAD_DIM, zero, qt)
            carry_ref[g] = jnp.zeros(carry_ref.shape[1:], F32)
            acc_ref[g] = jnp.zeros(acc_ref.shape[1:], F32)

        def logits(i, diagonal=False):
            for half in range(2):
                k = k_ref[0, pl.ds(pl.multiple_of(tile_of(2 * i + half) * t, t), t), :]
                z = _dot(k, q2_ref[g])
                if diagonal and half == 0:
                    key = lax.broadcasted_iota(jnp.int32, (t, t), 0)
                    qry = lax.broadcasted_iota(jnp.int32, (t, t), 1)
                    strict = jnp.concatenate([key < qry, key < qry], axis=1)
                    z = jnp.where(strict, z, -SB_MASK * LOG2E)
                z_ref[g, half] = z

        def log_weights():
            for half in range(2):
                z = z_ref[g, half]
                neg_abs = pltpu.bitcast(pltpu.bitcast(z, jnp.uint32) | jnp.uint32(0x80000000), F32)
                sp = jnp.maximum(z, 0.0) + jnp.log(1.0 + jnp.exp2(neg_abs)) * LOG2E
                w = _dot(tri_ref[...], sp.astype(BF16))
                lw_ref[g, half, 0:t] = z + w[0:t]
                lw_ref[g, half, t:] = w[t:]

        def weights():
            carry = carry_ref[g]
            near_sum = lw_ref[g, 0, t:t + 1]
            p_ref[g, 0:t] = jnp.exp2(lw_ref[g, 0, 0:t]).astype(BF16)
            p_ref[g, t:] = jnp.exp2(lw_ref[g, 1, 0:t] + near_sum).astype(BF16)
            scale_ref[g] = jnp.exp2(carry)
            carry_ref[g] = carry + near_sum + lw_ref[g, 1, t:t + 1]

        def values(i):
            vts = []
            for half in range(2):
                step = 2 * i + half
                valid = jnp.logical_and(step >= 0, step <= qi)
                vt = vt_ref[0, tile_of(step)]
                vts.append(jnp.where(valid, vt, jnp.zeros_like(vt)))
            acc_ref[g] += _dot(jnp.concatenate(vts, axis=1), p_ref[g]) * scale_ref[g]

        def sweep(i):
            values(i - 1)
            weights()
            log_weights()
            logits(i + 2)

        def first_pair():
            init()
            logits(0, diagonal=True)
            log_weights()
            weights()

        def catch_up():
            logits(1)
            log_weights()
            logits(2)

        def drain(n_iter):
            values(n_iter - 1)
            out_t = jnp.concatenate([acc_ref[g, 0:HEAD_DIM, 0:t], acc_ref[g, HEAD_DIM:, t:]], axis=0)
            o_ref[0, g * t:(g + 1) * t] = out_t.T.astype(BF16)

        return first_pair, catch_up, sweep, drain

    stages = [pipeline(g) for g in range(QUERY_TILES)]
    for first_pair, _, _, _ in stages:
        first_pair()
    n_iter = (pl.program_id(2) * QUERY_TILES + QUERY_TILES + 1) // 2

    def live():
        return (jnp.max(jnp.exp2(carry_ref[...])) > 0.0).astype(jnp.int32)

    live_0 = live()

    @pl.when(live_0 > 0)
    def _():
        for _, catch_up, _, _ in stages:
            catch_up()

    def body(state):
        i, _ = state
        for _, _, sweep, _ in stages:
            sweep(i)
        return i + 1, live()

    n_done, _ = lax.while_loop(lambda st: jnp.logical_and(st[0] < n_iter, st[1] > 0), body,
                               (jnp.int32(1), live_0))
    for _, _, _, drain in stages:
        drain(n_done)


def _stick_attention(qct, kc, vct):
    b, s, wc = kc.shape
    n_pairs = wc // LANES
    t = ATT_TILE
    nt = s // t
    g = QUERY_TILES
    idx = np.arange(t)
    tri = np.concatenate([idx[None, :] >= idx[:, None], np.ones((ONES_ROWS, t), bool)], axis=0)
    tri = -jnp.asarray(tri, dtype=BF16)
    return pl.pallas_call(
        _stick_kernel,
        out_shape=jax.ShapeDtypeStruct((b, s, wc), BF16),
        grid=(b, n_pairs, nt // g),
        in_specs=[pl.BlockSpec((1, g, LANES, t), lambda i, p, j: (i, j, p, 0)),
                  pl.BlockSpec((1, s, LANES), lambda i, p, j: (i, 0, p)),
                  pl.BlockSpec((1, nt, LANES, t), lambda i, p, j: (i, 0, p, 0)),
                  pl.BlockSpec((t + ONES_ROWS, t), lambda i, p, j: (0, 0))],
        out_specs=pl.BlockSpec((1, g * t, LANES), lambda i, p, j: (i, j, p)),
        scratch_shapes=[pltpu.VMEM((g, LANES, 2 * t), BF16),
                        pltpu.VMEM((g, 1, 2 * t), F32),
                        pltpu.VMEM((g, 1, 2 * t), F32),
                        pltpu.VMEM((g, LANES, 2 * t), F32),
                        pltpu.VMEM((g, 2, t, 2 * t), F32),
                        pltpu.VMEM((g, 2, t + ONES_ROWS, 2 * t), F32),
                        pltpu.VMEM((g, 2 * t, 2 * t), BF16)],
        compiler_params=_cparams("parallel", "parallel", "parallel"),
        name="stick_breaking_attention",
    )(qct.reshape(b, nt, wc, t), kc, vct.reshape(b, nt, wc, t), tri)


def _out_proj_kernel(widths, n_ffn, x_ref, a_ref, b_ref, c_ref, w_ref, *rest):
    o_ref = rest[4 * n_ffn]
    wa, wb, wc = widths
    y = _dot(a_ref[...], w_ref[0:wa])
    y += _dot(b_ref[...], w_ref[wa:wa + wb])
    y += _dot(c_ref[...], w_ref[wa + wb:wa + wb + wc])
    x = x_ref[...] + y
    for f in range(n_ffn):
        g_ref, wg_ref, wu_ref, wd_ref = rest[4 * f:4 * f + 4]
        h = _rms(x, g_ref[...]).astype(BF16)
        gate = _dot(h, wg_ref[...])
        up = _dot(h, wu_ref[...])
        act = (gate * jax.nn.sigmoid(gate) * up).astype(BF16)
        x = x + 0.5 * _dot(act, wd_ref[...])
    o_ref[...] = x


def _out_proj(x2, oa, ob, oc, w_out, widths, ffns):
    t, d = x2.shape
    row = lambda w: pl.BlockSpec((ROW_TILE, w), lambda i: (i, 0))
    full = lambda a: pl.BlockSpec(a.shape, lambda i: (0, 0))
    ffn_args = []
    for gain, wg, wu, wd in ffns:
        ffn_args += [gain.reshape(1, d), wg, wu, wd]
    return pl.pallas_call(
        functools.partial(_out_proj_kernel, widths, len(ffns)),
        out_shape=jax.ShapeDtypeStruct((t, d), F32),
        grid=(t // ROW_TILE,),
        in_specs=[row(d), row(widths[0]), row(widths[1]), row(widths[2]), full(w_out)]
                 + [full(a) for a in ffn_args],
        out_specs=row(d),
        compiler_params=_cparams("parallel"),
        name="out_proj_residual_ffn",
    )(x2, oa, ob, oc, w_out, *ffn_args)


def _t5_bucket(dist):
    dist = jnp.maximum(dist, 0)
    max_exact = N_BUCKETS // 2
    d_f = jnp.maximum(dist, 1).astype(F32)
    large = max_exact + (jnp.log(d_f / max_exact) / math.log(MAX_DISTANCE / max_exact)
                         * (N_BUCKETS - max_exact)).astype(jnp.int32)
    large = jnp.minimum(large, N_BUCKETS - 1)
    return jnp.where(dist < max_exact, dist, large)


def _bias_of_distance(bias, dist):
    bucket = _t5_bucket(dist)[None]
    out = jnp.zeros((bias.shape[1],) + dist.shape, F32)
    for b in range(N_BUCKETS):
        out = jnp.where(bucket == b, bias[b].reshape((-1,) + (1,) * dist.ndim), out)
    return out


def _dilated_bias_table(bias_a):
    n_heads = bias_a.shape[1]
    i = jnp.arange(Q_BLOCK, dtype=jnp.int32)[:, None]
    j = jnp.arange(2 * Q_BLOCK, dtype=jnp.int32)[None, :]
    tables = []
    for window, dil in DILATED_BRANCHES:
        n = window // dil
        off = i + n - j
        band = (off >= 0) & (off <= n)
        bias = _bias_of_distance(bias_a, off * dil)
        variants = [jnp.where(valid[None], bias * LOG2E, NEG_INF)
                    for valid in (band, band & (j >= n))]
        tables.append(jnp.stack(variants))
    table = jnp.stack(tables)
    return table.reshape(len(DILATED_BRANCHES), 2, n_heads // 2, 2 * Q_BLOCK, 2 * Q_BLOCK)


def _diff_bias_table(bias_b, seq):
    t = ATT_TILE
    n_bias = min(seq // t, MAX_DISTANCE // t + 2)
    key = jnp.arange(t, dtype=jnp.int32)[None, :, None]
    qry = jnp.arange(t, dtype=jnp.int32)[None, None, :]
    dist = jnp.arange(n_bias + 1, dtype=jnp.int32)[:, None, None] * t + qry - key
    valid = (dist >= 0) & (jnp.arange(n_bias + 1)[:, None, None] < n_bias)
    tiles = jnp.where(valid[None], _bias_of_distance(bias_b, dist) * LOG2E, NEG_INF)
    return jnp.swapaxes(tiles, 0, 1)


def kernel(x, rel_bias, ffn1_norm, ffn1_w_gate, ffn1_w_up, ffn1_w_down, mix_norm, w_in,
           q_norm_a, k_norm_a, q_norm_b, k_norm_b, lambda_q1, lambda_k1, lambda_q2, lambda_k2,
           diff_subln, w_out, ffn2_norm, ffn2_w_gate, ffn2_w_up, ffn2_w_down):
    b, s, d = x.shape
    depth = w_in.shape[0]
    n_heads = d // HEAD_DIM
    wa = (n_heads // 2) * HEAD_DIM
    wb = (n_heads // 4) * HEAD_DIM
    wc = d - wa - wb
    widths = (wa, wb, wc)
    assert w_in.shape[2] == 3 * d and s % A_CHUNK == 0 and (b * s) % ROW_TILE == 0

    rb = rel_bias.astype(F32)
    bias_a = _dilated_bias_table(rb[:, :wa // HEAD_DIM])
    bias_bt = _diff_bias_table(rb[:, wa // HEAD_DIM:], s)

    def ffn1(layer):
        return (ffn1_norm[layer], ffn1_w_gate[layer].astype(BF16),
                ffn1_w_up[layer].astype(BF16), ffn1_w_down[layer].astype(BF16))

    def ffn2(layer):
        return (ffn2_norm[layer], ffn2_w_gate[layer].astype(BF16),
                ffn2_w_up[layer].astype(BF16), ffn2_w_down[layer].astype(BF16))

    x2 = _ffn(x.reshape(b * s, d), *ffn1(0))
    for layer in range(depth):
        qa, ka, va, kb, kc, qbt, vbt, qct, vct = _proj(
            x2, mix_norm[layer], w_in[layer], q_norm_a[layer], k_norm_a[layer],
            q_norm_b[layer], k_norm_b[layer], widths)
        seq3 = lambda t: t.reshape(b, s, t.shape[-1])

        out_a = _dilated_attention(seq3(qa), seq3(ka), seq3(va), bias_a)

        lam_init = 0.8 - 0.6 * math.exp(-0.3 * layer)
        lam = (jnp.exp(jnp.sum(lambda_q1[layer].astype(F32) * lambda_k1[layer].astype(F32)))
               - jnp.exp(jnp.sum(lambda_q2[layer].astype(F32) * lambda_k2[layer].astype(F32)))
               + lam_init)
        lam_row = jnp.full((1, ATT_TILE), lam, F32)
        post_gain = jnp.broadcast_to(
            (jnp.tile(diff_subln[layer].astype(F32), LANES // HEAD_DIM) * (1.0 - lam_init))[:, None],
            (LANES, ATT_TILE))
        out_b = _diff_attention(qbt, seq3(kb), vbt, bias_bt, lam_row, post_gain)

        out_c = _stick_attention(qct, seq3(kc), vct)

        ffns = [ffn2(layer)] + ([ffn1(layer + 1)] if layer + 1 < depth else [])
        x2 = _out_proj(x2, out_a.reshape(b * s, wa), out_b.reshape(b * s, wb),
                       out_c.reshape(b * s, wc), w_out[layer].astype(BF16), widths, ffns)
    return x2.reshape(b, s, d)
```

```python
import functools
import math

import jax
import jax.numpy as jnp
import numpy as np
from jax import lax
from jax.experimental import pallas as pl
from jax.experimental.pallas import tpu as pltpu

F32 = jnp.float32
BF16 = jnp.bfloat16

HEAD_DIM = 64
DIFF_HALF = HEAD_DIM // 2
N_BUCKETS = 32
MAX_DISTANCE = 2048
DILATED_BRANCHES = ((128, 1), (512, 4), (2048, 16))
Q_BLOCK = 128
EPS = 1e-6
NEG_INF = -1e30
SB_MASK = 1e4
LOG2E = math.log2(math.e)

LANES = 128
A_CHUNK = 2048
A_BLOCKS_PER_ITER = 16
ATT_TILE = 256
ROW_TILE = 512
ONES_ROWS = 16
QUERY_TILES = 2
VMEM_LIMIT = 56 * 1024 * 1024


def _cparams(*sem):
    return pltpu.CompilerParams(dimension_semantics=sem, vmem_limit_bytes=VMEM_LIMIT)


def _rms(x, gain_row):
    ms = jnp.mean(x * x, axis=-1, keepdims=True)
    return x * lax.rsqrt(ms + EPS) * gain_row


def _dot(a, b):
    return jnp.dot(a, b, preferred_element_type=F32)


def _dot_nt(a, b):
    return lax.dot_general(a, b, (((1,), (1,)), ((), ())), preferred_element_type=F32)


def _ffn_kernel(x_ref, g_ref, wg_ref, wu_ref, wd_ref, o_ref):
    x = x_ref[...]
    h = _rms(x, g_ref[...]).astype(BF16)
    gate = _dot(h, wg_ref[...])
    up = _dot(h, wu_ref[...])
    act = (gate * jax.nn.sigmoid(gate) * up).astype(BF16)
    o_ref[...] = x + 0.5 * _dot(act, wd_ref[...])


def _ffn(x2, gain, wg, wu, wd):
    t, d = x2.shape
    dff = wg.shape[1]
    row = pl.BlockSpec((ROW_TILE, d), lambda i: (i, 0))
    full = lambda shape: pl.BlockSpec(shape, lambda i: (0, 0))
    return pl.pallas_call(
        _ffn_kernel,
        out_shape=jax.ShapeDtypeStruct((t, d), F32),
        grid=(t // ROW_TILE,),
        in_specs=[row, full((1, d)), full((d, dff)), full((d, dff)), full((dff, d))],
        out_specs=row,
        compiler_params=_cparams("parallel"),
        name="ffn_half_step",
    )(x2, gain.reshape(1, d), wg, wu, wd)


def _proj_kernel(widths, x_ref, g_ref, wn_ref, wt_ref, gqa_ref, gka_ref, gkb_ref, gqb_ref,
                 g64_ref, g32_ref, qa_ref, ka_ref, va_ref, kb_ref, kc_ref,
                 qbt_ref, vbt_ref, qct_ref, vct_ref):
    wa, wb, wc = widths
    h = _rms(x_ref[...], g_ref[...]).astype(BF16)
    sub = 256

    def group_norm(t, ones_ref, gain_ref, group, scale):
        ss = _dot((t * t).astype(BF16), ones_ref[...])
        return t * lax.rsqrt(ss * (1.0 / group) + EPS) * (gain_ref[...] * scale)

    def group_norm_t(t, ones_ref, gain_ref, group, scale):
        ss = _dot(ones_ref[...], (t * t).astype(BF16))
        return t * lax.rsqrt(ss * (1.0 / group) + EPS) * (gain_ref[...] * scale)

    nat = _dot(h, wn_ref[...])
    col = 0
    for ref, width, post in (
            (qa_ref, wa, lambda t: group_norm(t, g64_ref, gqa_ref, HEAD_DIM,
                                              HEAD_DIM ** -0.5 * LOG2E)),
            (ka_ref, wa, lambda t: group_norm(t, g64_ref, gka_ref, HEAD_DIM, 1.0)),
            (va_ref, wa, lambda t: t),
            (kb_ref, wb, lambda t: group_norm(t, g32_ref, gkb_ref, DIFF_HALF, 1.0)),
            (kc_ref, wc, lambda t: t)):
        for c in range(0, width, sub):
            ref[:, c:c + sub] = post(nat[:, col + c:col + c + sub]).astype(ref.dtype)
        col += width

    tra = _dot_nt(wt_ref[...], h)
    row = 0
    for ref, width, post in (
            (qbt_ref, wb, lambda t: group_norm_t(t, g32_ref, gqb_ref, DIFF_HALF,
                                                 DIFF_HALF ** -0.5 * LOG2E)),
            (vbt_ref, wb, lambda t: t),
            (qct_ref, wc, lambda t: t * (HEAD_DIM ** -0.5 * LOG2E)),
            (vct_ref, wc, lambda t: t)):
        for r in range(0, width, sub):
            t = post(tra[row + r:row + r + sub]).astype(BF16)
            for j in range(ROW_TILE // ATT_TILE):
                ref[j, r:r + sub, :] = t[:, j * ATT_TILE:(j + 1) * ATT_TILE]
        row += width


def _block_diag_ones(group):
    idx = np.arange(256) // group
    return jnp.asarray(idx[:, None] == idx[None, :], dtype=BF16)


def _proj(x2, gain, w_in, gqa, gka, gqb, gkb, widths):
    t, d = x2.shape
    wa, wb, wc = widths
    assert wa % 256 == 0 and wb % 256 == 0 and wc % 256 == 0
    w = w_in.astype(BF16)
    o = np.cumsum([0, wa, wa, wa, wb, wb, wb, wc, wc, wc])
    sec = lambda i: w[:, o[i]:o[i + 1]]
    w_nat = jnp.concatenate([sec(0), sec(1), sec(2), sec(4), sec(7)], axis=1)
    w_tr = jnp.concatenate([sec(3), sec(5), sec(6), sec(8)], axis=1).T
    row = lambda wd: pl.BlockSpec((ROW_TILE, wd), lambda i: (i, 0))
    full = lambda shape: pl.BlockSpec(shape, lambda i: (0,) * len(shape))
    slab = lambda wd: pl.BlockSpec((ROW_TILE // ATT_TILE, wd, ATT_TILE), lambda i: (i, 0, 0))
    tile256 = lambda g: jnp.tile(g.astype(F32), 256 // g.shape[0])
    out_shape = ([jax.ShapeDtypeStruct((t, wa), F32)] * 3
                 + [jax.ShapeDtypeStruct((t, wb), BF16), jax.ShapeDtypeStruct((t, wc), BF16)]
                 + [jax.ShapeDtypeStruct((t // ATT_TILE, wd, ATT_TILE), BF16)
                    for wd in (wb, wb, wc, wc)])
    return pl.pallas_call(
        functools.partial(_proj_kernel, widths),
        out_shape=out_shape,
        grid=(t // ROW_TILE,),
        in_specs=[row(d), full((1, d)), full(w_nat.shape), full(w_tr.shape)]
                 + [full((1, 256))] * 3 + [full((256, 1))] + [full((256, 256))] * 2,
        out_specs=[row(wa)] * 3 + [row(wb), row(wc)] + [slab(wb), slab(wb), slab(wc), slab(wc)],
        compiler_params=_cparams("parallel"),
        name="norm_in_proj",
    )(x2, gain.reshape(1, d), w_nat, w_tr, tile256(gqa).reshape(1, 256), tile256(gka).reshape(1, 256),
      tile256(gkb).reshape(1, 256), tile256(gqb).reshape(256, 1),
      _block_diag_ones(HEAD_DIM), _block_diag_ones(DIFF_HALF))


def _dilated_kernel(q_ref, kp_ref, kc_ref, vp_ref, vc_ref, bias_ref, o_ref,
                    kk_ref, vv_ref, m_ref, l_ref, acc_ref):
    chunk = pl.program_id(1)
    kk_ref[0:A_CHUNK] = kp_ref[0]
    kk_ref[A_CHUNK:] = kc_ref[0]
    vv_ref[0:A_CHUNK] = vp_ref[0]
    vv_ref[A_CHUNK:] = vc_ref[0]

    lane = lax.broadcasted_iota(jnp.int32, (Q_BLOCK, LANES), 1)
    head0 = lane < HEAD_DIM

    order = sorted(range(len(DILATED_BRANCHES)), key=lambda b: -DILATED_BRANCHES[b][1])
    for bi in order:
        window, dil = DILATED_BRANCHES[bi]
        fresh = bi == order[0]
        nblk = A_CHUNK // (Q_BLOCK * dil)
        shift = int(math.log2(nblk))
        n_iter = dil * nblk
        ds = (lambda start, size, dil=dil:
              pl.ds(start, size, stride=dil) if dil > 1 else pl.ds(start, size))

        def load(idx, bi=bi, dil=dil, nblk=nblk, shift=shift, ds=ds, fresh=fresh):
            rho = idx >> shift
            t = idx & (nblk - 1)
            q_start = rho + t * (Q_BLOCK * dil)
            rows = ds(q_start, Q_BLOCK)
            krows = ds(A_CHUNK + q_start - Q_BLOCK * dil, 2 * Q_BLOCK)
            first = jnp.logical_and(chunk == 0, t == 0).astype(jnp.int32)
            blk = dict(rows=rows, q=lambda: q_ref[0, rows, :], k=lambda: kk_ref[krows, :],
                       v=lambda: vv_ref[krows, :], bias=lambda: bias_ref[bi, first, 0])
            if not fresh:
                blk.update(m=jnp.concatenate([m_ref[0, rows, :], m_ref[1, rows, :]], axis=0),
                           l=jnp.concatenate([l_ref[0, rows, :], l_ref[1, rows, :]], axis=0),
                           acc=acc_ref[rows, :])
            return blk

        def compute(b):
            q = b["q"]()
            q2 = jnp.concatenate([jnp.where(head0, q, 0.0), jnp.where(head0, 0.0, q)],
                                 axis=0).astype(BF16)
            s = _dot_nt(q2, b["k"]().astype(BF16)) + b["bias"]()
            row_max = jnp.max(s, axis=-1, keepdims=True)
            if "m" not in b:
                m_new = jnp.broadcast_to(row_max, (2 * Q_BLOCK, LANES))
                p = jnp.exp2(s - row_max)
                l_new = jnp.broadcast_to(jnp.sum(p, axis=-1, keepdims=True), (2 * Q_BLOCK, LANES))
                pv = _dot(p.astype(BF16), b["v"]().astype(BF16))
                return m_new, l_new, jnp.where(head0, pv[:Q_BLOCK], pv[Q_BLOCK:])
            m_new = jnp.maximum(b["m"], row_max)
            alpha = jnp.exp2(b["m"] - m_new)
            p = jnp.exp2(s - jnp.concatenate([m_new, m_new], axis=1))
            l_new = alpha * b["l"] + jnp.sum(p, axis=-1, keepdims=True)
            pv = _dot(p.astype(BF16), b["v"]().astype(BF16))
            acc_new = jnp.where(head0, alpha[:Q_BLOCK] * b["acc"] + pv[:Q_BLOCK],
                                alpha[Q_BLOCK:] * b["acc"] + pv[Q_BLOCK:])
            return m_new, l_new, acc_new

        def store(b, res):
            m_new, l_new, acc_new = res
            rows = b["rows"]
            acc_ref[rows, :] = acc_new
            m_ref[0, rows, :] = m_new[:Q_BLOCK]
            m_ref[1, rows, :] = m_new[Q_BLOCK:]
            l_ref[0, rows, :] = l_new[:Q_BLOCK]
            l_ref[1, rows, :] = l_new[Q_BLOCK:]

        def some_blocks(i, carry, load=load, compute=compute, store=store,
                        part=n_iter // A_BLOCKS_PER_ITER):
            blocks = [load(i + c * part) for c in range(A_BLOCKS_PER_ITER)]
            results = [compute(b) for b in blocks]
            for b, res in zip(blocks, results):
                store(b, res)
            return carry

        lax.fori_loop(0, n_iter // A_BLOCKS_PER_ITER, some_blocks, 0)

    lane_c = lax.broadcasted_iota(jnp.int32, (A_CHUNK, LANES), 1)
    denom = jnp.where(lane_c < HEAD_DIM, l_ref[0], l_ref[1])
    o_ref[0] = (acc_ref[...] / denom).astype(BF16)


def _dilated_attention(qa, ka, va, bias_a):
    b, s, wa = qa.shape
    n_pairs = wa // LANES
    cur = pl.BlockSpec((1, A_CHUNK, LANES), lambda i, c, p: (i, c, p))
    prev = pl.BlockSpec((1, A_CHUNK, LANES), lambda i, c, p: (i, jnp.maximum(c - 1, 0), p))
    nb = len(DILATED_BRANCHES)
    bias = pl.BlockSpec((nb, 2, 1, 2 * Q_BLOCK, 2 * Q_BLOCK), lambda i, c, p: (0, 0, p, 0, 0))
    return pl.pallas_call(
        _dilated_kernel,
        out_shape=jax.ShapeDtypeStruct((b, s, wa), BF16),
        grid=(b, s // A_CHUNK, n_pairs),
        in_specs=[cur, prev, cur, prev, cur, bias],
        out_specs=cur,
        scratch_shapes=[pltpu.VMEM((2 * A_CHUNK, LANES), F32),
                        pltpu.VMEM((2 * A_CHUNK, LANES), F32),
                        pltpu.VMEM((2, A_CHUNK, LANES), F32),
                        pltpu.VMEM((2, A_CHUNK, LANES), F32),
                        pltpu.VMEM((A_CHUNK, LANES), F32)],
        compiler_params=_cparams("parallel", "parallel", "parallel"),
        name="dilated_attention",
    )(qa, ka, ka, va, va, bias_a)


def _diff_kernel(n_bias, qt_ref, k_ref, vt_ref, bias_ref, lam_ref, gain_ref, o_ref,
                 q4_ref, m_ref, mt_ref, acc_ref, s_ref, p_ref):
    t = ATT_TILE
    row = lax.broadcasted_iota(jnp.int32, (LANES, t), 0)
    ones = jnp.ones((ONES_ROWS, 2 * t), BF16)

    def pipeline(g):
        qi = pl.program_id(2) * QUERY_TILES + g

        def tile_of(step):
            return jnp.clip(qi - step, 0, qi)

        def init():
            qt = qt_ref[0, g]
            for c in range(4):
                sel = jnp.logical_and(row >= c * DIFF_HALF, row < (c + 1) * DIFF_HALF)
                q4_ref[g, :, c * t:(c + 1) * t] = jnp.where(sel, qt, jnp.zeros_like(qt))
            acc_ref[g] = jnp.zeros(acc_ref.shape[1:], F32)

        def values(i):
            vt = jnp.concatenate([vt_ref[0, tile_of(2 * i)], vt_ref[0, tile_of(2 * i + 1)]], axis=1)
            return [_dot(jnp.concatenate([vt[h * HEAD_DIM:(h + 1) * HEAD_DIM], ones], axis=0),
                         p_ref[g, :, h * 2 * t:(h + 1) * 2 * t]) for h in range(2)]

        def logits(i):
            tile_max = None
            for half in range(2):
                step = 2 * i + half
                k = k_ref[0, pl.ds(pl.multiple_of(tile_of(step) * t, t), t), :]
                d = jnp.where(step > qi, n_bias, jnp.minimum(step, n_bias - 1))
                b0 = bias_ref[d, 0]
                b1 = bias_ref[d, 1]
                s = _dot(k, q4_ref[g]) + jnp.concatenate([b0, b0, b1, b1], axis=1)
                s_ref[g, half * t:(half + 1) * t] = s
                mx = jnp.max(s, axis=0, keepdims=True)
                tile_max = mx if tile_max is None else jnp.maximum(tile_max, mx)
            mt_ref[g] = tile_max

        def sweep(i):
            pv = values(i - 1)
            m_old = m_ref[g]
            m_new = jnp.maximum(m_old, mt_ref[g])
            alpha = jnp.exp2(m_old - m_new)
            m_ref[g] = m_new
            p_ref[g] = jnp.exp2(s_ref[g] - m_new).astype(BF16)
            for h in range(2):
                acc_ref[g, h] = alpha[:, h * 2 * t:(h + 1) * 2 * t] * (acc_ref[g, h] + pv[h])
            logits(i + 1)

        def fill():
            init()
            logits(0)
            m_new = mt_ref[g]
            m_ref[g] = m_new
            p_ref[g] = jnp.exp2(s_ref[g] - m_new).astype(BF16)
            logits(1)

        def drain(n_iter):
            last = values(n_iter - 1)
            lam = lam_ref[...]
            outs = []
            for h in range(2):
                acc = acc_ref[g, h] + last[h]
                pv = acc[0:HEAD_DIM] / acc[HEAD_DIM:HEAD_DIM + 1]
                diff = pv[:, 0:t] - lam * pv[:, t:2 * t]
                ms = jnp.mean(diff * diff, axis=0, keepdims=True)
                outs.append(diff * lax.rsqrt(ms + EPS))
            out_t = jnp.concatenate(outs, axis=0) * gain_ref[...]
            o_ref[0, g * t:(g + 1) * t] = out_t.T.astype(BF16)

        return fill, sweep, drain

    stages = [pipeline(g) for g in range(QUERY_TILES)]
    for fill, _, _ in stages:
        fill()
    n_iter = (pl.program_id(2) * QUERY_TILES + QUERY_TILES + 1) // 2

    def body(i, c):
        for _, sweep, _ in stages:
            sweep(i)
        return c

    lax.fori_loop(1, n_iter, body, 0)
    for _, _, drain in stages:
        drain(n_iter)


def _diff_attention(qbt, kb, vbt, bias_bt, lam, post_gain):
    b, s, wb = kb.shape
    n_pairs = wb // LANES
    n_bias = bias_bt.shape[0] - 1
    t = ATT_TILE
    nt = s // t
    g = QUERY_TILES
    qspec = pl.BlockSpec((1, g, LANES, t), lambda i, p, j: (i, j, p, 0))
    kspec = pl.BlockSpec((1, s, LANES), lambda i, p, j: (i, 0, p))
    vspec = pl.BlockSpec((1, nt, LANES, t), lambda i, p, j: (i, 0, p, 0))
    bspec = pl.BlockSpec((n_bias + 1, 2, t, t), lambda i, p, j: (0, p, 0, 0))
    return pl.pallas_call(
        functools.partial(_diff_kernel, n_bias),
        out_shape=jax.ShapeDtypeStruct((b, s, wb), BF16),
        grid=(b, n_pairs, nt // g),
        in_specs=[qspec, kspec, vspec, bspec,
                  pl.BlockSpec((1, t), lambda i, p, j: (0, 0)),
                  pl.BlockSpec((LANES, t), lambda i, p, j: (0, 0))],
        out_specs=pl.BlockSpec((1, g * t, LANES), lambda i, p, j: (i, j, p)),
        scratch_shapes=[pltpu.VMEM((g, LANES, 4 * t), BF16),
                        pltpu.VMEM((g, 1, 4 * t), F32),
                        pltpu.VMEM((g, 1, 4 * t), F32),
                        pltpu.VMEM((g, 2, HEAD_DIM + ONES_ROWS, 2 * t), F32),
                        pltpu.VMEM((g, 2 * t, 4 * t), F32),
                        pltpu.VMEM((g, 2 * t, 4 * t), BF16)],
        compiler_params=_cparams("parallel", "parallel", "parallel"),
        name="diff_attention",
    )(qbt.reshape(b, nt, wb, t), kb, vbt.reshape(b, nt, wb, t), bias_bt, lam, post_gain)


def _stick_kernel(qt_ref, k_ref, vt_ref, tri_ref, o_ref, q2_ref, carry_ref, scale_ref, acc_ref,
                  z_ref, lw_ref, p_ref):
    t = ATT_TILE
    row = lax.broadcasted_iota(jnp.int32, (LANES, t), 0)

    def pipeline(g):
        qi = pl.program_id(2) * QUERY_TILES + g

        def tile_of(step):
            return jnp.clip(qi - step, 0, qi)

        def init():
            qt = qt_ref[0, g]
            zero = jnp.zeros_like(qt)
            q2_ref[g, :, 0:t] = jnp.where(row < HEAD_DIM, qt, zero)
            q2_ref[g, :, t:] = jnp.where(row < HEAD_DIM, zero, qt)
            carry_ref[g] = jnp.zeros(carry_ref.shape[1:], F32)
            acc_ref[g] = jnp.zeros(acc_ref.shape[1:], F32)

        def logits(i, diagonal=False):
            for half in range(2):
                k = k_ref[0, pl.ds(pl.multiple_of(tile_of(2 * i + half) * t, t), t), :]
                z = _dot(k, q2_ref[g])
                if diagonal and half == 0:
                    key = lax.broadcasted_iota(jnp.int32, (t, t), 0)
                    qry = lax.broadcasted_iota(jnp.int32, (t, t), 1)
                    strict = jnp.concatenate([key < qry, key < qry], axis=1)
                    z = jnp.where(strict, z, -SB_MASK * LOG2E)
                z_ref[g, half] = z

        def log_weights():
            for half in range(2):
                z = z_ref[g, half]
                neg_abs = pltpu.bitcast(pltpu.bitcast(z, jnp.uint32) | jnp.uint32(0x80000000), F32)
                sp = jnp.maximum(z, 0.0) + jnp.log(1.0 + jnp.exp2(neg_abs)) * LOG2E
                w = _dot(tri_ref[...], sp.astype(BF16))
                lw_ref[g, half, 0:t] = z + w[0:t]
                lw_ref[g, half, t:] = w[t:]

        def weights():
            carry = carry_ref[g]
            near_sum = lw_ref[g, 0, t:t + 1]
            p_ref[g, 0:t] = jnp.exp2(lw_ref[g, 0, 0:t]).astype(BF16)
            p_ref[g, t:] = jnp.exp2(lw_ref[g, 1, 0:t] + near_sum).astype(BF16)
            scale_ref[g] = jnp.exp2(carry)
            carry_ref[g] = carry + near_sum + lw_ref[g, 1, t:t + 1]

        def values(i):
            vts = []
            for half in range(2):
                step = 2 * i + half
                valid = jnp.logical_and(step >= 0, step <= qi)
                vt = vt_ref[0, tile_of(step)]
                vts.append(jnp.where(valid, vt, jnp.zeros_like(vt)))
            acc_ref[g] += _dot(jnp.concatenate(vts, axis=1), p_ref[g]) * scale_ref[g]

        def sweep(i):
            values(i - 1)
            weights()
            log_weights()
            logits(i + 2)

        def first_pair():
            init()
            logits(0, diagonal=True)
            log_weights()
            weights()

        def catch_up():
            logits(1)
            log_weights()
            logits(2)

        def drain(n_iter):
            values(n_iter - 1)
            out_t = jnp.concatenate([acc_ref[g, 0:HEAD_DIM, 0:t], acc_ref[g, HEAD_DIM:, t:]], axis=0)
            o_ref[0, g * t:(g + 1) * t] = out_t.T.astype(BF16)

        return first_pair, catch_up, sweep, drain

    stages = [pipeline(g) for g in range(QUERY_TILES)]
    for first_pair, _, _, _ in stages:
        first_pair()
    n_iter = (pl.program_id(2) * QUERY_TILES + QUERY_TILES + 1) // 2

    def live():
        return (jnp.max(jnp.exp2(carry_ref[...])) > 0.0).astype(jnp.int32)

    live_0 = live()

    @pl.when(live_0 > 0)
    def _():
        for _, catch_up, _, _ in stages:
            catch_up()

    def body(state):
        i, _ = state
        for _, _, sweep, _ in stages:
            sweep(i)
        return i + 1, live()

    n_done, _ = lax.while_loop(lambda st: jnp.logical_and(st[0] < n_iter, st[1] > 0), body,
                               (jnp.int32(1), live_0))
    for _, _, _, drain in stages:
        drain(n_done)


def _stick_attention(qct, kc, vct):
    b, s, wc = kc.shape
    n_pairs = wc // LANES
    t = ATT_TILE
    nt = s // t
    g = QUERY_TILES
    idx = np.arange(t)
    tri = np.concatenate([idx[None, :] >= idx[:, None], np.ones((ONES_ROWS, t), bool)], axis=0)
    tri = -jnp.asarray(tri, dtype=BF16)
    return pl.pallas_call(
        _stick_kernel,
        out_shape=jax.ShapeDtypeStruct((b, s, wc), BF16),
        grid=(b, n_pairs, nt // g),
        in_specs=[pl.BlockSpec((1, g, LANES, t), lambda i, p, j: (i, j, p, 0)),
                  pl.BlockSpec((1, s, LANES), lambda i, p, j: (i, 0, p)),
                  pl.BlockSpec((1, nt, LANES, t), lambda i, p, j: (i, 0, p, 0)),
                  pl.BlockSpec((t + ONES_ROWS, t), lambda i, p, j: (0, 0))],
        out_specs=pl.BlockSpec((1, g * t, LANES), lambda i, p, j: (i, j, p)),
        scratch_shapes=[pltpu.VMEM((g, LANES, 2 * t), BF16),
                        pltpu.VMEM((g, 1, 2 * t), F32),
                        pltpu.VMEM((g, 1, 2 * t), F32),
                        pltpu.VMEM((g, LANES, 2 * t), F32),
                        pltpu.VMEM((g, 2, t, 2 * t), F32),
                        pltpu.VMEM((g, 2, t + ONES_ROWS, 2 * t), F32),
                        pltpu.VMEM((g, 2 * t, 2 * t), BF16)],
        compiler_params=_cparams("parallel", "parallel", "parallel"),
        name="stick_breaking_attention",
    )(qct.reshape(b, nt, wc, t), kc, vct.reshape(b, nt, wc, t), tri)


def _out_proj_kernel(widths, n_ffn, x_ref, a_ref, b_ref, c_ref, w_ref, *rest):
    o_ref = rest[4 * n_ffn]
    wa, wb, wc = widths
    y = _dot(a_ref[...], w_ref[0:wa])
    y += _dot(b_ref[...], w_ref[wa:wa + wb])
    y += _dot(c_ref[...], w_ref[wa + wb:wa + wb + wc])
    x = x_ref[...] + y
    for f in range(n_ffn):
        g_ref, wg_ref, wu_ref, wd_ref = rest[4 * f:4 * f + 4]
        h = _rms(x, g_ref[...]).astype(BF16)
        gate = _dot(h, wg_ref[...])
        up = _dot(h, wu_ref[...])
        act = (gate * jax.nn.sigmoid(gate) * up).astype(BF16)
        x = x + 0.5 * _dot(act, wd_ref[...])
    o_ref[...] = x


def _out_proj(x2, oa, ob, oc, w_out, widths, ffns):
    t, d = x2.shape
    row = lambda w: pl.BlockSpec((ROW_TILE, w), lambda i: (i, 0))
    full = lambda a: pl.BlockSpec(a.shape, lambda i: (0, 0))
    ffn_args = []
    for gain, wg, wu, wd in ffns:
        ffn_args += [gain.reshape(1, d), wg, wu, wd]
    return pl.pallas_call(
        functools.partial(_out_proj_kernel, widths, len(ffns)),
        out_shape=jax.ShapeDtypeStruct((t, d), F32),
        grid=(t // ROW_TILE,),
        in_specs=[row(d), row(widths[0]), row(widths[1]), row(widths[2]), full(w_out)]
                 + [full(a) for a in ffn_args],
        out_specs=row(d),
        compiler_params=_cparams("parallel"),
        name="out_proj_residual_ffn",
    )(x2, oa, ob, oc, w_out, *ffn_args)


def _t5_bucket(dist):
    dist = jnp.maximum(dist, 0)
    max_exact = N_BUCKETS // 2
    d_f = jnp.maximum(dist, 1).astype(F32)
    large = max_exact + (jnp.log(d_f / max_exact) / math.log(MAX_DISTANCE / max_exact)
                         * (N_BUCKETS - max_exact)).astype(jnp.int32)
    large = jnp.minimum(large, N_BUCKETS - 1)
    return jnp.where(dist < max_exact, dist, large)


def _bias_of_distance(bias, dist):
    bucket = _t5_bucket(dist)[None]
    out = jnp.zeros((bias.shape[1],) + dist.shape, F32)
    for b in range(N_BUCKETS):
        out = jnp.where(bucket == b, bias[b].reshape((-1,) + (1,) * dist.ndim), out)
    return out


def _dilated_bias_table(bias_a):
    n_heads = bias_a.shape[1]
    i = jnp.arange(Q_BLOCK, dtype=jnp.int32)[:, None]
    j = jnp.arange(2 * Q_BLOCK, dtype=jnp.int32)[None, :]
    tables = []
    for window, dil in DILATED_BRANCHES:
        n = window // dil
        off = i + n - j
        band = (off >= 0) & (off <= n)
        bias = _bias_of_distance(bias_a, off * dil)
        variants = [jnp.where(valid[None], bias * LOG2E, NEG_INF)
                    for valid in (band, band & (j >= n))]
        tables.append(jnp.stack(variants))
    table = jnp.stack(tables)
    return table.reshape(len(DILATED_BRANCHES), 2, n_heads // 2, 2 * Q_BLOCK, 2 * Q_BLOCK)


def _diff_bias_table(bias_b, seq):
    t = ATT_TILE
    n_bias = min(seq // t, MAX_DISTANCE // t + 2)
    key = jnp.arange(t, dtype=jnp.int32)[None, :, None]
    qry = jnp.arange(t, dtype=jnp.int32)[None, None, :]
    dist = jnp.arange(n_bias + 1, dtype=jnp.int32)[:, None, None] * t + qry - key
    valid = (dist >= 0) & (jnp.arange(n_bias + 1)[:, None, None] < n_bias)
    tiles = jnp.where(valid[None], _bias_of_distance(bias_b, dist) * LOG2E, NEG_INF)
    return jnp.swapaxes(tiles, 0, 1)


def kernel(x, rel_bias, ffn1_norm, ffn1_w_gate, ffn1_w_up, ffn1_w_down, mix_norm, w_in,
           q_norm_a, k_norm_a, q_norm_b, k_norm_b, lambda_q1, lambda_k1, lambda_q2, lambda_k2,
           diff_subln, w_out, ffn2_norm, ffn2_w_gate, ffn2_w_up, ffn2_w_down):
    b, s, d = x.shape
    depth = w_in.shape[0]
    n_heads = d // HEAD_DIM
    wa = (n_heads // 2) * HEAD_DIM
    wb = (n_heads // 4) * HEAD_DIM
    wc = d - wa - wb
    widths = (wa, wb, wc)
    assert w_in.shape[2] == 3 * d and s % A_CHUNK == 0 and (b * s) % ROW_TILE == 0

    rb = rel_bias.astype(F32)
    bias_a = _dilated_bias_table(rb[:, :wa // HEAD_DIM])
    bias_bt = _diff_bias_table(rb[:, wa // HEAD_DIM:], s)

    def ffn1(layer):
        return (ffn1_norm[layer], ffn1_w_gate[layer].astype(BF16),
                ffn1_w_up[layer].astype(BF16), ffn1_w_down[layer].astype(BF16))

    def ffn2(layer):
        return (ffn2_norm[layer], ffn2_w_gate[layer].astype(BF16),
                ffn2_w_up[layer].astype(BF16), ffn2_w_down[layer].astype(BF16))

    x2 = _ffn(x.reshape(b * s, d), *ffn1(0))
    for layer in range(depth):
        qa, ka, va, kb, kc, qbt, vbt, qct, vct = _proj(
            x2, mix_norm[layer], w_in[layer], q_norm_a[layer], k_norm_a[layer],
            q_norm_b[layer], k_norm_b[layer], widths)
        seq3 = lambda t: t.reshape(b, s, t.shape[-1])

        out_a = _dilated_attention(seq3(qa), seq3(ka), seq3(va), bias_a)

        lam_init = 0.8 - 0.6 * math.exp(-0.3 * layer)
        lam = (jnp.exp(jnp.sum(lambda_q1[layer].astype(F32) * lambda_k1[layer].astype(F32)))
               - jnp.exp(jnp.sum(lambda_q2[layer].astype(F32) * lambda_k2[layer].astype(F32)))
               + lam_init)
        lam_row = jnp.full((1, ATT_TILE), lam, F32)
        post_gain = jnp.broadcast_to(
            (jnp.tile(diff_subln[layer].astype(F32), LANES // HEAD_DIM) * (1.0 - lam_init))[:, None],
            (LANES, ATT_TILE))
        out_b = _diff_attention(qbt, seq3(kb), vbt, bias_bt, lam_row, post_gain)

        out_c = _stick_attention(qct, seq3(kc), vct)

        ffns = [ffn2(layer)] + ([ffn1(layer + 1)] if layer + 1 < depth else [])
        x2 = _out_proj(x2, out_a.reshape(b * s, wa), out_b.reshape(b * s, wb),
                       out_c.reshape(b * s, wc), w_out[layer].astype(BF16), widths, ffns)
    return x2.reshape(b, s, d)
```

```python
import functools
import math

import jax
import jax.numpy as jnp
import numpy as np
from jax import lax
from jax.experimental import pallas as pl
from jax.experimental.pallas import tpu as pltpu

F32 = jnp.float32
BF16 = jnp.bfloat16

HEAD_DIM = 64
DIFF_HALF = HEAD_DIM // 2
N_BUCKETS = 32
MAX_DISTANCE = 2048
DILATED_BRANCHES = ((128, 1), (512, 4), (2048, 16))
Q_BLOCK = 128
EPS = 1e-6
NEG_INF = -1e30
SB_MASK = 1e4
LOG2E = math.log2(math.e)

LANES = 128
A_CHUNK = 2048
A_BLOCKS_PER_ITER = 16
ATT_TILE = 256
ROW_TILE = 512
ONES_ROWS = 16
QUERY_TILES = 2
VMEM_LIMIT = 56 * 1024 * 1024


def _cparams(*sem):
    return pltpu.CompilerParams(dimension_semantics=sem, vmem_limit_bytes=VMEM_LIMIT)


def _rms(x, gain_row):
    ms = jnp.mean(x * x, axis=-1, keepdims=True)
    return x * lax.rsqrt(ms + EPS) * gain_row


def _dot(a, b):
    return jnp.dot(a, b, preferred_element_type=F32)


def _dot_nt(a, b):
    return lax.dot_general(a, b, (((1,), (1,)), ((), ())), preferred_element_type=F32)


def _ffn_kernel(x_ref, g_ref, wg_ref, wu_ref, wd_ref, o_ref):
    x = x_ref[...]
    h = _rms(x, g_ref[...]).astype(BF16)
    gate = _dot(h, wg_ref[...])
    up = _dot(h, wu_ref[...])
    act = (gate * jax.nn.sigmoid(gate) * up).astype(BF16)
    o_ref[...] = x + 0.5 * _dot(act, wd_ref[...])


def _ffn(x2, gain, wg, wu, wd):
    t, d = x2.shape
    dff = wg.shape[1]
    row = pl.BlockSpec((ROW_TILE, d), lambda i: (i, 0))
    full = lambda shape: pl.BlockSpec(shape, lambda i: (0, 0))
    return pl.pallas_call(
        _ffn_kernel,
        out_shape=jax.ShapeDtypeStruct((t, d), F32),
        grid=(t // ROW_TILE,),
        in_specs=[row, full((1, d)), full((d, dff)), full((d, dff)), full((dff, d))],
        out_specs=row,
        compiler_params=_cparams("parallel"),
        name="ffn_half_step",
    )(x2, gain.reshape(1, d), wg, wu, wd)


def _proj_kernel(widths, x_ref, g_ref, wn_ref, wt_ref, gqa_ref, gka_ref, gkb_ref, gqb_ref,
                 g64_ref, g32_ref, qa_ref, ka_ref, va_ref, kb_ref, kc_ref,
                 qbt_ref, vbt_ref, qct_ref, vct_ref):
    wa, wb, wc = widths
    h = _rms(x_ref[...], g_ref[...]).astype(BF16)
    sub = 256

    def group_norm(t, ones_ref, gain_ref, group, scale):
        ss = _dot((t * t).astype(BF16), ones_ref[...])
        return t * lax.rsqrt(ss * (1.0 / group) + EPS) * (gain_ref[...] * scale)

    def group_norm_t(t, ones_ref, gain_ref, group, scale):
        ss = _dot(ones_ref[...], (t * t).astype(BF16))
        return t * lax.rsqrt(ss * (1.0 / group) + EPS) * (gain_ref[...] * scale)

    nat = _dot(h, wn_ref[...])
    col = 0
    for ref, width, post in (
            (qa_ref, wa, lambda t: group_norm(t, g64_ref, gqa_ref, HEAD_DIM,
                                              HEAD_DIM ** -0.5 * LOG2E)),
            (ka_ref, wa, lambda t: group_norm(t, g64_ref, gka_ref, HEAD_DIM, 1.0)),
            (va_ref, wa, lambda t: t),
            (kb_ref, wb, lambda t: group_norm(t, g32_ref, gkb_ref, DIFF_HALF, 1.0)),
            (kc_ref, wc, lambda t: t)):
        for c in range(0, width, sub):
            ref[:, c:c + sub] = post(nat[:, col + c:col + c + sub]).astype(ref.dtype)
        col += width

    tra = _dot_nt(wt_ref[...], h)
    row = 0
    for ref, width, post in (
            (qbt_ref, wb, lambda t: group_norm_t(t, g32_ref, gqb_ref, DIFF_HALF,
                                                 DIFF_HALF ** -0.5 * LOG2E)),
            (vbt_ref, wb, lambda t: t),
            (qct_ref, wc, lambda t: t * (HEAD_DIM ** -0.5 * LOG2E)),
            (vct_ref, wc, lambda t: t)):
        for r in range(0, width, sub):
            t = post(tra[row + r:row + r + sub]).astype(BF16)
            for j in range(ROW_TILE // ATT_TILE):
                ref[j, r:r + sub, :] = t[:, j * ATT_TILE:(j + 1) * ATT_TILE]
        row += width


def _block_diag_ones(group):
    idx = np.arange(256) // group
    return jnp.asarray(idx[:, None] == idx[None, :], dtype=BF16)


def _proj(x2, gain, w_in, gqa, gka, gqb, gkb, widths):
    t, d = x2.shape
    wa, wb, wc = widths
    assert wa % 256 == 0 and wb % 256 == 0 and wc % 256 == 0
    w = w_in.astype(BF16)
    o = np.cumsum([0, wa, wa, wa, wb, wb, wb, wc, wc, wc])
    sec = lambda i: w[:, o[i]:o[i + 1]]
    w_nat = jnp.concatenate([sec(0), sec(1), sec(2), sec(4), sec(7)], axis=1)
    w_tr = jnp.concatenate([sec(3), sec(5), sec(6), sec(8)], axis=1).T
    row = lambda wd: pl.BlockSpec((ROW_TILE, wd), lambda i: (i, 0))
    full = lambda shape: pl.BlockSpec(shape, lambda i: (0,) * len(shape))
    slab = lambda wd: pl.BlockSpec((ROW_TILE // ATT_TILE, wd, ATT_TILE), lambda i: (i, 0, 0))
    tile256 = lambda g: jnp.tile(g.astype(F32), 256 // g.shape[0])
    out_shape = ([jax.ShapeDtypeStruct((t, wa), F32)] * 3
                 + [jax.ShapeDtypeStruct((t, wb), BF16), jax.ShapeDtypeStruct((t, wc), BF16)]
                 + [jax.ShapeDtypeStruct((t // ATT_TILE, wd, ATT_TILE), BF16)
                    for wd in (wb, wb, wc, wc)])
    return pl.pallas_call(
        functools.partial(_proj_kernel, widths),
        out_shape=out_shape,
        grid=(t // ROW_TILE,),
        in_specs=[row(d), full((1, d)), full(w_nat.shape), full(w_tr.shape)]
                 + [full((1, 256))] * 3 + [full((256, 1))] + [full((256, 256))] * 2,
        out_specs=[row(wa)] * 3 + [row(wb), row(wc)] + [slab(wb), slab(wb), slab(wc), slab(wc)],
        compiler_params=_cparams("parallel"),
        name="norm_in_proj",
    )(x2, gain.reshape(1, d), w_nat, w_tr, tile256(gqa).reshape(1, 256), tile256(gka).reshape(1, 256),
      tile256(gkb).reshape(1, 256), tile256(gqb).reshape(256, 1),
      _block_diag_ones(HEAD_DIM), _block_diag_ones(DIFF_HALF))


def _dilated_kernel(q_ref, kp_ref, kc_ref, vp_ref, vc_ref, bias_ref, o_ref,
                    kk_ref, vv_ref, m_ref, l_ref, acc_ref):
    chunk = pl.program_id(1)
    kk_ref[0:A_CHUNK] = kp_ref[0]
    kk_ref[A_CHUNK:] = kc_ref[0]
    vv_ref[0:A_CHUNK] = vp_ref[0]
    vv_ref[A_CHUNK:] = vc_ref[0]

    lane = lax.broadcasted_iota(jnp.int32, (Q_BLOCK, LANES), 1)
    head0 = lane < HEAD_DIM

    order = sorted(range(len(DILATED_BRANCHES)), key=lambda b: -DILATED_BRANCHES[b][1])
    for bi in order:
        window, dil = DILATED_BRANCHES[bi]
        fresh = bi == order[0]
        nblk = A_CHUNK // (Q_BLOCK * dil)
        shift = int(math.log2(nblk))
        n_iter = dil * nblk
        ds = (lambda start, size, dil=dil:
              pl.ds(start, size, stride=dil) if dil > 1 else pl.ds(start, size))

        def load(idx, bi=bi, dil=dil, nblk=nblk, shift=shift, ds=ds, fresh=fresh):
            rho = idx >> shift
            t = idx & (nblk - 1)
            q_start = rho + t * (Q_BLOCK * dil)
            rows = ds(q_start, Q_BLOCK)
            krows = ds(A_CHUNK + q_start - Q_BLOCK * dil, 2 * Q_BLOCK)
            first = jnp.logical_and(chunk == 0, t == 0).astype(jnp.int32)
            blk = dict(rows=rows, q=q_ref[0, rows, :], k=kk_ref[krows, :], v=vv_ref[krows, :],
                       bias=bias_ref[bi, first, 0])
            if not fresh:
                blk.update(m=jnp.concatenate([m_ref[0, rows, :], m_ref[1, rows, :]], axis=0),
                           l=jnp.concatenate([l_ref[0, rows, :], l_ref[1, rows, :]], axis=0),
                           acc=acc_ref[rows, :])
            return blk

        def compute(b):
            q = b["q"]
            q2 = jnp.concatenate([jnp.where(head0, q, 0.0), jnp.where(head0, 0.0, q)],
                                 axis=0).astype(BF16)
            s = _dot_nt(q2, b["k"].astype(BF16)) + b["bias"]
            row_max = jnp.max(s, axis=-1, keepdims=True)
            if "m" not in b:
                m_new = jnp.broadcast_to(row_max, (2 * Q_BLOCK, LANES))
                p = jnp.exp2(s - row_max)
                l_new = jnp.broadcast_to(jnp.sum(p, axis=-1, keepdims=True), (2 * Q_BLOCK, LANES))
                pv = _dot(p.astype(BF16), b["v"].astype(BF16))
                return m_new, l_new, jnp.where(head0, pv[:Q_BLOCK], pv[Q_BLOCK:])
            m_new = jnp.maximum(b["m"], row_max)
            alpha = jnp.exp2(b["m"] - m_new)
            p = jnp.exp2(s - jnp.concatenate([m_new, m_new], axis=1))
            l_new = alpha * b["l"] + jnp.sum(p, axis=-1, keepdims=True)
            pv = _dot(p.astype(BF16), b["v"].astype(BF16))
            acc_new = jnp.where(head0, alpha[:Q_BLOCK] * b["acc"] + pv[:Q_BLOCK],
                                alpha[Q_BLOCK:] * b["acc"] + pv[Q_BLOCK:])
            return m_new, l_new, acc_new

        def store(b, res):
            m_new, l_new, acc_new = res
            rows = b["rows"]
            acc_ref[rows, :] = acc_new
            m_ref[0, rows, :] = m_new[:Q_BLOCK]
            m_ref[1, rows, :] = m_new[Q_BLOCK:]
            l_ref[0, rows, :] = l_new[:Q_BLOCK]
            l_ref[1, rows, :] = l_new[Q_BLOCK:]

        def some_blocks(i, carry, load=load, compute=compute, store=store,
                        part=n_iter // A_BLOCKS_PER_ITER):
            blocks = [load(i + c * part) for c in range(A_BLOCKS_PER_ITER)]
            results = [compute(b) for b in blocks]
            for b, res in zip(blocks, results):
                store(b, res)
            return carry

        lax.fori_loop(0, n_iter // A_BLOCKS_PER_ITER, some_blocks, 0)

    lane_c = lax.broadcasted_iota(jnp.int32, (A_CHUNK, LANES), 1)
    denom = jnp.where(lane_c < HEAD_DIM, l_ref[0], l_ref[1])
    o_ref[0] = (acc_ref[...] / denom).astype(BF16)


def _dilated_attention(qa, ka, va, bias_a):
    b, s, wa = qa.shape
    n_pairs = wa // LANES
    cur = pl.BlockSpec((1, A_CHUNK, LANES), lambda i, c, p: (i, c, p))
    prev = pl.BlockSpec((1, A_CHUNK, LANES), lambda i, c, p: (i, jnp.maximum(c - 1, 0), p))
    nb = len(DILATED_BRANCHES)
    bias = pl.BlockSpec((nb, 2, 1, 2 * Q_BLOCK, 2 * Q_BLOCK), lambda i, c, p: (0, 0, p, 0, 0))
    return pl.pallas_call(
        _dilated_kernel,
        out_shape=jax.ShapeDtypeStruct((b, s, wa), BF16),
        grid=(b, s // A_CHUNK, n_pairs),
        in_specs=[cur, prev, cur, prev, cur, bias],
        out_specs=cur,
        scratch_shapes=[pltpu.VMEM((2 * A_CHUNK, LANES), F32),
                        pltpu.VMEM((2 * A_CHUNK, LANES), F32),
                        pltpu.VMEM((2, A_CHUNK, LANES), F32),
                        pltpu.VMEM((2, A_CHUNK, LANES), F32),
                        pltpu.VMEM((A_CHUNK, LANES), F32)],
        compiler_params=_cparams("parallel", "parallel", "parallel"),
        name="dilated_attention",
    )(qa, ka, ka, va, va, bias_a)


def _diff_kernel(n_bias, qt_ref, k_ref, vt_ref, bias_ref, lam_ref, gain_ref, o_ref,
                 q4_ref, m_ref, mt_ref, acc_ref, s_ref, p_ref):
    t = ATT_TILE
    row = lax.broadcasted_iota(jnp.int32, (LANES, t), 0)
    ones = jnp.ones((ONES_ROWS, 2 * t), BF16)

    def pipeline(g):
        qi = pl.program_id(2) * QUERY_TILES + g

        def tile_of(step):
            return jnp.clip(qi - step, 0, qi)

        def init():
            qt = qt_ref[0, g]
            for c in range(4):
                sel = jnp.logical_and(row >= c * DIFF_HALF, row < (c + 1) * DIFF_HALF)
                q4_ref[g, :, c * t:(c + 1) * t] = jnp.where(sel, qt, jnp.zeros_like(qt))
            acc_ref[g] = jnp.zeros(acc_ref.shape[1:], F32)

        def values(i):
            vt = jnp.concatenate([vt_ref[0, tile_of(2 * i)], vt_ref[0, tile_of(2 * i + 1)]], axis=1)
            return [_dot(jnp.concatenate([vt[h * HEAD_DIM:(h + 1) * HEAD_DIM], ones], axis=0),
                         p_ref[g, :, h * 2 * t:(h + 1) * 2 * t]) for h in range(2)]

        def logits(i):
            tile_max = None
            for half in range(2):
                step = 2 * i + half
                k = k_ref[0, pl.ds(pl.multiple_of(tile_of(step) * t, t), t), :]
                d = jnp.where(step > qi, n_bias, jnp.minimum(step, n_bias - 1))
                b0 = bias_ref[d, 0]
                b1 = bias_ref[d, 1]
                s = _dot(k, q4_ref[g]) + jnp.concatenate([b0, b0, b1, b1], axis=1)
                s_ref[g, half * t:(half + 1) * t] = s
                mx = jnp.max(s, axis=0, keepdims=True)
                tile_max = mx if tile_max is None else jnp.maximum(tile_max, mx)
            mt_ref[g] = tile_max

        def sweep(i):
            pv = values(i - 1)
            m_old = m_ref[g]
            m_new = jnp.maximum(m_old, mt_ref[g])
            alpha = jnp.exp2(m_old - m_new)
            m_ref[g] = m_new
            p_ref[g] = jnp.exp2(s_ref[g] - m_new).astype(BF16)
            for h in range(2):
                acc_ref[g, h] = alpha[:, h * 2 * t:(h + 1) * 2 * t] * (acc_ref[g, h] + pv[h])
            logits(i + 1)

        def fill():
            init()
            logits(0)
            m_new = mt_ref[g]
            m_ref[g] = m_new
            p_ref[g] = jnp.exp2(s_ref[g] - m_new).astype(BF16)
            logits(1)

        def drain(n_iter):
            last = values(n_iter - 1)
            lam = lam_ref[...]
            outs = []
            for h in range(2):
                acc = acc_ref[g, h] + last[h]
                pv = acc[0:HEAD_DIM] / acc[HEAD_DIM:HEAD_DIM + 1]
                diff = pv[:, 0:t] - lam * pv[:, t:2 * t]
                ms = jnp.mean(diff * diff, axis=0, keepdims=True)
                outs.append(diff * lax.rsqrt(ms + EPS))
            out_t = jnp.concatenate(outs, axis=0) * gain_ref[...]
            o_ref[0, g * t:(g + 1) * t] = out_t.T.astype(BF16)

        return fill, sweep, drain

    stages = [pipeline(g) for g in range(QUERY_TILES)]
    for fill, _, _ in stages:
        fill()
    n_iter = (pl.program_id(2) * QUERY_TILES + QUERY_TILES + 1) // 2

    def body(i, c):
        for _, sweep, _ in stages:
            sweep(i)
        return c

    lax.fori_loop(1, n_iter, body, 0)
    for _, _, drain in stages:
        drain(n_iter)


def _diff_attention(qbt, kb, vbt, bias_bt, lam, post_gain):
    b, s, wb = kb.shape
    n_pairs = wb // LANES
    n_bias = bias_bt.shape[0] - 1
    t = ATT_TILE
    nt = s // t
    g = QUERY_TILES
    qspec = pl.BlockSpec((1, g, LANES, t), lambda i, p, j: (i, j, p, 0))
    kspec = pl.BlockSpec((1, s, LANES), lambda i, p, j: (i, 0, p))
    vspec = pl.BlockSpec((1, nt, LANES, t), lambda i, p, j: (i, 0, p, 0))
    bspec = pl.BlockSpec((n_bias + 1, 2, t, t), lambda i, p, j: (0, p, 0, 0))
    return pl.pallas_call(
        functools.partial(_diff_kernel, n_bias),
        out_shape=jax.ShapeDtypeStruct((b, s, wb), BF16),
        grid=(b, n_pairs, nt // g),
        in_specs=[qspec, kspec, vspec, bspec,
                  pl.BlockSpec((1, t), lambda i, p, j: (0, 0)),
                  pl.BlockSpec((LANES, t), lambda i, p, j: (0, 0))],
        out_specs=pl.BlockSpec((1, g * t, LANES), lambda i, p, j: (i, j, p)),
        scratch_shapes=[pltpu.VMEM((g, LANES, 4 * t), BF16),
                        pltpu.VMEM((g, 1, 4 * t), F32),
                        pltpu.VMEM((g, 1, 4 * t), F32),
                        pltpu.VMEM((g, 2, HEAD_DIM + ONES_ROWS, 2 * t), F32),
                        pltpu.VMEM((g, 2 * t, 4 * t), F32),
                        pltpu.VMEM((g, 2 * t, 4 * t), BF16)],
        compiler_params=_cparams("parallel", "parallel", "parallel"),
        name="diff_attention",
    )(qbt.reshape(b, nt, wb, t), kb, vbt.reshape(b, nt, wb, t), bias_bt, lam, post_gain)


def _stick_kernel(qt_ref, k_ref, vt_ref, tri_ref, o_ref, q2_ref, carry_ref, scale_ref, acc_ref,
                  z_ref, lw_ref, p_ref):
    t = ATT_TILE
    row = lax.broadcasted_iota(jnp.int32, (LANES, t), 0)

    def pipeline(g):
        qi = pl.program_id(2) * QUERY_TILES + g

        def tile_of(step):
            return jnp.clip(qi - step, 0, qi)

        def init():
            qt = qt_ref[0, g]
            zero = jnp.zeros_like(qt)
            q2_ref[g, :, 0:t] = jnp.where(row < HEAD_DIM, qt, zero)
            q2_ref[g, :, t:] = jnp.where(row < HEAD_DIM, zero, qt)
            carry_ref[g] = jnp.zeros(carry_ref.shape[1:], F32)
            acc_ref[g] = jnp.zeros(acc_ref.shape[1:], F32)

        def logits(i, diagonal=False):
            for half in range(2):
                k = k_ref[0, pl.ds(pl.multiple_of(tile_of(2 * i + half) * t, t), t), :]
                z = _dot(k, q2_ref[g])
                if diagonal and half == 0:
                    key = lax.broadcasted_iota(jnp.int32, (t, t), 0)
                    qry = lax.broadcasted_iota(jnp.int32, (t, t), 1)
                    strict = jnp.concatenate([key < qry, key < qry], axis=1)
                    z = jnp.where(strict, z, -SB_MASK * LOG2E)
                z_ref[g, half] = z

        def log_weights():
            for half in range(2):
                z = z_ref[g, half]
                neg_abs = pltpu.bitcast(pltpu.bitcast(z, jnp.uint32) | jnp.uint32(0x80000000), F32)
                sp = jnp.maximum(z, 0.0) + jnp.log(1.0 + jnp.exp2(neg_abs)) * LOG2E
                w = _dot(tri_ref[...], sp.astype(BF16))
                lw_ref[g, half, 0:t] = z + w[0:t]
                lw_ref[g, half, t:] = w[t:]

        def weights():
            carry = carry_ref[g]
            near_sum = lw_ref[g, 0, t:t + 1]
            p_ref[g, 0:t] = jnp.exp2(lw_ref[g, 0, 0:t]).astype(BF16)
            p_ref[g, t:] = jnp.exp2(lw_ref[g, 1, 0:t] + near_sum).astype(BF16)
            scale_ref[g] = jnp.exp2(carry)
            carry_ref[g] = carry + near_sum + lw_ref[g, 1, t:t + 1]

        def values(i):
            vts = []
            for half in range(2):
                step = 2 * i + half
                valid = jnp.logical_and(step >= 0, step <= qi)
                vt = vt_ref[0, tile_of(step)]
                vts.append(jnp.where(valid, vt, jnp.zeros_like(vt)))
            acc_ref[g] += _dot(jnp.concatenate(vts, axis=1), p_ref[g]) * scale_ref[g]

        def sweep(i):
            values(i - 1)
            weights()
            log_weights()
            logits(i + 2)

        def first_pair():
            init()
            logits(0, diagonal=True)
            log_weights()
            weights()

        def catch_up():
            logits(1)
            log_weights()
            logits(2)

        def drain(n_iter):
            values(n_iter - 1)
            out_t = jnp.concatenate([acc_ref[g, 0:HEAD_DIM, 0:t], acc_ref[g, HEAD_DIM:, t:]], axis=0)
            o_ref[0, g * t:(g + 1) * t] = out_t.T.astype(BF16)

        return first_pair, catch_up, sweep, drain

    stages = [pipeline(g) for g in range(QUERY_TILES)]
    for first_pair, _, _, _ in stages:
        first_pair()
    n_iter = (pl.program_id(2) * QUERY_TILES + QUERY_TILES + 1) // 2

    def live():
        return (jnp.max(jnp.exp2(carry_ref[...])) > 0.0).astype(jnp.int32)

    live_0 = live()

    @pl.when(live_0 > 0)
    def _():
        for _, catch_up, _, _ in stages:
            catch_up()

    def body(state):
        i, _ = state
        for _, _, sweep, _ in stages:
            sweep(i)
        return i + 1, live()

    n_done, _ = lax.while_loop(lambda st: jnp.logical_and(st[0] < n_iter, st[1] > 0), body,
                               (jnp.int32(1), live_0))
    for _, _, _, drain in stages:
        drain(n_done)


def _stick_attention(qct, kc, vct):
    b, s, wc = kc.shape
    n_pairs = wc // LANES
    t = ATT_TILE
    nt = s // t
    g = QUERY_TILES
    idx = np.arange(t)
    tri = np.concatenate([idx[None, :] >= idx[:, None], np.ones((ONES_ROWS, t), bool)], axis=0)
    tri = -jnp.asarray(tri, dtype=BF16)
    return pl.pallas_call(
        _stick_kernel,
        out_shape=jax.ShapeDtypeStruct((b, s, wc), BF16),
        grid=(b, n_pairs, nt // g),
        in_specs=[pl.BlockSpec((1, g, LANES, t), lambda i, p, j: (i, j, p, 0)),
                  pl.BlockSpec((1, s, LANES), lambda i, p, j: (i, 0, p)),
                  pl.BlockSpec((1, nt, LANES, t), lambda i, p, j: (i, 0, p, 0)),
                  pl.BlockSpec((t + ONES_ROWS, t), lambda i, p, j: (0, 0))],
        out_specs=pl.BlockSpec((1, g * t, LANES), lambda i, p, j: (i, j, p)),
        scratch_shapes=[pltpu.VMEM((g, LANES, 2 * t), BF16),
                        pltpu.VMEM((g, 1, 2 * t), F32),
                        pltpu.VMEM((g, 1, 2 * t), F32),
                        pltpu.VMEM((g, LANES, 2 * t), F32),
                        pltpu.VMEM((g, 2, t, 2 * t), F32),
                        pltpu.VMEM((g, 2, t + ONES_ROWS, 2 * t), F32),
                        pltpu.VMEM((g, 2 * t, 2 * t), BF16)],
        compiler_params=_cparams("parallel", "parallel", "parallel"),
        name="stick_breaking_attention",
    )(qct.reshape(b, nt, wc, t), kc, vct.reshape(b, nt, wc, t), tri)


def _out_proj_kernel(widths, n_ffn, x_ref, a_ref, b_ref, c_ref, w_ref, *rest):
    o_ref = rest[4 * n_ffn]
    wa, wb, wc = widths
    y = _dot(a_ref[...], w_ref[0:wa])
    y += _dot(b_ref[...], w_ref[wa:wa + wb])
    y += _dot(c_ref[...], w_ref[wa + wb:wa + wb + wc])
    x = x_ref[...] + y
    for f in range(n_ffn):
        g_ref, wg_ref, wu_ref, wd_ref = rest[4 * f:4 * f + 4]
        h = _rms(x, g_ref[...]).astype(BF16)
        half = wg_ref.shape[1] // 2
        y = None
        for c in (0, half):
            gate = _dot(h, wg_ref[:, c:c + half])
            up = _dot(h, wu_ref[:, c:c + half])
            act = (gate * jax.nn.sigmoid(gate) * up).astype(BF16)
            part = _dot(act, wd_ref[c:c + half])
            y = part if y is None else y + part
        x = x + 0.5 * y
    o_ref[...] = x


def _out_proj(x2, oa, ob, oc, w_out, widths, ffns):
    t, d = x2.shape
    row = lambda w: pl.BlockSpec((ROW_TILE, w), lambda i: (i, 0))
    full = lambda a: pl.BlockSpec(a.shape, lambda i: (0, 0))
    ffn_args = []
    for gain, wg, wu, wd in ffns:
        ffn_args += [gain.reshape(1, d), wg, wu, wd]
    return pl.pallas_call(
        functools.partial(_out_proj_kernel, widths, len(ffns)),
        out_shape=jax.ShapeDtypeStruct((t, d), F32),
        grid=(t // ROW_TILE,),
        in_specs=[row(d), row(widths[0]), row(widths[1]), row(widths[2]), full(w_out)]
                 + [full(a) for a in ffn_args],
        out_specs=row(d),
        compiler_params=_cparams("parallel"),
        name="out_proj_residual_ffn",
    )(x2, oa, ob, oc, w_out, *ffn_args)


def _t5_bucket(dist):
    dist = jnp.maximum(dist, 0)
    max_exact = N_BUCKETS // 2
    d_f = jnp.maximum(dist, 1).astype(F32)
    large = max_exact + (jnp.log(d_f / max_exact) / math.log(MAX_DISTANCE / max_exact)
                         * (N_BUCKETS - max_exact)).astype(jnp.int32)
    large = jnp.minimum(large, N_BUCKETS - 1)
    return jnp.where(dist < max_exact, dist, large)


def _bias_of_distance(bias, dist):
    bucket = _t5_bucket(dist)[None]
    out = jnp.zeros((bias.shape[1],) + dist.shape, F32)
    for b in range(N_BUCKETS):
        out = jnp.where(bucket == b, bias[b].reshape((-1,) + (1,) * dist.ndim), out)
    return out


def _dilated_bias_table(bias_a):
    n_heads = bias_a.shape[1]
    i = jnp.arange(Q_BLOCK, dtype=jnp.int32)[:, None]
    j = jnp.arange(2 * Q_BLOCK, dtype=jnp.int32)[None, :]
    tables = []
    for window, dil in DILATED_BRANCHES:
        n = window // dil
        off = i + n - j
        band = (off >= 0) & (off <= n)
        bias = _bias_of_distance(bias_a, off * dil)
        variants = [jnp.where(valid[None], bias * LOG2E, NEG_INF)
                    for valid in (band, band & (j >= n))]
        tables.append(jnp.stack(variants))
    table = jnp.stack(tables)
    return table.reshape(len(DILATED_BRANCHES), 2, n_heads // 2, 2 * Q_BLOCK, 2 * Q_BLOCK)


def _diff_bias_table(bias_b, seq):
    t = ATT_TILE
    n_bias = min(seq // t, MAX_DISTANCE // t + 2)
    key = jnp.arange(t, dtype=jnp.int32)[None, :, None]
    qry = jnp.arange(t, dtype=jnp.int32)[None, None, :]
    dist = jnp.arange(n_bias + 1, dtype=jnp.int32)[:, None, None] * t + qry - key
    valid = (dist >= 0) & (jnp.arange(n_bias + 1)[:, None, None] < n_bias)
    tiles = jnp.where(valid[None], _bias_of_distance(bias_b, dist) * LOG2E, NEG_INF)
    return jnp.swapaxes(tiles, 0, 1)


def kernel(x, rel_bias, ffn1_norm, ffn1_w_gate, ffn1_w_up, ffn1_w_down, mix_norm, w_in,
           q_norm_a, k_norm_a, q_norm_b, k_norm_b, lambda_q1, lambda_k1, lambda_q2, lambda_k2,
           diff_subln, w_out, ffn2_norm, ffn2_w_gate, ffn2_w_up, ffn2_w_down):
    b, s, d = x.shape
    depth = w_in.shape[0]
    n_heads = d // HEAD_DIM
    wa = (n_heads // 2) * HEAD_DIM
    wb = (n_heads // 4) * HEAD_DIM
    wc = d - wa - wb
    widths = (wa, wb, wc)
    assert w_in.shape[2] == 3 * d and s % A_CHUNK == 0 and (b * s) % ROW_TILE == 0

    rb = rel_bias.astype(F32)
    bias_a = _dilated_bias_table(rb[:, :wa // HEAD_DIM])
    bias_bt = _diff_bias_table(rb[:, wa // HEAD_DIM:], s)

    def ffn1(layer):
        return (ffn1_norm[layer], ffn1_w_gate[layer].astype(BF16),
                ffn1_w_up[layer].astype(BF16), ffn1_w_down[layer].astype(BF16))

    def ffn2(layer):
        return (ffn2_norm[layer], ffn2_w_gate[layer].astype(BF16),
                ffn2_w_up[layer].astype(BF16), ffn2_w_down[layer].astype(BF16))

    x2 = _ffn(x.reshape(b * s, d), *ffn1(0))
    for layer in range(depth):
        qa, ka, va, kb, kc, qbt, vbt, qct, vct = _proj(
            x2, mix_norm[layer], w_in[layer], q_norm_a[layer], k_norm_a[layer],
            q_norm_b[layer], k_norm_b[layer], widths)
        seq3 = lambda t: t.reshape(b, s, t.shape[-1])

        out_a = _dilated_attention(seq3(qa), seq3(ka), seq3(va), bias_a)

        lam_init = 0.8 - 0.6 * math.exp(-0.3 * layer)
        lam = (jnp.exp(jnp.sum(lambda_q1[layer].astype(F32) * lambda_k1[layer].astype(F32)))
               - jnp.exp(jnp.sum(lambda_q2[layer].astype(F32) * lambda_k2[layer].astype(F32)))
               + lam_init)
        lam_row = jnp.full((1, ATT_TILE), lam, F32)
        post_gain = jnp.broadcast_to(
            (jnp.tile(diff_subln[layer].astype(F32), LANES // HEAD_DIM) * (1.0 - lam_init))[:, None],
            (LANES, ATT_TILE))
        out_b = _diff_attention(qbt, seq3(kb), vbt, bias_bt, lam_row, post_gain)

        out_c = _stick_attention(qct, seq3(kc), vct)

        ffns = [ffn2(layer)] + ([ffn1(layer + 1)] if layer + 1 < depth else [])
        x2 = _out_proj(x2, out_a.reshape(b * s, wa), out_b.reshape(b * s, wb),
                       out_c.reshape(b * s, wc), w_out[layer].astype(BF16), widths, ffns)
    return x2.reshape(b, s, d)
```
